```python
import math
import jax, jax.numpy as jnp
from jax import lax
import numpy as np

D_MODEL = 2048
BATCH = 2
SEQ = 8192
DEPTH = 2
DEC_BATCH = 16
DEC_SEQ = 64
PAST_LEN = 4096

CHUNK = 64
N_A = DEPTH // 2
N_B = DEPTH - N_A
CONV_W = 3
CONV_DIM = 3 * D_MODEL // 4
HEAD_DIM = 128
N_HEADS_F = CONV_DIM // HEAD_DIM
F_DIM = N_HEADS_F * HEAD_DIM
N_MEM = 256
N_HEADS_M = 4
M_DIM = N_HEADS_M * HEAD_DIM
Q_BLOCK = 128
IN_A = 4 * CONV_DIM + 2 * M_DIM
IN_B = 2 * F_DIM + 2 * M_DIM
MIX_DIM = CONV_DIM + M_DIM
KVF_DIM = 2 * F_DIM + N_HEADS_F
EPS = 1e-6

kernel_name = 'yoco_conv_fox_memory_stream_step'


def rmsnorm(x, g):
    xf = x.astype(jnp.float32)
    y = xf * lax.rsqrt(jnp.mean(xf * xf, axis=-1, keepdims=True) + EPS)
    return (y * g.astype(jnp.float32)).astype(x.dtype)


def causal_depthwise_conv(u, state, w):
    up = jnp.concatenate([state.astype(u.dtype), u], axis=1)
    y = lax.conv_general_dilated(up, w.astype(u.dtype)[:, None, :], window_strides=(1,),
                                 padding='VALID', dimension_numbers=('NWC', 'WIO', 'NWC'),
                                 feature_group_count=u.shape[-1])
    return y, up[:, -(CONV_W - 1):]


def memory_kv(mem, g, w):
    b, n = mem.shape[0], mem.shape[1]
    k, v = jnp.split(rmsnorm(mem, g) @ w, 2, axis=-1)
    return k.reshape(b, n, N_HEADS_M, HEAD_DIM), v.reshape(b, n, N_HEADS_M, HEAD_DIM)


def memory_attention(q, mk, mv):
    s = jnp.einsum('bthd,bmhd->bhtm', q, mk.astype(q.dtype),
                   preferred_element_type=jnp.float32) / math.sqrt(HEAD_DIM)
    p = jax.nn.softmax(s, axis=-1)
    return jnp.einsum('bhtm,bmhd->bthd', p.astype(q.dtype), mv.astype(q.dtype))


def fox_block(q, k, v, c_q, c_k, pos_q, pos_k):
    s = jnp.einsum('bqhd,bkhd->bhqk', q, k.astype(q.dtype),
                   preferred_element_type=jnp.float32) / math.sqrt(HEAD_DIM)
    decay = jnp.swapaxes(c_q, 1, 2)[..., :, None] - jnp.swapaxes(c_k, 1, 2)[..., None, :]
    mask = pos_k[None, :] <= pos_q[:, None]
    s = jnp.where(mask, s + decay, -jnp.inf)
    p = jax.nn.softmax(s, axis=-1)
    return jnp.einsum('bhqk,bkhd->bqhd', p.astype(q.dtype), v.astype(q.dtype))


def forgetting_attention(q, k_all, v_all, c_all, past_len):
    bsz, t = q.shape[0], q.shape[1]
    tk = k_all.shape[1]
    pos_k = jnp.arange(tk)
    pos_q = past_len + jnp.arange(t)
    c_q = c_all[:, past_len:]
    if t <= Q_BLOCK:
        return fox_block(q, k_all, v_all, c_q, c_all, pos_q, pos_k)
    nb = t // Q_BLOCK
    qb = q.reshape(bsz, nb, Q_BLOCK, N_HEADS_F, HEAD_DIM).transpose(1, 0, 2, 3, 4)
    cb = c_q.reshape(bsz, nb, Q_BLOCK, N_HEADS_F).transpose(1, 0, 2, 3)
    pb = pos_q.reshape(nb, Q_BLOCK)
    out = lax.map(lambda a: fox_block(a[0], k_all, v_all, a[1], c_all, a[2], pos_k), (qb, cb, pb))
    return out.transpose(1, 0, 2, 3, 4).reshape(bsz, t, N_HEADS_F, HEAD_DIM)


def trunk(x, conv_state, past_k, past_v, past_logf, mem_k, mem_v,
          g_norm, w_in_a, conv_w, w_in_b, w_out, g_kv, w_kvf, b_f, g_final):
    bsz, t = x.shape[0], x.shape[1]
    past_len = past_k.shape[1]
    new_conv = []
    k_new = v_new = logf_new = None
    k_all = v_all = c_all = None
    for i in range(DEPTH):
        h = rmsnorm(x, g_norm[i])
        if i < N_A:
            proj = h @ w_in_a[i]
            b_g, c_g, u, z_c, q_m, z_m = jnp.split(
                proj, [CONV_DIM, 2 * CONV_DIM, 3 * CONV_DIM, 4 * CONV_DIM, 4 * CONV_DIM + M_DIM], axis=-1)
            conv_out, st = causal_depthwise_conv(c_g * u, conv_state[i], conv_w[i])
            new_conv.append(st)
            branch = b_g * conv_out * jax.nn.silu(z_c)
        else:
            if i == N_A:
                kvf = rmsnorm(x, g_kv) @ w_kvf
                k_flat, v_flat, f_logit = jnp.split(kvf, [F_DIM, 2 * F_DIM], axis=-1)
                k_new = k_flat.reshape(bsz, t, N_HEADS_F, HEAD_DIM)
                v_new = v_flat.reshape(bsz, t, N_HEADS_F, HEAD_DIM)
                logf_new = jax.nn.log_sigmoid(f_logit.astype(jnp.float32) + b_f.astype(jnp.float32))
                k_all = jnp.concatenate([past_k.astype(k_new.dtype), k_new], axis=1)
                v_all = jnp.concatenate([past_v.astype(v_new.dtype), v_new], axis=1)
                c_all = jnp.cumsum(jnp.concatenate([past_logf.astype(jnp.float32), logf_new], axis=1), axis=1)
            proj = h @ w_in_b[i - N_A]
            q_f, z_f, q_m, z_m = jnp.split(proj, [F_DIM, 2 * F_DIM, 2 * F_DIM + M_DIM], axis=-1)
            o = forgetting_attention(q_f.reshape(bsz, t, N_HEADS_F, HEAD_DIM), k_all, v_all, c_all, past_len)
            branch = o.reshape(bsz, t, F_DIM) * jax.nn.silu(z_f)
        o_m = memory_attention(q_m.reshape(bsz, t, N_HEADS_M, HEAD_DIM), mem_k[i], mem_v[i])
        mixed = jnp.concatenate([branch, o_m.reshape(bsz, t, M_DIM) * jax.nn.silu(z_m)], axis=-1)
        x = x + mixed @ w_out[i]
    y = rmsnorm(x, g_final)
    return y, jnp.stack(new_conv), k_new, v_new, logf_new


def setup_inputs(seed: int = 0) -> dict:
    key = jax.random.key(seed)
    ks = jax.random.split(key, 20)

    def nrm(k, shape, scale=1.0):
        return jax.random.normal(k, shape, jnp.float32) * scale

    return {
        'x_prompt': nrm(ks[0], (BATCH, SEQ, D_MODEL)),
        'x_sample': nrm(ks[1], (DEC_BATCH, DEC_SEQ, D_MODEL)),
        'state_conv': nrm(ks[2], (N_A, DEC_BATCH, CONV_W - 1, CONV_DIM)),
        'cache_k': nrm(ks[3], (DEC_BATCH, PAST_LEN, N_HEADS_F, HEAD_DIM)),
        'cache_v': nrm(ks[4], (DEC_BATCH, PAST_LEN, N_HEADS_F, HEAD_DIM)),
        'cache_logf': jax.nn.log_sigmoid(3.0 + nrm(ks[5], (DEC_BATCH, PAST_LEN, N_HEADS_F))),
        'cache_mem_k': nrm(ks[6], (DEPTH, DEC_BATCH, N_MEM, N_HEADS_M, HEAD_DIM)),
        'cache_mem_v': nrm(ks[7], (DEPTH, DEC_BATCH, N_MEM, N_HEADS_M, HEAD_DIM)),
        'mem_prompt': nrm(ks[8], (BATCH, N_MEM, D_MODEL)),
        'g_norm': 1.0 + 0.02 * nrm(ks[9], (DEPTH, D_MODEL)),
        'w_in_a': nrm(ks[10], (N_A, D_MODEL, IN_A), D_MODEL ** -0.5),
        'conv_w': nrm(ks[11], (N_A, CONV_W, CONV_DIM), CONV_W ** -0.5),
        'w_in_b': nrm(ks[12], (N_B, D_MODEL, IN_B), D_MODEL ** -0.5),
        'w_out': nrm(ks[13], (DEPTH, MIX_DIM, D_MODEL), MIX_DIM ** -0.5),
        'g_mem': 1.0 + 0.02 * nrm(ks[14], (DEPTH, D_MODEL)),
        'w_mem_kv': nrm(ks[15], (DEPTH, D_MODEL, 2 * M_DIM), D_MODEL ** -0.5),
        'g_kv': 1.0 + 0.02 * nrm(ks[16], (D_MODEL,)),
        'w_kvf': nrm(ks[17], (D_MODEL, KVF_DIM), D_MODEL ** -0.5),
        'b_f': 3.0 + 0.5 * nrm(ks[18], (N_HEADS_F,)),
        'g_final': 1.0 + 0.02 * nrm(ks[19], (D_MODEL,)),
    }


def reference(x_prompt, x_sample, state_conv, cache_k, cache_v, cache_logf, cache_mem_k, cache_mem_v,
              mem_prompt, g_norm, w_in_a, conv_w, w_in_b, w_out, g_mem, w_mem_kv, g_kv, w_kvf, b_f, g_final):
    pm = [memory_kv(mem_prompt, g_mem[i], w_mem_kv[i]) for i in range(DEPTH)]
    p_mem_k = jnp.stack([a for a, _ in pm])
    p_mem_v = jnp.stack([b for _, b in pm])
    bp = x_prompt.shape[0]
    dt = x_prompt.dtype
    zero_conv = jnp.zeros((N_A, bp, CONV_W - 1, CONV_DIM), dt)
    zero_kv = jnp.zeros((bp, 0, N_HEADS_F, HEAD_DIM), dt)
    zero_logf = jnp.zeros((bp, 0, N_HEADS_F), jnp.float32)
    y_prompt, p_state_conv, p_k, p_v, p_logf = trunk(
        x_prompt, zero_conv, zero_kv, zero_kv, zero_logf, p_mem_k, p_mem_v,
        g_norm, w_in_a, conv_w, w_in_b, w_out, g_kv, w_kvf, b_f, g_final)
    y_sample, s_state_conv, s_k, s_v, s_logf = trunk(
        x_sample, state_conv, cache_k, cache_v, cache_logf, cache_mem_k, cache_mem_v,
        g_norm, w_in_a, conv_w, w_in_b, w_out, g_kv, w_kvf, b_f, g_final)
    return (y_prompt, y_sample, p_state_conv, p_k, p_v, p_logf, p_mem_k, p_mem_v,
            s_state_conv, s_k, s_v, s_logf)
```

```python
import functools
import math

import jax
import jax.numpy as jnp
from jax import lax
from jax.experimental import pallas as pl
from jax.experimental.pallas import tpu as pltpu

F32 = jnp.float32
BF16 = jnp.bfloat16

D_MODEL = 2048
CONV_W = 3
CONV_DIM = 1536
HEAD_DIM = 128
N_HEADS_F = 12
F_DIM = N_HEADS_F * HEAD_DIM
N_MEM = 256
N_HEADS_M = 4
M_DIM = N_HEADS_M * HEAD_DIM
MIX_DIM = CONV_DIM + M_DIM
EPS = 1e-6
SCALE = 1.0 / math.sqrt(HEAD_DIM)

LANES = 128
AUG_DIM = 2 * HEAD_DIM
VMEM_LIMIT = 56 * 1024 * 1024


def _params(*sem):
    return pltpu.CompilerParams(dimension_semantics=sem, vmem_limit_bytes=VMEM_LIMIT)


def _silu(z):
    return z * (1.0 / (1.0 + jnp.exp(-z)))


def _split3(c):
    hi = c.astype(BF16).astype(F32)
    r = c - hi
    mid = r.astype(BF16).astype(F32)
    lo = (r - mid).astype(BF16).astype(F32)
    return hi, mid, lo


def _aug_block(c_col, key_side):
    rows = c_col.shape[0]
    hi, mid, lo = _split3(-c_col if key_side else c_col)
    lane = lax.broadcasted_iota(jnp.int32, (rows, LANES), 1)
    term0 = 3 if key_side else 0
    one0 = 0 if key_side else 3
    terms = jnp.where(lane == term0, hi,
                      jnp.where(lane == term0 + 1, mid, jnp.where(lane == term0 + 2, lo, 0.0)))
    return jnp.where((lane >= one0) & (lane < one0 + 3), 1.0, terms).astype(BF16)


def _rmsnorm_rows(x_ref, g_ref, h_ref, rows_per_chunk=64):
    tm = x_ref.shape[0]
    g = g_ref[...]

    def body(r, carry):
        rows = pl.ds(pl.multiple_of(r * rows_per_chunk, rows_per_chunk), rows_per_chunk)
        x = x_ref[rows, :]
        ms = jnp.mean(x * x, axis=-1, keepdims=True)
        h_ref[rows, :] = (x * lax.rsqrt(ms + EPS) * g).astype(h_ref.dtype)
        return carry

    lax.fori_loop(0, tm // rows_per_chunk, body, 0)


def _norm_proj_kernel(x_ref, g_ref, w_ref, o_ref, h_ref):
    @pl.when(pl.program_id(1) == 0)
    def _():
        _rmsnorm_rows(x_ref, g_ref, h_ref)

    o_ref[...] = jnp.dot(h_ref[...], w_ref[...], preferred_element_type=F32).astype(o_ref.dtype)


def _norm_proj(x, g, w, *, tm, tn, out_dtype=F32):
    m, d = x.shape
    n = w.shape[1]
    tm = min(tm, m)
    assert m % tm == 0 and n % tn == 0
    return pl.pallas_call(
        _norm_proj_kernel,
        grid=(m // tm, n // tn),
        in_specs=[
            pl.BlockSpec((tm, d), lambda i, j: (i, 0)),
            pl.BlockSpec((1, d), lambda i, j: (0, 0)),
            pl.BlockSpec((d, tn), lambda i, j: (0, j)),
        ],
        out_specs=pl.BlockSpec((tm, tn), lambda i, j: (i, j)),
        out_shape=jax.ShapeDtypeStruct((m, n), out_dtype),
        scratch_shapes=[pltpu.VMEM((tm, d), BF16)],
        compiler_params=_params("parallel", "arbitrary"),
        name="norm_proj",
    )(x, g.reshape(1, d), w)


def _memory_attention_into(qm_ref, zm_ref, mk_ref, mv_ref, mixed_ref):
    for h in range(N_HEADS_M):
        cols = slice(h * HEAD_DIM, (h + 1) * HEAD_DIM)
        q = (qm_ref[:, cols] * SCALE).astype(BF16)
        k = mk_ref[0, :, cols].astype(BF16)
        v = mv_ref[0, :, cols].astype(BF16)
        s = lax.dot_general(q, k, (((1,), (1,)), ((), ())), preferred_element_type=F32)
        m = jnp.max(s, axis=-1, keepdims=True)
        p = jnp.exp(s - m)
        l = jnp.sum(p, axis=-1, keepdims=True)
        o = jnp.dot(p.astype(BF16), v, preferred_element_type=F32) / l
        out_cols = slice(CONV_DIM + h * HEAD_DIM, CONV_DIM + (h + 1) * HEAD_DIM)
        mixed_ref[:, out_cols] = (o * _silu(zm_ref[:, cols])).astype(mixed_ref.dtype)


def _mixer_a_kernel(bg_ref, cg_ref, u_ref, zc_ref, cgh_ref, uh_ref, st_ref, qm_ref, zm_ref,
                    mk_ref, mv_ref, cw_ref, mixed_ref, nst_ref):
    t = pl.program_id(1)
    tt = bg_ref.shape[0]
    first = t == 0
    row = lax.broadcasted_iota(jnp.int32, (tt, LANES), 0)
    for c in range(CONV_DIM // LANES):
        cols = slice(c * LANES, (c + 1) * LANES)
        ci = cg_ref[:, cols] * u_ref[:, cols]
        halo = cgh_ref[:, cols] * uh_ref[:, cols]
        st = st_ref[0, :, cols]
        prev1 = jnp.where(first, st[1:2, :], halo[7:8, :])
        prev2 = jnp.where(first, st[0:1, :], halo[6:7, :])
        s1 = jnp.where(row == 0, prev1, pltpu.roll(ci, 1, axis=0))
        s2 = jnp.where(row == 0, prev2, jnp.where(row == 1, prev1, pltpu.roll(ci, 2, axis=0)))
        w = cw_ref[:, cols]
        conv = w[0:1, :] * s2 + w[1:2, :] * s1 + w[2:3, :] * ci
        branch = bg_ref[:, cols] * conv * _silu(zc_ref[:, cols])
        mixed_ref[:, cols] = branch.astype(mixed_ref.dtype)

        @pl.when(t == pl.num_programs(1) - 1)
        def _():
            nst_ref[0, :, cols] = ci[tt - 2:tt, :]

    _memory_attention_into(qm_ref, zm_ref, mk_ref, mv_ref, mixed_ref)


def _mixer_a(proj, state, mem_k, mem_v, conv_w, *, bsz, t_len, tt):
    m = bsz * t_len
    tt = min(tt, t_len)
    nt = t_len // tt
    assert t_len % tt == 0 and tt % 8 == 0
    wide = lambda k: pl.BlockSpec((tt, CONV_DIM), lambda b, t: (b * nt + t, k))
    halo = lambda k: pl.BlockSpec(
        (8, CONV_DIM), lambda b, t: (jnp.maximum((b * nt + t) * (tt // 8) - 1, 0), k))
    narrow = lambda k: pl.BlockSpec((tt, M_DIM), lambda b, t: (b * nt + t, k))
    mem = pl.BlockSpec((1, N_MEM, M_DIM), lambda b, t: (b, 0, 0))
    q_col = 4 * CONV_DIM // M_DIM
    return pl.pallas_call(
        _mixer_a_kernel,
        grid=(bsz, nt),
        in_specs=[wide(0), wide(1), wide(2), wide(3), halo(1), halo(2),
                  pl.BlockSpec((1, CONV_W - 1, CONV_DIM), lambda b, t: (b, 0, 0)),
                  narrow(q_col), narrow(q_col + 1), mem, mem,
                  pl.BlockSpec((CONV_W, CONV_DIM), lambda b, t: (0, 0))],
        out_specs=[pl.BlockSpec((tt, MIX_DIM), lambda b, t: (b * nt + t, 0)),
                   pl.BlockSpec((1, CONV_W - 1, CONV_DIM), lambda b, t: (b, 0, 0))],
        out_shape=[jax.ShapeDtypeStruct((m, MIX_DIM), BF16),
                   jax.ShapeDtypeStruct((bsz, CONV_W - 1, CONV_DIM), F32)],
        compiler_params=_params("parallel", "arbitrary"),
        name="mixer_a",
    )(proj, proj, proj, proj, proj, proj, state, proj, proj, mem_k, mem_v, conv_w)


def _mixer_b_kernel(o_ref, zf_ref, qm_ref, zm_ref, mk_ref, mv_ref, mixed_ref):
    for c in range(F_DIM // LANES):
        cols = slice(c * LANES, (c + 1) * LANES)
        mixed_ref[:, cols] = (o_ref[:, cols] * _silu(zf_ref[:, cols])).astype(mixed_ref.dtype)
    _memory_attention_into(qm_ref, zm_ref, mk_ref, mv_ref, mixed_ref)


def _mixer_b(o, rest, mem_k, mem_v, *, bsz, t_len, tt):
    m = bsz * t_len
    tt = min(tt, t_len)
    nt = t_len // tt
    assert t_len % tt == 0
    narrow = lambda k: pl.BlockSpec((tt, M_DIM), lambda b, t: (b * nt + t, k))
    mem = pl.BlockSpec((1, N_MEM, M_DIM), lambda b, t: (b, 0, 0))
    q_col = F_DIM // M_DIM
    return pl.pallas_call(
        _mixer_b_kernel,
        grid=(bsz, nt),
        in_specs=[pl.BlockSpec((tt, F_DIM), lambda b, t: (b * nt + t, 0)),
                  pl.BlockSpec((tt, F_DIM), lambda b, t: (b * nt + t, 0)),
                  narrow(q_col), narrow(q_col + 1), mem, mem],
        out_specs=pl.BlockSpec((tt, MIX_DIM), lambda b, t: (b * nt + t, 0)),
        out_shape=jax.ShapeDtypeStruct((m, MIX_DIM), BF16),
        compiler_params=_params("parallel", "parallel"),
        name="mixer_b",
    )(o, rest, rest, rest, mem_k, mem_v)


def _out_proj_kernel(a_ref, w_ref, x_ref, g_ref, o_ref, *, final_norm, tn, rows_per_chunk=64):
    n = o_ref.shape[1]
    for c in range(n // tn):
        cols = slice(c * tn, (c + 1) * tn)
        o_ref[:, cols] = x_ref[:, cols] + jnp.dot(a_ref[...], w_ref[:, cols],
                                                  preferred_element_type=F32)
    if final_norm:
        g = g_ref[...]

        def body(r, carry):
            rows = pl.ds(pl.multiple_of(r * rows_per_chunk, rows_per_chunk), rows_per_chunk)
            x = o_ref[rows, :]
            ms = jnp.mean(x * x, axis=-1, keepdims=True)
            o_ref[rows, :] = x * lax.rsqrt(ms + EPS) * g
            return carry

        lax.fori_loop(0, o_ref.shape[0] // rows_per_chunk, body, 0)


def _out_proj(a, w, x, g, *, final_norm, tm):
    m, k = a.shape
    n = w.shape[1]
    tm = min(tm, m)
    assert m % tm == 0
    return pl.pallas_call(
        functools.partial(_out_proj_kernel, final_norm=final_norm, tn=512),
        grid=(m // tm,),
        in_specs=[pl.BlockSpec((tm, k), lambda i: (i, 0)),
                  pl.BlockSpec((k, n), lambda i: (0, 0)),
                  pl.BlockSpec((tm, n), lambda i: (i, 0)),
                  pl.BlockSpec((1, n), lambda i: (0, 0))],
        out_specs=pl.BlockSpec((tm, n), lambda i: (i, 0)),
        out_shape=jax.ShapeDtypeStruct((m, n), F32),
        compiler_params=_params("parallel"),
        name="out_proj_norm" if final_norm else "out_proj",
    )(a, w, x, g.reshape(1, n))


def _kvf_kernel(x_ref, g_ref, wkv_ref, wf_ref, bf_ref, c0_ref,
                k_ref, v_ref, logf_ref, c_ref, kx_ref, vb_ref, h_ref, carry_ref,
                *, seg, tiles_per_seq, tn):
    i = pl.program_id(0)
    tm = x_ref.shape[0]
    _rmsnorm_rows(x_ref, g_ref, h_ref)
    h = h_ref[...]

    logit = jnp.dot(h, wf_ref[...], preferred_element_type=F32) + bf_ref[...]
    logf = jnp.minimum(logit, 0.0) - jnp.log1p(jnp.exp(-jnp.abs(logit)))
    logf_ref[...] = logf[:, :N_HEADS_F]
    row = lax.broadcasted_iota(jnp.int32, (tm, tm), 0)
    col = lax.broadcasted_iota(jnp.int32, (tm, tm), 1)
    tri = col <= row
    if seg < tm:
        shift = seg.bit_length() - 1
        tri = tri & ((col >> shift) == (row >> shift))
    tri = jnp.where(tri, 1.0, 0.0).astype(BF16)
    hi, mid, lo = _split3(logf)
    c = (jnp.dot(tri, hi.astype(BF16), preferred_element_type=F32)
         + jnp.dot(tri, mid.astype(BF16), preferred_element_type=F32)
         + jnp.dot(tri, lo.astype(BF16), preferred_element_type=F32))
    if tiles_per_seq > 1:
        @pl.when(i % tiles_per_seq == 0)
        def _():
            carry_ref[...] = c0_ref[0:1, :]

        c = c + carry_ref[...]
        carry_ref[...] = c[tm - 1:tm, :]
    else:
        c = c + c0_ref[...]
    c_ref[...] = c

    heads_per_chunk = tn // HEAD_DIM
    for n in range(2 * F_DIM // tn):
        y = jnp.dot(h, wkv_ref[:, n * tn:(n + 1) * tn], preferred_element_type=F32)
        if n < F_DIM // tn:
            k_ref[:, n * tn:(n + 1) * tn] = y
            for hh in range(heads_per_chunk):
                head = n * heads_per_chunk + hh
                base = head * AUG_DIM
                kx_ref[:, base:base + HEAD_DIM] = y[:, hh * HEAD_DIM:(hh + 1) * HEAD_DIM].astype(BF16)
                kx_ref[:, base + HEAD_DIM:base + AUG_DIM] = _aug_block(c[:, head:head + 1], True)
        else:
            off = n * tn - F_DIM
            v_ref[:, off:off + tn] = y
            vb_ref[:, off:off + tn] = y.astype(BF16)


def _kvf(x, g, w_kv, w_f, b_f, c0, *, t_len, tm):
    m, d = x.shape
    tm = min(tm, m)
    seg = min(t_len, tm)
    assert m % tm == 0 and seg & (seg - 1) == 0 and (t_len % tm == 0 or tm % t_len == 0)
    tiles_per_seq = max(t_len // tm, 1)
    row = lambda w: pl.BlockSpec((tm, w), lambda i: (i, 0))
    full = lambda a, b: pl.BlockSpec((a, b), lambda i: (0, 0))
    return pl.pallas_call(
        functools.partial(_kvf_kernel, seg=seg, tiles_per_seq=tiles_per_seq, tn=512),
        grid=(m // tm,),
        in_specs=[row(d), full(1, d), full(d, 2 * F_DIM), full(d, LANES), full(1, LANES),
                  row(LANES)],
        out_specs=[row(F_DIM), row(F_DIM), row(N_HEADS_F), row(LANES),
                   row(N_HEADS_F * AUG_DIM), row(F_DIM)],
        out_shape=[jax.ShapeDtypeStruct((m, F_DIM), F32),
                   jax.ShapeDtypeStruct((m, F_DIM), F32),
                   jax.ShapeDtypeStruct((m, N_HEADS_F), F32),
                   jax.ShapeDtypeStruct((m, LANES), F32),
                   jax.ShapeDtypeStruct((m, N_HEADS_F * AUG_DIM), BF16),
                   jax.ShapeDtypeStruct((m, F_DIM), BF16)],
        scratch_shapes=[pltpu.VMEM((tm, d), BF16), pltpu.VMEM((1, LANES), F32)],
        compiler_params=_params("arbitrary"),
        name="kvf",
    )(x, g.reshape(1, d), w_kv, w_f, b_f, c0)


def _qproj_kernel(x_ref, g_ref, wq_ref, c_ref, qx_ref, h_ref, *, tn):
    _rmsnorm_rows(x_ref, g_ref, h_ref)
    h = h_ref[...]
    c = c_ref[...]
    heads_per_chunk = tn // HEAD_DIM
    for n in range(F_DIM // tn):
        y = jnp.dot(h, wq_ref[:, n * tn:(n + 1) * tn], preferred_element_type=F32) * SCALE
        for hh in range(heads_per_chunk):
            head = n * heads_per_chunk + hh
            base = head * AUG_DIM
            qx_ref[:, base:base + HEAD_DIM] = y[:, hh * HEAD_DIM:(hh + 1) * HEAD_DIM].astype(BF16)
            qx_ref[:, base + HEAD_DIM:base + AUG_DIM] = _aug_block(c[:, head:head + 1], False)


def _qproj(x, g, w_q, c, *, tm):
    m, d = x.shape
    tm = min(tm, m)
    assert m % tm == 0
    return pl.pallas_call(
        functools.partial(_qproj_kernel, tn=512),
        grid=(m // tm,),
        in_specs=[pl.BlockSpec((tm, d), lambda i: (i, 0)),
                  pl.BlockSpec((1, d), lambda i: (0, 0)),
                  pl.BlockSpec((d, F_DIM), lambda i: (0, 0)),
                  pl.BlockSpec((tm, LANES), lambda i: (i, 0))],
        out_specs=pl.BlockSpec((tm, N_HEADS_F * AUG_DIM), lambda i: (i, 0)),
        out_shape=jax.ShapeDtypeStruct((m, N_HEADS_F * AUG_DIM), BF16),
        scratch_shapes=[pltpu.VMEM((tm, d), BF16)],
        compiler_params=_params("parallel"),
        name="qproj",
    )(x, g.reshape(1, d), w_q, c)


def _softmax_update(s, v, m_ref, l_ref, acc_ref):
    m_old = m_ref[...]
    m_new = jnp.maximum(m_old, jnp.max(s, axis=-1, keepdims=True))
    alpha = jnp.exp(m_old - m_new)
    p = jnp.exp(s - m_new)
    l_ref[...] = alpha * l_ref[...] + jnp.sum(p, axis=-1, keepdims=True)
    acc_ref[...] = alpha * acc_ref[...] + jnp.dot(p.astype(BF16), v, preferred_element_type=F32)
    m_ref[...] = m_new


def _fox_prompt_kernel(qx_ref, kx_ref, v_ref, o_ref, m_ref, l_ref, acc_ref):
    qi = pl.program_id(2)
    ki = pl.program_id(3)
    tq, tk = qx_ref.shape[0], kx_ref.shape[0]

    @pl.when(ki == 0)
    def _():
        m_ref[...] = jnp.full_like(m_ref, -jnp.inf)
        l_ref[...] = jnp.zeros_like(l_ref)
        acc_ref[...] = jnp.zeros_like(acc_ref)

    def scores():
        return lax.dot_general(qx_ref[...], kx_ref[...], (((1,), (1,)), ((), ())),
                               preferred_element_type=F32)

    @pl.when(ki < qi)
    def _():
        _softmax_update(scores(), v_ref[...], m_ref, l_ref, acc_ref)

    @pl.when(ki == qi)
    def _():
        row = lax.broadcasted_iota(jnp.int32, (tq, tk), 0)
        col = lax.broadcasted_iota(jnp.int32, (tq, tk), 1)
        s = jnp.where(col <= row, scores(), -jnp.inf)
        _softmax_update(s, v_ref[...], m_ref, l_ref, acc_ref)
        o_ref[...] = (acc_ref[...] / l_ref[...]).astype(o_ref.dtype)


def _fox_prompt(qx, kx, vb, *, bsz, t_len, tq):
    m = bsz * t_len
    nq = t_len // tq
    assert t_len % tq == 0
    kv_row = lambda b, h, qi, ki: b * nq + jnp.minimum(ki, qi)
    return pl.pallas_call(
        _fox_prompt_kernel,
        grid=(bsz, N_HEADS_F, nq, nq),
        in_specs=[pl.BlockSpec((tq, AUG_DIM), lambda b, h, qi, ki: (b * nq + qi, h)),
                  pl.BlockSpec((tq, AUG_DIM), lambda b, h, qi, ki: (kv_row(b, h, qi, ki), h)),
                  pl.BlockSpec((tq, HEAD_DIM), lambda b, h, qi, ki: (kv_row(b, h, qi, ki), h))],
        out_specs=pl.BlockSpec((tq, HEAD_DIM), lambda b, h, qi, ki: (b * nq + qi, h)),
        out_shape=jax.ShapeDtypeStruct((m, F_DIM), F32),
        scratch_shapes=[pltpu.VMEM((tq, 1), F32), pltpu.VMEM((tq, 1), F32),
                        pltpu.VMEM((tq, HEAD_DIM), F32)],
        compiler_params=_params("parallel", "parallel", "parallel", "arbitrary"),
        name="fox_prompt",
    )(qx, kx, vb)


def _fox_cached_kernel(qx_ref, kxn_ref, vn_ref, cn_ref, ck_ref, cv_ref, cp_ref, o_ref,
                       m_ref, l_ref, acc_ref):
    ki = pl.program_id(1)
    t_new = qx_ref.shape[0]

    @pl.when(ki == 0)
    def _():
        m_ref[...] = jnp.full_like(m_ref, -jnp.inf)
        l_ref[...] = jnp.zeros_like(l_ref)
        acc_ref[...] = jnp.zeros_like(acc_ref)

    def update(h, s, v):
        cols = slice(h * HEAD_DIM, (h + 1) * HEAD_DIM)
        m_old = m_ref[:, h:h + 1]
        m_new = jnp.maximum(m_old, jnp.max(s, axis=-1, keepdims=True))
        alpha = jnp.exp(m_old - m_new)
        p = jnp.exp(s - m_new)
        l_ref[:, h:h + 1] = alpha * l_ref[:, h:h + 1] + jnp.sum(p, axis=-1, keepdims=True)
        acc_ref[:, cols] = alpha * acc_ref[:, cols] + jnp.dot(p.astype(BF16), v,
                                                              preferred_element_type=F32)
        m_ref[:, h:h + 1] = m_new

    for h in range(N_HEADS_F):
        cols = slice(h * HEAD_DIM, (h + 1) * HEAD_DIM)
        q = qx_ref[:, h * AUG_DIM:h * AUG_DIM + HEAD_DIM]
        k = ck_ref[0, :, cols].astype(BF16)
        s = lax.dot_general(q, k, (((1,), (1,)), ((), ())), preferred_element_type=F32)
        s = s + (cn_ref[:, h:h + 1] - cp_ref[0, h:h + 1, :])
        update(h, s, cv_ref[0, :, cols].astype(BF16))

    @pl.when(ki == pl.num_programs(1) - 1)
    def _():
        row = lax.broadcasted_iota(jnp.int32, (t_new, t_new), 0)
        col = lax.broadcasted_iota(jnp.int32, (t_new, t_new), 1)
        for h in range(N_HEADS_F):
            cols = slice(h * HEAD_DIM, (h + 1) * HEAD_DIM)
            xcols = slice(h * AUG_DIM, (h + 1) * AUG_DIM)
            s = lax.dot_general(qx_ref[:, xcols], kxn_ref[:, xcols], (((1,), (1,)), ((), ())),
                                preferred_element_type=F32)
            s = jnp.where(col <= row, s, -jnp.inf)
            update(h, s, vn_ref[:, cols])
            o_ref[:, cols] = (acc_ref[:, cols] / l_ref[:, h:h + 1]).astype(o_ref.dtype)


def _fox_cached(qx, kx_new, vb_new, c_new, cache_k, cache_v, c_past, *, bsz, t_len, tk):
    m = bsz * t_len
    past = cache_k.shape[1]
    assert past % tk == 0 and past > 0
    seq = lambda w: pl.BlockSpec((t_len, w), lambda b, ki: (b, 0))
    cache = pl.BlockSpec((1, tk, F_DIM), lambda b, ki: (b, ki, 0))
    return pl.pallas_call(
        _fox_cached_kernel,
        grid=(bsz, past // tk),
        in_specs=[seq(N_HEADS_F * AUG_DIM), seq(N_HEADS_F * AUG_DIM), seq(F_DIM), seq(LANES),
                  cache, cache,
                  pl.BlockSpec((1, c_past.shape[1], tk), lambda b, ki: (b, 0, ki))],
        out_specs=seq(F_DIM),
        out_shape=jax.ShapeDtypeStruct((m, F_DIM), F32),
        scratch_shapes=[pltpu.VMEM((t_len, LANES), F32), pltpu.VMEM((t_len, LANES), F32),
                        pltpu.VMEM((t_len, F_DIM), F32)],
        compiler_params=_params("parallel", "arbitrary"),
        name="fox_cached",
    )(qx, kx_new, vb_new, c_new, cache_k, cache_v, c_past)


def _cumsum_lanes_kernel(x_ref, o_ref):
    rows, n = x_ref.shape
    r = lax.broadcasted_iota(jnp.int32, (LANES, LANES), 0)
    c = lax.broadcasted_iota(jnp.int32, (LANES, LANES), 1)
    upper = jnp.where(r <= c, 1.0, 0.0).astype(BF16)
    carry = jnp.zeros((rows, 1), F32)
    for j in range(n // LANES):
        cols = slice(j * LANES, (j + 1) * LANES)
        hi, mid, lo = _split3(x_ref[:, cols])
        local = (jnp.dot(hi.astype(BF16), upper, preferred_element_type=F32)
                 + jnp.dot(mid.astype(BF16), upper, preferred_element_type=F32)
                 + jnp.dot(lo.astype(BF16), upper, preferred_element_type=F32))
        o_ref[:, cols] = local + carry
        carry = carry + local[:, LANES - 1:LANES]


def _cumsum_lanes(x):
    rows, n = x.shape
    return pl.pallas_call(
        _cumsum_lanes_kernel,
        grid=(1,),
        in_specs=[pl.BlockSpec((rows, n), lambda i: (0, 0))],
        out_specs=pl.BlockSpec((rows, n), lambda i: (0, 0)),
        out_shape=jax.ShapeDtypeStruct((rows, n), F32),
        compiler_params=_params("arbitrary"),
        name="cumsum_lanes",
    )(x)


def _trunk(x, conv_state, mem_k, mem_v, past, w):
    bsz, t_len, d = x.shape
    m = bsz * t_len
    x0 = x.reshape(m, d)
    head_pad = 16

    proj = _norm_proj(x0, w["g_norm"][0], w["w_in_a"], tm=1024, tn=512)
    mixed, new_state = _mixer_a(proj, conv_state, mem_k[0], mem_v[0], w["conv_w"],
                                bsz=bsz, t_len=t_len, tt=256)
    x1 = _out_proj(mixed, w["w_out"][0], x0, w["g_final"], final_norm=False, tm=512)

    if past is None:
        c0 = jnp.zeros((m, LANES), F32)
    else:
        cache_k, cache_v, cache_logf = past
        past_len = cache_k.shape[1]
        logf_t = jnp.pad(jnp.swapaxes(cache_logf, 1, 2), ((0, 0), (0, head_pad - N_HEADS_F), (0, 0)))
        c_past = _cumsum_lanes(logf_t.reshape(bsz * head_pad, past_len)).reshape(bsz, head_pad, past_len)
        c_end = jnp.pad(c_past[:, :, past_len - 1], ((0, 0), (0, LANES - head_pad)))
        c0 = jnp.repeat(c_end, t_len, axis=0)
    k_new, v_new, logf, c, kx, vb = _kvf(x1, w["g_kv"], w["w_kv"], w["w_f"], w["b_f"], c0,
                                         t_len=t_len, tm=256)
    qx = _qproj(x1, w["g_norm"][1], w["w_q"], c, tm=512)
    rest = _norm_proj(x1, w["g_norm"][1], w["w_in_b_rest"], tm=1024, tn=512)

    if past is None:
        o = _fox_prompt(qx, kx, vb, bsz=bsz, t_len=t_len, tq=512)
    else:
        o = _fox_cached(qx, kx, vb, c, cache_k.reshape(bsz, past_len, F_DIM),
                        cache_v.reshape(bsz, past_len, F_DIM), c_past,
                        bsz=bsz, t_len=t_len, tk=512)
    mixed = _mixer_b(o, rest, mem_k[1], mem_v[1], bsz=bsz, t_len=t_len, tt=256)
    y = _out_proj(mixed, w["w_out"][1], x1, w["g_final"], final_norm=True, tm=512)
    return (y.reshape(bsz, t_len, d), new_state[None],
            k_new.reshape(bsz, t_len, N_HEADS_F, HEAD_DIM),
            v_new.reshape(bsz, t_len, N_HEADS_F, HEAD_DIM),
            logf.reshape(bsz, t_len, N_HEADS_F))


def kernel(x_prompt, x_sample, state_conv, cache_k, cache_v, cache_logf, cache_mem_k, cache_mem_v,
           mem_prompt, g_norm, w_in_a, conv_w, w_in_b, w_out, g_mem, w_mem_kv, g_kv, w_kvf, b_f,
           g_final):
    depth = g_norm.shape[0]
    bp = x_prompt.shape[0]
    bs = x_sample.shape[0]
    w = {
        "g_norm": g_norm, "g_kv": g_kv, "g_final": g_final, "conv_w": conv_w[0],
        "w_in_a": w_in_a[0].astype(BF16),
        "w_q": w_in_b[0][:, :F_DIM].astype(BF16),
        "w_in_b_rest": w_in_b[0][:, F_DIM:].astype(BF16),
        "w_out": w_out.astype(BF16),
        "w_kv": w_kvf[:, :2 * F_DIM].astype(BF16),
        "w_f": jnp.pad(w_kvf[:, 2 * F_DIM:], ((0, 0), (0, LANES - N_HEADS_F))).astype(BF16),
        "b_f": jnp.pad(b_f, (0, LANES - N_HEADS_F)).reshape(1, LANES),
    }

    mem_rows = mem_prompt.reshape(bp * N_MEM, D_MODEL)
    mem_kv = [_norm_proj(mem_rows, g_mem[i], w_mem_kv[i].astype(BF16), tm=512, tn=512)
              for i in range(depth)]
    p_mem_k = jnp.stack([a[:, :M_DIM] for a in mem_kv]).reshape(depth, bp, N_MEM, N_HEADS_M, HEAD_DIM)
    p_mem_v = jnp.stack([a[:, M_DIM:] for a in mem_kv]).reshape(depth, bp, N_MEM, N_HEADS_M, HEAD_DIM)

    zero_conv = jnp.zeros((bp, CONV_W - 1, CONV_DIM), F32)
    y_p, p_state, p_k, p_v, p_logf = _trunk(
        x_prompt, zero_conv, p_mem_k.reshape(depth, bp, N_MEM, M_DIM),
        p_mem_v.reshape(depth, bp, N_MEM, M_DIM), None, w)
    y_s, s_state, s_k, s_v, s_logf = _trunk(
        x_sample, state_conv[0], cache_mem_k.reshape(depth, bs, N_MEM, M_DIM),
        cache_mem_v.reshape(depth, bs, N_MEM, M_DIM), (cache_k, cache_v, cache_logf), w)
    return (y_p, y_s, p_state, p_k, p_v, p_logf, p_mem_k, p_mem_v, s_state, s_k, s_v, s_logf)
```

```python
import functools
import math

import jax
import jax.numpy as jnp
from jax import lax
from jax.experimental import pallas as pl
from jax.experimental.pallas import tpu as pltpu

F32 = jnp.float32
BF16 = jnp.bfloat16

D_MODEL = 2048
CONV_W = 3
CONV_DIM = 1536
HEAD_DIM = 128
N_HEADS_F = 12
F_DIM = N_HEADS_F * HEAD_DIM
N_MEM = 256
N_HEADS_M = 4
M_DIM = N_HEADS_M * HEAD_DIM
MIX_DIM = CONV_DIM + M_DIM
EPS = 1e-6
SCALE = 1.0 / math.sqrt(HEAD_DIM)
LOG2E = math.log2(math.e)

LANES = 128
AUG_DIM = 2 * HEAD_DIM
VMEM_LIMIT = 56 * 1024 * 1024


def _params(*sem):
    return pltpu.CompilerParams(dimension_semantics=sem, vmem_limit_bytes=VMEM_LIMIT)


def _silu(z):
    return z * (1.0 / (1.0 + jnp.exp(-z)))


def _split3(c):
    hi = c.astype(BF16).astype(F32)
    r = c - hi
    mid = r.astype(BF16).astype(F32)
    lo = (r - mid).astype(BF16).astype(F32)
    return hi, mid, lo


def _aug_block(c_col, key_side):
    rows = c_col.shape[0]
    hi, mid, lo = _split3(-c_col if key_side else c_col)
    lane = lax.broadcasted_iota(jnp.int32, (rows, LANES), 1)
    term0 = 3 if key_side else 0
    one0 = 0 if key_side else 3
    terms = jnp.where(lane == term0, hi,
                      jnp.where(lane == term0 + 1, mid, jnp.where(lane == term0 + 2, lo, 0.0)))
    return jnp.where((lane >= one0) & (lane < one0 + 3), 1.0, terms).astype(BF16)


def _rmsnorm_rows(x_ref, g_ref, h_ref, rows_per_chunk=64):
    tm = x_ref.shape[0]
    g = g_ref[...]

    def body(r, carry):
        rows = pl.ds(pl.multiple_of(r * rows_per_chunk, rows_per_chunk), rows_per_chunk)
        x = x_ref[rows, :]
        ms = jnp.mean(x * x, axis=-1, keepdims=True)
        h_ref[rows, :] = (x * lax.rsqrt(ms + EPS) * g).astype(h_ref.dtype)
        return carry

    lax.fori_loop(0, tm // rows_per_chunk, body, 0)


def _norm_proj_kernel(x_ref, g_ref, w_ref, o_ref, h_ref):
    @pl.when(pl.program_id(1) == 0)
    def _():
        _rmsnorm_rows(x_ref, g_ref, h_ref)

    o_ref[...] = jnp.dot(h_ref[...], w_ref[...], preferred_element_type=F32).astype(o_ref.dtype)


def _norm_proj(x, g, w, *, tm, tn, out_dtype=F32):
    m, d = x.shape
    n = w.shape[1]
    tm = min(tm, m)
    assert m % tm == 0 and n % tn == 0
    return pl.pallas_call(
        _norm_proj_kernel,
        grid=(m // tm, n // tn),
        in_specs=[
            pl.BlockSpec((tm, d), lambda i, j: (i, 0)),
            pl.BlockSpec((1, d), lambda i, j: (0, 0)),
            pl.BlockSpec((d, tn), lambda i, j: (0, j)),
        ],
        out_specs=pl.BlockSpec((tm, tn), lambda i, j: (i, j)),
        out_shape=jax.ShapeDtypeStruct((m, n), out_dtype),
        scratch_shapes=[pltpu.VMEM((tm, d), BF16)],
        compiler_params=_params("parallel", "arbitrary"),
        name="norm_proj",
    )(x, g.reshape(1, d), w)


def _memory_attention_into(qm_ref, zm_ref, mk_ref, mv_ref, mixed_ref):
    for h in range(N_HEADS_M):
        cols = slice(h * HEAD_DIM, (h + 1) * HEAD_DIM)
        q = (qm_ref[:, cols] * SCALE).astype(BF16)
        k = mk_ref[0, :, cols].astype(BF16)
        v = mv_ref[0, :, cols].astype(BF16)
        s = lax.dot_general(q, k, (((1,), (1,)), ((), ())), preferred_element_type=F32)
        m = jnp.max(s, axis=-1, keepdims=True)
        p = jnp.exp(s - m)
        l = jnp.sum(p, axis=-1, keepdims=True)
        o = jnp.dot(p.astype(BF16), v, preferred_element_type=F32) / l
        out_cols = slice(CONV_DIM + h * HEAD_DIM, CONV_DIM + (h + 1) * HEAD_DIM)
        mixed_ref[:, out_cols] = (o * _silu(zm_ref[:, cols])).astype(mixed_ref.dtype)


def _mixer_a_kernel(bg_ref, cg_ref, u_ref, zc_ref, cgh_ref, uh_ref, st_ref, qm_ref, zm_ref,
                    mk_ref, mv_ref, cw_ref, mixed_ref, nst_ref):
    t = pl.program_id(1)
    tt = bg_ref.shape[0]
    first = t == 0
    row = lax.broadcasted_iota(jnp.int32, (tt, LANES), 0)
    for c in range(CONV_DIM // LANES):
        cols = slice(c * LANES, (c + 1) * LANES)
        ci = cg_ref[:, cols] * u_ref[:, cols]
        halo = cgh_ref[:, cols] * uh_ref[:, cols]
        st = st_ref[0, :, cols]
        prev1 = jnp.where(first, st[1:2, :], halo[7:8, :])
        prev2 = jnp.where(first, st[0:1, :], halo[6:7, :])
        s1 = jnp.where(row == 0, prev1, pltpu.roll(ci, 1, axis=0))
        s2 = jnp.where(row == 0, prev2, jnp.where(row == 1, prev1, pltpu.roll(ci, 2, axis=0)))
        w = cw_ref[:, cols]
        conv = w[0:1, :] * s2 + w[1:2, :] * s1 + w[2:3, :] * ci
        branch = bg_ref[:, cols] * conv * _silu(zc_ref[:, cols])
        mixed_ref[:, cols] = branch.astype(mixed_ref.dtype)

        @pl.when(t == pl.num_programs(1) - 1)
        def _():
            nst_ref[0, :, cols] = ci[tt - 2:tt, :]

    _memory_attention_into(qm_ref, zm_ref, mk_ref, mv_ref, mixed_ref)


def _mixer_a(proj, state, mem_k, mem_v, conv_w, *, bsz, t_len, tt):
    m = bsz * t_len
    tt = min(tt, t_len)
    nt = t_len // tt
    assert t_len % tt == 0 and tt % 8 == 0
    wide = lambda k: pl.BlockSpec((tt, CONV_DIM), lambda b, t: (b * nt + t, k))
    halo = lambda k: pl.BlockSpec(
        (8, CONV_DIM), lambda b, t: (jnp.maximum((b * nt + t) * (tt // 8) - 1, 0), k))
    narrow = lambda k: pl.BlockSpec((tt, M_DIM), lambda b, t: (b * nt + t, k))
    mem = pl.BlockSpec((1, N_MEM, M_DIM), lambda b, t: (b, 0, 0))
    q_col = 4 * CONV_DIM // M_DIM
    return pl.pallas_call(
        _mixer_a_kernel,
        grid=(bsz, nt),
        in_specs=[wide(0), wide(1), wide(2), wide(3), halo(1), halo(2),
                  pl.BlockSpec((1, CONV_W - 1, CONV_DIM), lambda b, t: (b, 0, 0)),
                  narrow(q_col), narrow(q_col + 1), mem, mem,
                  pl.BlockSpec((CONV_W, CONV_DIM), lambda b, t: (0, 0))],
        out_specs=[pl.BlockSpec((tt, MIX_DIM), lambda b, t: (b * nt + t, 0)),
                   pl.BlockSpec((1, CONV_W - 1, CONV_DIM), lambda b, t: (b, 0, 0))],
        out_shape=[jax.ShapeDtypeStruct((m, MIX_DIM), BF16),
                   jax.ShapeDtypeStruct((bsz, CONV_W - 1, CONV_DIM), F32)],
        compiler_params=_params("parallel", "arbitrary"),
        name="mixer_a",
    )(proj, proj, proj, proj, proj, proj, state, proj, proj, mem_k, mem_v, conv_w)


def _mixer_b_kernel(o_ref, zf_ref, qm_ref, zm_ref, mk_ref, mv_ref, mixed_ref):
    for c in range(F_DIM // LANES):
        cols = slice(c * LANES, (c + 1) * LANES)
        mixed_ref[:, cols] = (o_ref[:, cols] * _silu(zf_ref[:, cols])).astype(mixed_ref.dtype)
    _memory_attention_into(qm_ref, zm_ref, mk_ref, mv_ref, mixed_ref)


def _mixer_b(o, rest, mem_k, mem_v, *, bsz, t_len, tt):
    m = bsz * t_len
    tt = min(tt, t_len)
    nt = t_len // tt
    assert t_len % tt == 0
    narrow = lambda k: pl.BlockSpec((tt, M_DIM), lambda b, t: (b * nt + t, k))
    mem = pl.BlockSpec((1, N_MEM, M_DIM), lambda b, t: (b, 0, 0))
    q_col = F_DIM // M_DIM
    return pl.pallas_call(
        _mixer_b_kernel,
        grid=(bsz, nt),
        in_specs=[pl.BlockSpec((tt, F_DIM), lambda b, t: (b * nt + t, 0)),
                  pl.BlockSpec((tt, F_DIM), lambda b, t: (b * nt + t, 0)),
                  narrow(q_col), narrow(q_col + 1), mem, mem],
        out_specs=pl.BlockSpec((tt, MIX_DIM), lambda b, t: (b * nt + t, 0)),
        out_shape=jax.ShapeDtypeStruct((m, MIX_DIM), BF16),
        compiler_params=_params("parallel", "parallel"),
        name="mixer_b",
    )(o, rest, rest, rest, mem_k, mem_v)


def _out_proj_kernel(a_ref, w_ref, x_ref, g_ref, o_ref, *, final_norm, tn, rows_per_chunk=64):
    n = o_ref.shape[1]
    for c in range(n // tn):
        cols = slice(c * tn, (c + 1) * tn)
        o_ref[:, cols] = x_ref[:, cols] + jnp.dot(a_ref[...], w_ref[:, cols],
                                                  preferred_element_type=F32)
    if final_norm:
        g = g_ref[...]

        def body(r, carry):
            rows = pl.ds(pl.multiple_of(r * rows_per_chunk, rows_per_chunk), rows_per_chunk)
            x = o_ref[rows, :]
            ms = jnp.mean(x * x, axis=-1, keepdims=True)
            o_ref[rows, :] = x * lax.rsqrt(ms + EPS) * g
            return carry

        lax.fori_loop(0, o_ref.shape[0] // rows_per_chunk, body, 0)


def _out_proj(a, w, x, g, *, final_norm, tm):
    m, k = a.shape
    n = w.shape[1]
    tm = min(tm, m)
    assert m % tm == 0
    return pl.pallas_call(
        functools.partial(_out_proj_kernel, final_norm=final_norm, tn=512),
        grid=(m // tm,),
        in_specs=[pl.BlockSpec((tm, k), lambda i: (i, 0)),
                  pl.BlockSpec((k, n), lambda i: (0, 0)),
                  pl.BlockSpec((tm, n), lambda i: (i, 0)),
                  pl.BlockSpec((1, n), lambda i: (0, 0))],
        out_specs=pl.BlockSpec((tm, n), lambda i: (i, 0)),
        out_shape=jax.ShapeDtypeStruct((m, n), F32),
        compiler_params=_params("parallel"),
        name="out_proj_norm" if final_norm else "out_proj",
    )(a, w, x, g.reshape(1, n))


def _kvf_kernel(x_ref, g_ref, wkv_ref, wf_ref, bf_ref, c0_ref, *refs,
                seg, tiles_per_seq, tn, emit_vt):
    if emit_vt:
        k_ref, v_ref, logf_ref, c_ref, kx_ref, vt_ref, h_ref, carry_ref = refs
    else:
        k_ref, v_ref, logf_ref, c_ref, kx_ref, h_ref, carry_ref = refs
    n_seq, seq_rows = k_ref.shape[0], k_ref.shape[2]
    i = pl.program_id(0)
    tm = x_ref.shape[0]
    _rmsnorm_rows(x_ref, g_ref, h_ref)
    h = h_ref[...]

    logit = jnp.dot(h, wf_ref[...], preferred_element_type=F32) + bf_ref[...]
    logf = jnp.minimum(logit, 0.0) - jnp.log1p(jnp.exp(-jnp.abs(logit)))
    logf_ref[...] = logf[:, :N_HEADS_F]
    row = lax.broadcasted_iota(jnp.int32, (tm, tm), 0)
    col = lax.broadcasted_iota(jnp.int32, (tm, tm), 1)
    tri = col <= row
    if seg < tm:
        shift = seg.bit_length() - 1
        tri = tri & ((col >> shift) == (row >> shift))
    tri = jnp.where(tri, 1.0, 0.0).astype(BF16)
    hi, mid, lo = _split3(logf)
    c = (jnp.dot(tri, hi.astype(BF16), preferred_element_type=F32)
         + jnp.dot(tri, mid.astype(BF16), preferred_element_type=F32)
         + jnp.dot(tri, lo.astype(BF16), preferred_element_type=F32))
    if tiles_per_seq > 1:
        @pl.when(i % tiles_per_seq == 0)
        def _():
            carry_ref[...] = c0_ref[0:1, :]

        c = c + carry_ref[...]
        carry_ref[...] = c[tm - 1:tm, :]
    else:
        c = c + c0_ref[...]
    c_ref[...] = c

    heads_per_chunk = tn // HEAD_DIM
    c2 = c * LOG2E
    for n in range(2 * F_DIM // tn):
        y = jnp.dot(h, wkv_ref[:, n * tn:(n + 1) * tn], preferred_element_type=F32)
        is_key = n < F_DIM // tn
        for hh in range(heads_per_chunk):
            head = (n % (F_DIM // tn)) * heads_per_chunk + hh
            yh = y[:, hh * HEAD_DIM:(hh + 1) * HEAD_DIM]
            if is_key:
                yb = yh.astype(BF16)
                aug = _aug_block(c2[:, head:head + 1], True)
            elif emit_vt:
                vt_ref[0, head, 0] = yh.T.astype(BF16)
            for s in range(n_seq):
                rows = slice(s * seq_rows, (s + 1) * seq_rows)
                if is_key:
                    k_ref[s, head] = yh[rows]
                    kx_ref[s, head, :, 0:HEAD_DIM] = yb[rows]
                    kx_ref[s, head, :, HEAD_DIM:AUG_DIM] = aug[rows]
                else:
                    v_ref[s, head] = yh[rows]


def _kvf(x, g, w_kv, w_f, b_f, c0, *, bsz, t_len, tm, emit_vt):
    m, d = x.shape
    tm = min(tm, m)
    seg = min(t_len, tm)
    assert m % tm == 0 and seg & (seg - 1) == 0 and (t_len % tm == 0 or tm % t_len == 0)
    tiles_per_seq = max(t_len // tm, 1)
    n_seq = tm // seg
    row = lambda w: pl.BlockSpec((tm, w), lambda i: (i, 0))
    full = lambda a, b: pl.BlockSpec((a, b), lambda i: (0, 0))
    heads = lambda w: pl.BlockSpec((n_seq, N_HEADS_F, seg, w),
                                   lambda i: (i // tiles_per_seq, 0, i % tiles_per_seq, 0))
    out_specs = [heads(HEAD_DIM), heads(HEAD_DIM), row(N_HEADS_F), row(LANES), heads(AUG_DIM)]
    out_shape = [jax.ShapeDtypeStruct((bsz, N_HEADS_F, t_len, HEAD_DIM), F32),
                 jax.ShapeDtypeStruct((bsz, N_HEADS_F, t_len, HEAD_DIM), F32),
                 jax.ShapeDtypeStruct((m, N_HEADS_F), F32),
                 jax.ShapeDtypeStruct((m, LANES), F32),
                 jax.ShapeDtypeStruct((bsz, N_HEADS_F, t_len, AUG_DIM), BF16)]
    if emit_vt:
        assert tiles_per_seq * tm == t_len
        out_specs.append(pl.BlockSpec(
            (1, N_HEADS_F, 1, HEAD_DIM, tm),
            lambda i: (i // tiles_per_seq, 0, i % tiles_per_seq, 0, 0)))
        out_shape.append(
            jax.ShapeDtypeStruct((bsz, N_HEADS_F, tiles_per_seq, HEAD_DIM, tm), BF16))
    return pl.pallas_call(
        functools.partial(_kvf_kernel, seg=seg, tiles_per_seq=tiles_per_seq, tn=512,
                          emit_vt=emit_vt),
        grid=(m // tm,),
        in_specs=[row(d), full(1, d), full(d, 2 * F_DIM), full(d, LANES), full(1, LANES),
                  row(LANES)],
        out_specs=out_specs,
        out_shape=out_shape,
        scratch_shapes=[pltpu.VMEM((tm, d), BF16), pltpu.VMEM((1, LANES), F32)],
        compiler_params=_params("arbitrary"),
        name="kvf",
    )(x, g.reshape(1, d), w_kv, w_f, b_f, c0)


def _qproj_kernel(x_ref, g_ref, wq_ref, c_ref, qx_ref, h_ref, *, tn):
    _rmsnorm_rows(x_ref, g_ref, h_ref)
    h = h_ref[...]
    c = c_ref[...] * LOG2E
    heads_per_chunk = tn // HEAD_DIM
    for n in range(F_DIM // tn):
        y = jnp.dot(h, wq_ref[:, n * tn:(n + 1) * tn],
                    preferred_element_type=F32) * (SCALE * LOG2E)
        for hh in range(heads_per_chunk):
            head = n * heads_per_chunk + hh
            base = head * AUG_DIM
            qx_ref[:, base:base + HEAD_DIM] = y[:, hh * HEAD_DIM:(hh + 1) * HEAD_DIM].astype(BF16)
            qx_ref[:, base + HEAD_DIM:base + AUG_DIM] = _aug_block(c[:, head:head + 1], False)


def _qproj(x, g, w_q, c, *, tm):
    m, d = x.shape
    tm = min(tm, m)
    assert m % tm == 0
    return pl.pallas_call(
        functools.partial(_qproj_kernel, tn=512),
        grid=(m // tm,),
        in_specs=[pl.BlockSpec((tm, d), lambda i: (i, 0)),
                  pl.BlockSpec((1, d), lambda i: (0, 0)),
                  pl.BlockSpec((d, F_DIM), lambda i: (0, 0)),
                  pl.BlockSpec((tm, LANES), lambda i: (i, 0))],
        out_specs=pl.BlockSpec((tm, N_HEADS_F * AUG_DIM), lambda i: (i, 0)),
        out_shape=jax.ShapeDtypeStruct((m, N_HEADS_F * AUG_DIM), BF16),
        scratch_shapes=[pltpu.VMEM((tm, d), BF16)],
        compiler_params=_params("parallel"),
        name="qproj",
    )(x, g.reshape(1, d), w_q, c)


def _fox_prompt_kernel(qx_ref, kx_ref, vt_ref, o_ref, m_ref, l_ref, acc_ref, s_ref, *, cw):
    qi = pl.program_id(2)
    tq = qx_ref.shape[0]
    tkv = vt_ref.shape[4]
    tk = s_ref.shape[1]
    n_chunks = tq // cw
    m_ref[...] = jnp.full_like(m_ref, -jnp.inf)
    l_ref[...] = jnp.zeros_like(l_ref)
    acc_ref[...] = jnp.zeros_like(acc_ref)

    def scores(kb, c, key_minus_query):
        k = kx_ref[0, 0, pl.ds(pl.multiple_of(kb * tk, tk), tk), :]
        q = qx_ref[c * cw:(c + 1) * cw, :]
        s = lax.dot_general(k, q, (((1,), (1,)), ((), ())), preferred_element_type=F32)
        if key_minus_query is not None:
            key = lax.broadcasted_iota(jnp.int32, (tk, cw), 0) + key_minus_query
            query = lax.broadcasted_iota(jnp.int32, (tk, cw), 1)
            s = jnp.where(key <= query, s, -jnp.inf)
        s_ref[c] = s
        return jnp.max(s, axis=0, keepdims=True)

    def accumulate(kb, c, m_blk):
        m_old = m_ref[c]
        m_new = jnp.maximum(m_old, m_blk)
        alpha = jnp.exp2(m_old - m_new)
        p = jnp.exp2(s_ref[c] - m_new)
        l_ref[c] = alpha * l_ref[c] + jnp.sum(p, axis=0, keepdims=True)
        pb = p.astype(BF16)
        upd = None
        for i in range(tk // tkv):
            part = jnp.dot(vt_ref[0, 0, kb * (tk // tkv) + i], pb[i * tkv:(i + 1) * tkv, :],
                           preferred_element_type=F32)
            upd = part if upd is None else upd + part
        acc_ref[c] = alpha * acc_ref[c] + upd
        m_ref[c] = m_new

    def fold(kb, offsets):
        maxes = {c: scores(kb, c, off) for c, off in offsets.items()}
        for c in offsets:
            accumulate(kb, c, maxes[c])

    n_full = qi * (tq // tk)

    def body(kb, carry):
        fold(kb, {c: None for c in range(n_chunks)})
        return carry

    lax.fori_loop(0, n_full, body, 0)

    for jj in range(tq // tk):
        offsets = {}
        for c in range(n_chunks):
            k_lo, q_lo = jj * tk, c * cw
            if k_lo >= q_lo + cw:
                continue
            offsets[c] = k_lo - q_lo if k_lo + tk - 1 > q_lo else None
        fold(n_full + jj, offsets)

    for c in range(n_chunks):
        o_ref[c * cw:(c + 1) * cw, :] = (acc_ref[c] / l_ref[c]).T.astype(o_ref.dtype)


def _fox_prompt(qx, kx, vt, *, bsz, t_len, tq, tk, cw):
    m = bsz * t_len
    nq = t_len // tq
    tkv = vt.shape[4]
    assert t_len % tq == 0 and tq % tk == 0 and tq % cw == 0 and tk % tkv == 0
    return pl.pallas_call(
        functools.partial(_fox_prompt_kernel, cw=cw),
        grid=(bsz, N_HEADS_F, nq),
        in_specs=[pl.BlockSpec((tq, AUG_DIM), lambda b, h, qi: (b * nq + qi, h)),
                  pl.BlockSpec((1, 1, t_len, AUG_DIM), lambda b, h, qi: (b, h, 0, 0)),
                  pl.BlockSpec((1, 1, t_len // tkv, HEAD_DIM, tkv),
                               lambda b, h, qi: (b, h, 0, 0, 0))],
        out_specs=pl.BlockSpec((tq, HEAD_DIM), lambda b, h, qi: (b * nq + qi, h)),
        out_shape=jax.ShapeDtypeStruct((m, F_DIM), F32),
        scratch_shapes=[pltpu.VMEM((tq // cw, 1, cw), F32), pltpu.VMEM((tq // cw, 1, cw), F32),
                        pltpu.VMEM((tq // cw, HEAD_DIM, cw), F32),
                        pltpu.VMEM((tq // cw, tk, cw), F32)],
        compiler_params=_params("parallel", "parallel", "arbitrary"),
        name="fox_prompt",
    )(qx, kx, vt)


def _fox_cached_kernel(qx_ref, kxn_ref, vn_ref, cn_ref, ck_ref, cv_ref, cp_ref, o_ref,
                       m_ref, l_ref, acc_ref):
    b = pl.program_id(0)
    ki = pl.program_id(1)
    t_new = qx_ref.shape[0]

    @pl.when(ki == 0)
    def _():
        m_ref[...] = jnp.full_like(m_ref, -jnp.inf)
        l_ref[...] = jnp.zeros_like(l_ref)
        acc_ref[...] = jnp.zeros_like(acc_ref)

    def update(h, s, v):
        cols = slice(h * HEAD_DIM, (h + 1) * HEAD_DIM)
        m_old = m_ref[:, h:h + 1]
        m_new = jnp.maximum(m_old, jnp.max(s, axis=-1, keepdims=True))
        alpha = jnp.exp2(m_old - m_new)
        p = jnp.exp2(s - m_new)
        l_ref[:, h:h + 1] = alpha * l_ref[:, h:h + 1] + jnp.sum(p, axis=-1, keepdims=True)
        acc_ref[:, cols] = alpha * acc_ref[:, cols] + jnp.dot(p.astype(BF16), v,
                                                              preferred_element_type=F32)
        m_ref[:, h:h + 1] = m_new

    for h in range(N_HEADS_F):
        cols = slice(h * HEAD_DIM, (h + 1) * HEAD_DIM)
        q = qx_ref[:, h * AUG_DIM:h * AUG_DIM + HEAD_DIM]
        k = ck_ref[0, h].astype(BF16)
        s = lax.dot_general(q, k, (((1,), (1,)), ((), ())), preferred_element_type=F32)
        s = s + (cn_ref[:, h:h + 1] - cp_ref[h, pl.ds(b, 1), :]) * LOG2E
        update(h, s, cv_ref[0, h].astype(BF16))

    @pl.when(ki == pl.num_programs(1) - 1)
    def _():
        row = lax.broadcasted_iota(jnp.int32, (t_new, t_new), 0)
        col = lax.broadcasted_iota(jnp.int32, (t_new, t_new), 1)
        for h in range(N_HEADS_F):
            cols = slice(h * HEAD_DIM, (h + 1) * HEAD_DIM)
            xcols = slice(h * AUG_DIM, (h + 1) * AUG_DIM)
            s = lax.dot_general(qx_ref[:, xcols], kxn_ref[0, h], (((1,), (1,)), ((), ())),
                                preferred_element_type=F32)
            s = jnp.where(col <= row, s, -jnp.inf)
            update(h, s, vn_ref[0, h].astype(BF16))
            o_ref[:, cols] = (acc_ref[:, cols] / l_ref[:, h:h + 1]).astype(o_ref.dtype)


def _fox_cached(qx, kx_new, v_new, c_new, cache_k, cache_v, c_past, *, bsz, t_len, tk):
    m = bsz * t_len
    past = cache_k.shape[2]
    assert past % tk == 0 and past > 0
    seq = lambda w: pl.BlockSpec((t_len, w), lambda b, ki: (b, 0))
    new = lambda w: pl.BlockSpec((1, N_HEADS_F, t_len, w), lambda b, ki: (b, 0, 0, 0))
    cache = pl.BlockSpec((1, N_HEADS_F, tk, HEAD_DIM), lambda b, ki: (b, 0, ki, 0))
    return pl.pallas_call(
        _fox_cached_kernel,
        grid=(bsz, past // tk),
        in_specs=[seq(N_HEADS_F * AUG_DIM), new(AUG_DIM), new(HEAD_DIM), seq(LANES),
                  cache, cache,
                  pl.BlockSpec((N_HEADS_F, bsz, tk), lambda b, ki: (0, 0, ki))],
        out_specs=seq(F_DIM),
        out_shape=jax.ShapeDtypeStruct((m, F_DIM), F32),
        scratch_shapes=[pltpu.VMEM((t_len, LANES), F32), pltpu.VMEM((t_len, LANES), F32),
                        pltpu.VMEM((t_len, F_DIM), F32)],
        compiler_params=_params("parallel", "arbitrary"),
        name="fox_cached",
    )(qx, kx_new, v_new, c_new, cache_k, cache_v, c_past)


def _cumsum_lanes_kernel(x_ref, o_ref):
    rows, n = x_ref.shape
    r = lax.broadcasted_iota(jnp.int32, (LANES, LANES), 0)
    c = lax.broadcasted_iota(jnp.int32, (LANES, LANES), 1)
    upper = jnp.where(r <= c, 1.0, 0.0).astype(BF16)
    carry = jnp.zeros((rows, 1), F32)
    for j in range(n // LANES):
        cols = slice(j * LANES, (j + 1) * LANES)
        hi, mid, lo = _split3(x_ref[:, cols])
        local = (jnp.dot(hi.astype(BF16), upper, preferred_element_type=F32)
                 + jnp.dot(mid.astype(BF16), upper, preferred_element_type=F32)
                 + jnp.dot(lo.astype(BF16), upper, preferred_element_type=F32))
        o_ref[:, cols] = local + carry
        carry = carry + local[:, LANES - 1:LANES]


def _cumsum_lanes(x):
    rows, n = x.shape
    return pl.pallas_call(
        _cumsum_lanes_kernel,
        grid=(1,),
        in_specs=[pl.BlockSpec((rows, n), lambda i: (0, 0))],
        out_specs=pl.BlockSpec((rows, n), lambda i: (0, 0)),
        out_shape=jax.ShapeDtypeStruct((rows, n), F32),
        compiler_params=_params("arbitrary"),
        name="cumsum_lanes",
    )(x)


def _trunk(x, conv_state, mem_k, mem_v, past, w):
    bsz, t_len, d = x.shape
    m = bsz * t_len
    x0 = x.reshape(m, d)

    proj = _norm_proj(x0, w["g_norm"][0], w["w_in_a"], tm=1024, tn=512)
    mixed, new_state = _mixer_a(proj, conv_state, mem_k[0], mem_v[0], w["conv_w"],
                                bsz=bsz, t_len=t_len, tt=256)
    x1 = _out_proj(mixed, w["w_out"][0], x0, w["g_final"], final_norm=False, tm=512)

    if past is None:
        c0 = jnp.zeros((m, LANES), F32)
    else:
        cache_k, cache_v, cache_logf = past
        past_len = cache_k.shape[1]
        cache_k = jnp.transpose(cache_k, (0, 2, 1, 3))
        cache_v = jnp.transpose(cache_v, (0, 2, 1, 3))
        logf_t = jnp.transpose(cache_logf, (2, 0, 1)).reshape(N_HEADS_F * bsz, past_len)
        c_past = _cumsum_lanes(logf_t).reshape(N_HEADS_F, bsz, past_len)
        c_end = jnp.pad(c_past[:, :, past_len - 1].T, ((0, 0), (0, LANES - N_HEADS_F)))
        c0 = jnp.repeat(c_end, t_len, axis=0)
    kvf_out = _kvf(x1, w["g_kv"], w["w_kv"], w["w_f"], w["b_f"], c0,
                   bsz=bsz, t_len=t_len, tm=256, emit_vt=past is None)
    k_new, v_new, logf, c, kx = kvf_out[:5]
    qx = _qproj(x1, w["g_norm"][1], w["w_q"], c, tm=512)
    rest = _norm_proj(x1, w["g_norm"][1], w["w_in_b_rest"], tm=1024, tn=512)

    if past is None:
        o = _fox_prompt(qx, kx, kvf_out[5], bsz=bsz, t_len=t_len, tq=1024, tk=512, cw=256)
    else:
        o = _fox_cached(qx, kx, v_new, c, cache_k, cache_v, c_past,
                        bsz=bsz, t_len=t_len, tk=512)
    mixed = _mixer_b(o, rest, mem_k[1], mem_v[1], bsz=bsz, t_len=t_len, tt=256)
    y = _out_proj(mixed, w["w_out"][1], x1, w["g_final"], final_norm=True, tm=512)
    return (y.reshape(bsz, t_len, d), new_state[None],
            jnp.transpose(k_new, (0, 2, 1, 3)), jnp.transpose(v_new, (0, 2, 1, 3)),
            logf.reshape(bsz, t_len, N_HEADS_F))


def kernel(x_prompt, x_sample, state_conv, cache_k, cache_v, cache_logf, cache_mem_k, cache_mem_v,
           mem_prompt, g_norm, w_in_a, conv_w, w_in_b, w_out, g_mem, w_mem_kv, g_kv, w_kvf, b_f,
           g_final):
    depth = g_norm.shape[0]
    bp = x_prompt.shape[0]
    bs = x_sample.shape[0]
    w = {
        "g_norm": g_norm, "g_kv": g_kv, "g_final": g_final, "conv_w": conv_w[0],
        "w_in_a": w_in_a[0].astype(BF16),
        "w_q": w_in_b[0][:, :F_DIM].astype(BF16),
        "w_in_b_rest": w_in_b[0][:, F_DIM:].astype(BF16),
        "w_out": w_out.astype(BF16),
        "w_kv": w_kvf[:, :2 * F_DIM].astype(BF16),
        "w_f": jnp.pad(w_kvf[:, 2 * F_DIM:], ((0, 0), (0, LANES - N_HEADS_F))).astype(BF16),
        "b_f": jnp.pad(b_f, (0, LANES - N_HEADS_F)).reshape(1, LANES),
    }

    mem_rows = mem_prompt.reshape(bp * N_MEM, D_MODEL)
    mem_kv = [_norm_proj(mem_rows, g_mem[i], w_mem_kv[i].astype(BF16), tm=512, tn=512)
              for i in range(depth)]
    p_mem_k = jnp.stack([a[:, :M_DIM] for a in mem_kv]).reshape(depth, bp, N_MEM, N_HEADS_M, HEAD_DIM)
    p_mem_v = jnp.stack([a[:, M_DIM:] for a in mem_kv]).reshape(depth, bp, N_MEM, N_HEADS_M, HEAD_DIM)

    zero_conv = jnp.zeros((bp, CONV_W - 1, CONV_DIM), F32)
    y_p, p_state, p_k, p_v, p_logf = _trunk(
        x_prompt, zero_conv, p_mem_k.reshape(depth, bp, N_MEM, M_DIM),
        p_mem_v.reshape(depth, bp, N_MEM, M_DIM), None, w)
    y_s, s_state, s_k, s_v, s_logf = _trunk(
        x_sample, state_conv[0], cache_mem_k.reshape(depth, bs, N_MEM, M_DIM),
        cache_mem_v.reshape(depth, bs, N_MEM, M_DIM), (cache_k, cache_v, cache_logf), w)
    return (y_p, y_s, p_state, p_k, p_v, p_logf, p_mem_k, p_mem_v, s_state, s_k, s_v, s_logf)
```

```python
import functools
import math

import jax
import jax.numpy as jnp
from jax import lax
from jax.experimental import pallas as pl
from jax.experimental.pallas import tpu as pltpu

F32 = jnp.float32
BF16 = jnp.bfloat16

D_MODEL = 2048
CONV_W = 3
CONV_DIM = 1536
HEAD_DIM = 128
N_HEADS_F = 12
F_DIM = N_HEADS_F * HEAD_DIM
N_MEM = 256
N_HEADS_M = 4
M_DIM = N_HEADS_M * HEAD_DIM
MIX_DIM = CONV_DIM + M_DIM
EPS = 1e-6
SCALE = 1.0 / math.sqrt(HEAD_DIM)
LOG2E = math.log2(math.e)

LANES = 128
AUG_DIM = 2 * HEAD_DIM
VT_ROWS = HEAD_DIM + 16
VMEM_LIMIT = 56 * 1024 * 1024


def _params(*sem):
    return pltpu.CompilerParams(dimension_semantics=sem, vmem_limit_bytes=VMEM_LIMIT)


def _resident(*shape):
    return pl.BlockSpec(shape, lambda *_: (0,) * len(shape), pipeline_mode=pl.Buffered(1))


def _silu(z):
    return z * (1.0 / (1.0 + jnp.exp(-z)))


def _split3(c):
    hi = c.astype(BF16).astype(F32)
    r = c - hi
    mid = r.astype(BF16).astype(F32)
    lo = (r - mid).astype(BF16).astype(F32)
    return hi, mid, lo


def _aug_block(c_col, key_side):
    rows = c_col.shape[0]
    hi, mid, lo = _split3(-c_col if key_side else c_col)
    lane = lax.broadcasted_iota(jnp.int32, (rows, LANES), 1)
    term0 = 3 if key_side else 0
    one0 = 0 if key_side else 3
    terms = jnp.where(lane == term0, hi,
                      jnp.where(lane == term0 + 1, mid, jnp.where(lane == term0 + 2, lo, 0.0)))
    return jnp.where((lane >= one0) & (lane < one0 + 3), 1.0, terms).astype(BF16)


def _rmsnorm_rows(x_ref, g_ref, h_ref, rows_per_chunk=64):
    tm = x_ref.shape[0]
    g = g_ref[...]

    def body(r, carry):
        rows = pl.ds(pl.multiple_of(r * rows_per_chunk, rows_per_chunk), rows_per_chunk)
        x = x_ref[rows, :]
        ms = jnp.mean(x * x, axis=-1, keepdims=True)
        h_ref[rows, :] = (x * lax.rsqrt(ms + EPS) * g).astype(h_ref.dtype)
        return carry

    lax.fori_loop(0, tm // rows_per_chunk, body, 0)


def _norm_proj_kernel(x_ref, g_ref, w_ref, o_ref, h_ref):
    @pl.when(pl.program_id(1) == 0)
    def _():
        _rmsnorm_rows(x_ref, g_ref, h_ref)

    o_ref[...] = jnp.dot(h_ref[...], w_ref[...], preferred_element_type=F32).astype(o_ref.dtype)


def _norm_proj(x, g, w, *, tm, tn, out_dtype=F32):
    m, d = x.shape
    n = w.shape[1]
    tm = min(tm, m)
    assert m % tm == 0 and n % tn == 0
    return pl.pallas_call(
        _norm_proj_kernel,
        grid=(m // tm, n // tn),
        in_specs=[
            pl.BlockSpec((tm, d), lambda i, j: (i, 0)),
            pl.BlockSpec((1, d), lambda i, j: (0, 0)),
            pl.BlockSpec((d, tn), lambda i, j: (0, j)),
        ],
        out_specs=pl.BlockSpec((tm, tn), lambda i, j: (i, j)),
        out_shape=jax.ShapeDtypeStruct((m, n), out_dtype),
        scratch_shapes=[pltpu.VMEM((tm, d), BF16)],
        compiler_params=_params("parallel", "arbitrary"),
        name="norm_proj",
    )(x, g.reshape(1, d), w)


def _memory_attention_into(qm_ref, zm_ref, mk_ref, mv_ref, mixed_ref):
    for h in range(N_HEADS_M):
        cols = slice(h * HEAD_DIM, (h + 1) * HEAD_DIM)
        q = (qm_ref[:, cols].astype(F32) * SCALE).astype(BF16)
        k = mk_ref[0, :, cols].astype(BF16)
        v = mv_ref[0, :, cols].astype(BF16)
        s = lax.dot_general(q, k, (((1,), (1,)), ((), ())), preferred_element_type=F32)
        m = jnp.max(s, axis=-1, keepdims=True)
        p = jnp.exp(s - m)
        l = jnp.sum(p, axis=-1, keepdims=True)
        o = jnp.dot(p.astype(BF16), v, preferred_element_type=F32) / l
        out_cols = slice(CONV_DIM + h * HEAD_DIM, CONV_DIM + (h + 1) * HEAD_DIM)
        mixed_ref[:, out_cols] = (o * _silu(zm_ref[:, cols].astype(F32))).astype(mixed_ref.dtype)


def _mixer_a_kernel(bg_ref, cg_ref, u_ref, zc_ref, cgh_ref, uh_ref, st_ref, qm_ref, zm_ref,
                    mk_ref, mv_ref, cw_ref, mixed_ref, nst_ref):
    t = pl.program_id(1)
    tt = bg_ref.shape[0]
    first = t == 0
    row = lax.broadcasted_iota(jnp.int32, (tt, LANES), 0)
    hr = cgh_ref.shape[0]
    for c in range(CONV_DIM // LANES):
        cols = slice(c * LANES, (c + 1) * LANES)
        ci = cg_ref[:, cols].astype(F32) * u_ref[:, cols].astype(F32)
        halo = cgh_ref[:, cols].astype(F32) * uh_ref[:, cols].astype(F32)
        st = st_ref[0, :, cols]
        prev1 = jnp.where(first, st[1:2, :], halo[hr - 1:hr, :])
        prev2 = jnp.where(first, st[0:1, :], halo[hr - 2:hr - 1, :])
        s1 = jnp.where(row == 0, prev1, pltpu.roll(ci, 1, axis=0))
        s2 = jnp.where(row == 0, prev2, jnp.where(row == 1, prev1, pltpu.roll(ci, 2, axis=0)))
        w = cw_ref[:, cols]
        conv = w[0:1, :] * s2 + w[1:2, :] * s1 + w[2:3, :] * ci
        branch = bg_ref[:, cols].astype(F32) * conv * _silu(zc_ref[:, cols].astype(F32))
        mixed_ref[:, cols] = branch.astype(mixed_ref.dtype)

        @pl.when(t == pl.num_programs(1) - 1)
        def _():
            nst_ref[0, :, cols] = ci[tt - 2:tt, :]

    _memory_attention_into(qm_ref, zm_ref, mk_ref, mv_ref, mixed_ref)


def _mixer_a(proj, state, mem_k, mem_v, conv_w, *, bsz, t_len, tt):
    m = bsz * t_len
    tt = min(tt, t_len)
    nt = t_len // tt
    hr = 16
    assert t_len % tt == 0 and tt % hr == 0
    wide = lambda k: pl.BlockSpec((tt, CONV_DIM), lambda b, t: (b * nt + t, k))
    halo = lambda k: pl.BlockSpec(
        (hr, CONV_DIM), lambda b, t: (jnp.maximum((b * nt + t) * (tt // hr) - 1, 0), k))
    narrow = lambda k: pl.BlockSpec((tt, M_DIM), lambda b, t: (b * nt + t, k))
    mem = pl.BlockSpec((1, N_MEM, M_DIM), lambda b, t: (b, 0, 0))
    q_col = 4 * CONV_DIM // M_DIM
    return pl.pallas_call(
        _mixer_a_kernel,
        grid=(bsz, nt),
        in_specs=[wide(0), wide(1), wide(2), wide(3), halo(1), halo(2),
                  pl.BlockSpec((1, CONV_W - 1, CONV_DIM), lambda b, t: (b, 0, 0)),
                  narrow(q_col), narrow(q_col + 1), mem, mem,
                  pl.BlockSpec((CONV_W, CONV_DIM), lambda b, t: (0, 0))],
        out_specs=[pl.BlockSpec((tt, MIX_DIM), lambda b, t: (b * nt + t, 0)),
                   pl.BlockSpec((1, CONV_W - 1, CONV_DIM), lambda b, t: (b, 0, 0))],
        out_shape=[jax.ShapeDtypeStruct((m, MIX_DIM), BF16),
                   jax.ShapeDtypeStruct((bsz, CONV_W - 1, CONV_DIM), F32)],
        compiler_params=_params("parallel", "arbitrary"),
        name="mixer_a",
    )(proj, proj, proj, proj, proj, proj, state, proj, proj, mem_k, mem_v, conv_w)


def _mixer_b_kernel(o_ref, zf_ref, qm_ref, zm_ref, mk_ref, mv_ref, mixed_ref):
    for c in range(F_DIM // LANES):
        cols = slice(c * LANES, (c + 1) * LANES)
        mixed_ref[:, cols] = (o_ref[:, cols].astype(F32)
                              * _silu(zf_ref[:, cols].astype(F32))).astype(mixed_ref.dtype)
    _memory_attention_into(qm_ref, zm_ref, mk_ref, mv_ref, mixed_ref)


def _mixer_b(o, rest, mem_k, mem_v, *, bsz, t_len, tt):
    m = bsz * t_len
    tt = min(tt, t_len)
    nt = t_len // tt
    assert t_len % tt == 0
    narrow = lambda k: pl.BlockSpec((tt, M_DIM), lambda b, t: (b * nt + t, k))
    mem = pl.BlockSpec((1, N_MEM, M_DIM), lambda b, t: (b, 0, 0))
    q_col = F_DIM // M_DIM
    return pl.pallas_call(
        _mixer_b_kernel,
        grid=(bsz, nt),
        in_specs=[pl.BlockSpec((tt, F_DIM), lambda b, t: (b * nt + t, 0)),
                  pl.BlockSpec((tt, F_DIM), lambda b, t: (b * nt + t, 0)),
                  narrow(q_col), narrow(q_col + 1), mem, mem],
        out_specs=pl.BlockSpec((tt, MIX_DIM), lambda b, t: (b * nt + t, 0)),
        out_shape=jax.ShapeDtypeStruct((m, MIX_DIM), BF16),
        compiler_params=_params("parallel", "parallel"),
        name="mixer_b",
    )(o, rest, rest, rest, mem_k, mem_v)


def _out_proj_kernel(a_ref, w_ref, x_ref, g_ref, o_ref, *, final_norm, tn, rows_per_chunk=64):
    n = o_ref.shape[1]
    for c in range(n // tn):
        cols = slice(c * tn, (c + 1) * tn)
        o_ref[:, cols] = x_ref[:, cols] + jnp.dot(a_ref[...], w_ref[:, cols],
                                                  preferred_element_type=F32)
    if final_norm:
        g = g_ref[...]

        def body(r, carry):
            rows = pl.ds(pl.multiple_of(r * rows_per_chunk, rows_per_chunk), rows_per_chunk)
            x = o_ref[rows, :]
            ms = jnp.mean(x * x, axis=-1, keepdims=True)
            o_ref[rows, :] = x * lax.rsqrt(ms + EPS) * g
            return carry

        lax.fori_loop(0, o_ref.shape[0] // rows_per_chunk, body, 0)


def _out_proj(a, w, x, g, *, final_norm, tm):
    m, k = a.shape
    n = w.shape[1]
    tm = min(tm, m)
    assert m % tm == 0
    return pl.pallas_call(
        functools.partial(_out_proj_kernel, final_norm=final_norm, tn=512),
        grid=(m // tm,),
        in_specs=[pl.BlockSpec((tm, k), lambda i: (i, 0)),
                  _resident(k, n),
                  pl.BlockSpec((tm, n), lambda i: (i, 0)),
                  _resident(1, n)],
        out_specs=pl.BlockSpec((tm, n), lambda i: (i, 0)),
        out_shape=jax.ShapeDtypeStruct((m, n), F32),
        compiler_params=_params("parallel"),
        name="out_proj_norm" if final_norm else "out_proj",
    )(a, w, x, g.reshape(1, n))


def _kvf_kernel(x_ref, g_ref, wkv_ref, wf_ref, bf_ref, c0_ref, *refs,
                seg, tiles_per_seq, tn, emit_vt):
    if emit_vt:
        k_ref, v_ref, logf_ref, c_ref, kx_ref, vt_ref, h_ref, carry_ref = refs
    else:
        k_ref, v_ref, logf_ref, c_ref, kx_ref, h_ref, carry_ref = refs
    n_seq, seq_rows = k_ref.shape[0], k_ref.shape[2]
    i = pl.program_id(0)
    tm = x_ref.shape[0]
    _rmsnorm_rows(x_ref, g_ref, h_ref)
    h = h_ref[...]

    logit = jnp.dot(h, wf_ref[...], preferred_element_type=F32) + bf_ref[...]
    logf = jnp.minimum(logit, 0.0) - jnp.log1p(jnp.exp(-jnp.abs(logit)))
    logf_ref[...] = logf[:, :N_HEADS_F]
    row = lax.broadcasted_iota(jnp.int32, (tm, tm), 0)
    col = lax.broadcasted_iota(jnp.int32, (tm, tm), 1)
    tri = col <= row
    if seg < tm:
        shift = seg.bit_length() - 1
        tri = tri & ((col >> shift) == (row >> shift))
    tri = jnp.where(tri, 1.0, 0.0).astype(BF16)
    hi, mid, lo = _split3(logf)
    c = (jnp.dot(tri, hi.astype(BF16), preferred_element_type=F32)
         + jnp.dot(tri, mid.astype(BF16), preferred_element_type=F32)
         + jnp.dot(tri, lo.astype(BF16), preferred_element_type=F32))
    if tiles_per_seq > 1:
        @pl.when(i % tiles_per_seq == 0)
        def _():
            carry_ref[...] = c0_ref[0:1, :]

        c = c + carry_ref[...]
        carry_ref[...] = c[tm - 1:tm, :]
    else:
        c = c + c0_ref[...]
    c_ref[...] = c

    heads_per_chunk = tn // HEAD_DIM
    c2 = c * LOG2E
    for n in range(2 * F_DIM // tn):
        y = jnp.dot(h, wkv_ref[:, n * tn:(n + 1) * tn], preferred_element_type=F32)
        is_key = n < F_DIM // tn
        for hh in range(heads_per_chunk):
            head = (n % (F_DIM // tn)) * heads_per_chunk + hh
            yh = y[:, hh * HEAD_DIM:(hh + 1) * HEAD_DIM]
            if is_key:
                yb = yh.astype(BF16)
                aug = _aug_block(c2[:, head:head + 1], True)
            elif emit_vt:
                vt_ref[0, head, 0, 0:HEAD_DIM, :] = yh.T.astype(BF16)
                one_row = lax.broadcasted_iota(jnp.int32, (VT_ROWS - HEAD_DIM, tm), 0) == 0
                vt_ref[0, head, 0, HEAD_DIM:VT_ROWS, :] = jnp.where(one_row, 1.0, 0.0).astype(BF16)
            for s in range(n_seq):
                rows = slice(s * seq_rows, (s + 1) * seq_rows)
                if is_key:
                    k_ref[s, head] = yh[rows]
                    kx_ref[s, head, :, 0:HEAD_DIM] = yb[rows]
                    kx_ref[s, head, :, HEAD_DIM:AUG_DIM] = aug[rows]
                else:
                    v_ref[s, head] = yh[rows]


def _kvf(x, g, w_kv, w_f, b_f, c0, *, bsz, t_len, tm, emit_vt):
    m, d = x.shape
    tm = min(tm, m)
    seg = min(t_len, tm)
    assert m % tm == 0 and seg & (seg - 1) == 0 and (t_len % tm == 0 or tm % t_len == 0)
    tiles_per_seq = max(t_len // tm, 1)
    n_seq = tm // seg
    row = lambda w: pl.BlockSpec((tm, w), lambda i: (i, 0))
    heads = lambda w: pl.BlockSpec((n_seq, N_HEADS_F, seg, w),
                                   lambda i: (i // tiles_per_seq, 0, i % tiles_per_seq, 0))
    out_specs = [heads(HEAD_DIM), heads(HEAD_DIM), row(N_HEADS_F), row(LANES), heads(AUG_DIM)]
    out_shape = [jax.ShapeDtypeStruct((bsz, N_HEADS_F, t_len, HEAD_DIM), F32),
                 jax.ShapeDtypeStruct((bsz, N_HEADS_F, t_len, HEAD_DIM), F32),
                 jax.ShapeDtypeStruct((m, N_HEADS_F), F32),
                 jax.ShapeDtypeStruct((m, LANES), F32),
                 jax.ShapeDtypeStruct((bsz, N_HEADS_F, t_len, AUG_DIM), BF16)]
    if emit_vt:
        assert tiles_per_seq * tm == t_len
        out_specs.append(pl.BlockSpec(
            (1, N_HEADS_F, 1, VT_ROWS, tm),
            lambda i: (i // tiles_per_seq, 0, i % tiles_per_seq, 0, 0)))
        out_shape.append(
            jax.ShapeDtypeStruct((bsz, N_HEADS_F, tiles_per_seq, VT_ROWS, tm), BF16))
    return pl.pallas_call(
        functools.partial(_kvf_kernel, seg=seg, tiles_per_seq=tiles_per_seq, tn=512,
                          emit_vt=emit_vt),
        grid=(m // tm,),
        in_specs=[row(d), _resident(1, d), _resident(d, 2 * F_DIM), _resident(d, LANES),
                  _resident(1, LANES), row(LANES)],
        out_specs=out_specs,
        out_shape=out_shape,
        scratch_shapes=[pltpu.VMEM((tm, d), BF16), pltpu.VMEM((1, LANES), F32)],
        compiler_params=_params("arbitrary"),
        name="kvf",
    )(x, g.reshape(1, d), w_kv, w_f, b_f, c0)


def _qproj_kernel(x_ref, g_ref, wq_ref, c_ref, qx_ref, h_ref, *, tn):
    _rmsnorm_rows(x_ref, g_ref, h_ref)
    h = h_ref[...]
    c = c_ref[...] * LOG2E
    heads_per_chunk = tn // HEAD_DIM
    for n in range(F_DIM // tn):
        y = jnp.dot(h, wq_ref[:, n * tn:(n + 1) * tn],
                    preferred_element_type=F32) * (SCALE * LOG2E)
        for hh in range(heads_per_chunk):
            head = n * heads_per_chunk + hh
            base = head * AUG_DIM
            qx_ref[:, base:base + HEAD_DIM] = y[:, hh * HEAD_DIM:(hh + 1) * HEAD_DIM].astype(BF16)
            qx_ref[:, base + HEAD_DIM:base + AUG_DIM] = _aug_block(c[:, head:head + 1], False)


def _qproj(x, g, w_q, c, *, tm):
    m, d = x.shape
    tm = min(tm, m)
    assert m % tm == 0
    return pl.pallas_call(
        functools.partial(_qproj_kernel, tn=512),
        grid=(m // tm,),
        in_specs=[pl.BlockSpec((tm, d), lambda i: (i, 0)),
                  _resident(1, d),
                  _resident(d, F_DIM),
                  pl.BlockSpec((tm, LANES), lambda i: (i, 0))],
        out_specs=pl.BlockSpec((tm, N_HEADS_F * AUG_DIM), lambda i: (i, 0)),
        out_shape=jax.ShapeDtypeStruct((m, N_HEADS_F * AUG_DIM), BF16),
        scratch_shapes=[pltpu.VMEM((tm, d), BF16)],
        compiler_params=_params("parallel"),
        name="qproj",
    )(x, g.reshape(1, d), w_q, c)


def _fox_prompt_kernel(qx_ref, kx_ref, vt_ref, o_ref, m_ref, acc_ref, sa_ref, sb_ref, ma_ref, mb_ref,
                       *, cw):
    qi = pl.program_id(2)
    tq = qx_ref.shape[0]
    tk = vt_ref.shape[4]
    n_chunks = tq // cw
    buf_a, buf_b = (sa_ref, ma_ref), (sb_ref, mb_ref)
    m_ref[...] = jnp.full_like(m_ref, -jnp.inf)
    acc_ref[...] = jnp.zeros_like(acc_ref)

    def scores(kb, offsets, buf):
        s_ref, smax_ref = buf
        k = kx_ref[0, 0, pl.ds(pl.multiple_of(kb * tk, tk), tk), :]
        for c, key_minus_query in offsets.items():
            q = qx_ref[c * cw:(c + 1) * cw, :]
            s = lax.dot_general(k, q, (((1,), (1,)), ((), ())), preferred_element_type=F32)
            if key_minus_query is not None:
                key = lax.broadcasted_iota(jnp.int32, (tk, cw), 0) + key_minus_query
                query = lax.broadcasted_iota(jnp.int32, (tk, cw), 1)
                s = jnp.where(key <= query, s, -jnp.inf)
            s_ref[c] = s
            smax_ref[c] = jnp.max(s, axis=0, keepdims=True)

    def update(kb, chunks, buf):
        s_ref, smax_ref = buf
        vt = vt_ref[0, 0, kb]
        for c in chunks:
            m_old = m_ref[c]
            m_new = jnp.maximum(m_old, smax_ref[c])
            alpha = jnp.exp2(m_old - m_new)
            p = jnp.exp2(s_ref[c] - m_new).astype(BF16)
            acc_ref[c] = alpha * acc_ref[c] + jnp.dot(vt, p, preferred_element_type=F32)
            m_ref[c] = m_new

    every = {c: None for c in range(n_chunks)}
    own = []
    for jj in range(2):
        offsets = {}
        for c in range(n_chunks):
            k_lo, q_lo = jj * tk, c * cw
            if k_lo < q_lo + cw:
                offsets[c] = k_lo - q_lo if k_lo + tk - 1 > q_lo else None
        own.append(offsets)

    @pl.when(qi == 0)
    def _():
        scores(0, own[0], buf_a)
        scores(1, own[1], buf_b)
        update(0, own[0], buf_a)
        update(1, own[1], buf_b)

    @pl.when(qi > 0)
    def _():
        n = 2 * qi
        scores(0, every, buf_a)

        def pair(t, carry):
            j = 2 * t + 1
            scores(j, every, buf_b)
            update(j - 1, every, buf_a)
            scores(j + 1, every, buf_a)
            update(j, every, buf_b)
            return carry

        lax.fori_loop(0, qi - 1, pair, 0)
        scores(n - 1, every, buf_b)
        update(n - 2, every, buf_a)
        scores(n, own[0], buf_a)
        update(n - 1, every, buf_b)
        scores(n + 1, own[1], buf_b)
        update(n, own[0], buf_a)
        update(n + 1, own[1], buf_b)

    for c in range(n_chunks):
        acc = acc_ref[c]
        o = acc[0:HEAD_DIM, :] / acc[HEAD_DIM:HEAD_DIM + 1, :]
        o_ref[c * cw:(c + 1) * cw, :] = o.T.astype(o_ref.dtype)


def _fox_prompt(qx, kx, vt, *, bsz, t_len, tq, cw):
    m = bsz * t_len
    nq = t_len // tq
    tk = vt.shape[4]
    n_chunks = tq // cw
    assert t_len % tq == 0 and tq == 2 * tk and tq % cw == 0
    return pl.pallas_call(
        functools.partial(_fox_prompt_kernel, cw=cw),
        grid=(bsz, N_HEADS_F, nq),
        in_specs=[pl.BlockSpec((tq, AUG_DIM), lambda b, h, qi: (b * nq + qi, h)),
                  pl.BlockSpec((1, 1, t_len, AUG_DIM), lambda b, h, qi: (b, h, 0, 0)),
                  pl.BlockSpec((1, 1, t_len // tk, VT_ROWS, tk),
                               lambda b, h, qi: (b, h, 0, 0, 0))],
        out_specs=pl.BlockSpec((tq, HEAD_DIM), lambda b, h, qi: (b * nq + qi, h)),
        out_shape=jax.ShapeDtypeStruct((m, F_DIM), BF16),
        scratch_shapes=[pltpu.VMEM((n_chunks, 1, cw), F32),
                        pltpu.VMEM((n_chunks, VT_ROWS, cw), F32),
                        pltpu.VMEM((n_chunks, tk, cw), F32), pltpu.VMEM((n_chunks, tk, cw), F32),
                        pltpu.VMEM((n_chunks, 1, cw), F32), pltpu.VMEM((n_chunks, 1, cw), F32)],
        compiler_params=_params("parallel", "parallel", "arbitrary"),
        name="fox_prompt",
    )(qx, kx, vt)


def _fox_cached_kernel(qx_ref, kxn_ref, vn_ref, cn_ref, ck_ref, cv_ref, cp_ref, o_ref,
                       m_ref, l_ref, acc_ref, cq_ref, smax_ref, s_ref):
    b = pl.program_id(0)
    ki = pl.program_id(1)
    t_new = qx_ref.shape[0]

    @pl.when(ki == 0)
    def _():
        m_ref[...] = jnp.full_like(m_ref, -jnp.inf)
        l_ref[...] = jnp.zeros_like(l_ref)
        acc_ref[...] = jnp.zeros_like(acc_ref)
        for h in range(N_HEADS_F):
            cq_ref[h] = jnp.broadcast_to(cn_ref[:, h:h + 1] * LOG2E, (t_new, LANES))

    def update(h, s, s_max, v):
        n = s.shape[1]
        across = (lambda x: pltpu.repeat(x, n // LANES, axis=1)) if n >= LANES else (
            lambda x: x[:, :n])
        m_old = m_ref[h]
        m_new = jnp.maximum(m_old, s_max)
        alpha = jnp.exp2(m_old - m_new)
        p = jnp.exp2(s - across(m_new))
        l_ref[h] = alpha * l_ref[h] + jnp.sum(p, axis=-1, keepdims=True)
        acc_ref[h] = alpha * acc_ref[h] + jnp.dot(p.astype(BF16), v, preferred_element_type=F32)
        m_ref[h] = m_new

    tk = ck_ref.shape[2]
    for h in range(N_HEADS_F):
        q = qx_ref[:, h * AUG_DIM:h * AUG_DIM + HEAD_DIM]
        k = ck_ref[0, h].astype(BF16)
        s = lax.dot_general(q, k, (((1,), (1,)), ((), ())), preferred_element_type=F32)
        decay = pltpu.repeat(cq_ref[h], tk // LANES, axis=1) - cp_ref[h, pl.ds(b, 1), :] * LOG2E
        s = s + decay
        s_ref[h] = s
        smax_ref[h] = jnp.broadcast_to(jnp.max(s, axis=-1, keepdims=True), (t_new, LANES))
    for h in range(N_HEADS_F):
        update(h, s_ref[h], smax_ref[h], cv_ref[0, h].astype(BF16))

    @pl.when(ki == pl.num_programs(1) - 1)
    def _():
        row = lax.broadcasted_iota(jnp.int32, (t_new, t_new), 0)
        col = lax.broadcasted_iota(jnp.int32, (t_new, t_new), 1)
        for h in range(N_HEADS_F):
            cols = slice(h * HEAD_DIM, (h + 1) * HEAD_DIM)
            xcols = slice(h * AUG_DIM, (h + 1) * AUG_DIM)
            s = lax.dot_general(qx_ref[:, xcols], kxn_ref[0, h], (((1,), (1,)), ((), ())),
                                preferred_element_type=F32)
            s = jnp.where(col <= row, s, -jnp.inf)
            update(h, s, jnp.max(s, axis=-1, keepdims=True), vn_ref[0, h].astype(BF16))
            o_ref[:, cols] = (acc_ref[h] / l_ref[h]).astype(o_ref.dtype)


def _fox_cached(qx, kx_new, v_new, c_new, cache_k, cache_v, c_past, *, bsz, t_len, tk):
    m = bsz * t_len
    past = cache_k.shape[2]
    assert past % tk == 0 and past > 0
    seq = lambda w: pl.BlockSpec((t_len, w), lambda b, ki: (b, 0))
    new = lambda w: pl.BlockSpec((1, N_HEADS_F, t_len, w), lambda b, ki: (b, 0, 0, 0))
    cache = pl.BlockSpec((1, N_HEADS_F, tk, HEAD_DIM), lambda b, ki: (b, 0, ki, 0))
    return pl.pallas_call(
        _fox_cached_kernel,
        grid=(bsz, past // tk),
        in_specs=[seq(N_HEADS_F * AUG_DIM), new(AUG_DIM), new(HEAD_DIM), seq(LANES),
                  cache, cache,
                  pl.BlockSpec((N_HEADS_F, bsz, tk), lambda b, ki: (0, 0, ki))],
        out_specs=seq(F_DIM),
        out_shape=jax.ShapeDtypeStruct((m, F_DIM), BF16),
        scratch_shapes=[pltpu.VMEM((N_HEADS_F, t_len, LANES), F32) for _ in range(5)]
        + [pltpu.VMEM((N_HEADS_F, t_len, tk), F32)],
        compiler_params=_params("parallel", "arbitrary"),
        name="fox_cached",
    )(qx, kx_new, v_new, c_new, cache_k, cache_v, c_past)


def _cumsum_lanes_kernel(x_ref, o_ref):
    rows, n = x_ref.shape
    r = lax.broadcasted_iota(jnp.int32, (LANES, LANES), 0)
    c = lax.broadcasted_iota(jnp.int32, (LANES, LANES), 1)
    upper = jnp.where(r <= c, 1.0, 0.0).astype(BF16)
    carry = jnp.zeros((rows, 1), F32)
    for j in range(n // LANES):
        cols = slice(j * LANES, (j + 1) * LANES)
        hi, mid, lo = _split3(x_ref[:, cols])
        local = (jnp.dot(hi.astype(BF16), upper, preferred_element_type=F32)
                 + jnp.dot(mid.astype(BF16), upper, preferred_element_type=F32)
                 + jnp.dot(lo.astype(BF16), upper, preferred_element_type=F32))
        o_ref[:, cols] = local + carry
        carry = carry + local[:, LANES - 1:LANES]


def _cumsum_lanes(x):
    rows, n = x.shape
    return pl.pallas_call(
        _cumsum_lanes_kernel,
        grid=(1,),
        in_specs=[pl.BlockSpec((rows, n), lambda i: (0, 0))],
        out_specs=pl.BlockSpec((rows, n), lambda i: (0, 0)),
        out_shape=jax.ShapeDtypeStruct((rows, n), F32),
        compiler_params=_params("arbitrary"),
        name="cumsum_lanes",
    )(x)


def _trunk(x, conv_state, mem_k, mem_v, past, w):
    bsz, t_len, d = x.shape
    m = bsz * t_len
    x0 = x.reshape(m, d)

    proj = _norm_proj(x0, w["g_norm"][0], w["w_in_a"], tm=1024, tn=1024, out_dtype=BF16)
    mixed, new_state = _mixer_a(proj, conv_state, mem_k[0], mem_v[0], w["conv_w"],
                                bsz=bsz, t_len=t_len, tt=512)
    x1 = _out_proj(mixed, w["w_out"][0], x0, w["g_final"], final_norm=False, tm=512)

    if past is None:
        c0 = jnp.zeros((m, LANES), F32)
    else:
        cache_k, cache_v, cache_logf = past
        past_len = cache_k.shape[1]
        cache_k = jnp.transpose(cache_k, (0, 2, 1, 3))
        cache_v = jnp.transpose(cache_v, (0, 2, 1, 3))
        logf_t = jnp.transpose(cache_logf, (2, 0, 1)).reshape(N_HEADS_F * bsz, past_len)
        c_past = _cumsum_lanes(logf_t).reshape(N_HEADS_F, bsz, past_len)
        c_end = jnp.pad(c_past[:, :, past_len - 1].T, ((0, 0), (0, LANES - N_HEADS_F)))
        c0 = jnp.repeat(c_end, t_len, axis=0)
    kvf_out = _kvf(x1, w["g_kv"], w["w_kv"], w["w_f"], w["b_f"], c0,
                   bsz=bsz, t_len=t_len, tm=512, emit_vt=past is None)
    k_new, v_new, logf, c, kx = kvf_out[:5]
    qx = _qproj(x1, w["g_norm"][1], w["w_q"], c, tm=512)
    rest = _norm_proj(x1, w["g_norm"][1], w["w_in_b_rest"], tm=1024, tn=512, out_dtype=BF16)

    if past is None:
        o = _fox_prompt(qx, kx, kvf_out[5], bsz=bsz, t_len=t_len, tq=1024, cw=256)
    else:
        o = _fox_cached(qx, kx, v_new, c, cache_k, cache_v, c_past,
                        bsz=bsz, t_len=t_len, tk=512)
    mixed = _mixer_b(o, rest, mem_k[1], mem_v[1], bsz=bsz, t_len=t_len, tt=512)
    y = _out_proj(mixed, w["w_out"][1], x1, w["g_final"], final_norm=True, tm=512)
    return (y.reshape(bsz, t_len, d), new_state[None],
            jnp.transpose(k_new, (0, 2, 1, 3)), jnp.transpose(v_new, (0, 2, 1, 3)),
            logf.reshape(bsz, t_len, N_HEADS_F))


def kernel(x_prompt, x_sample, state_conv, cache_k, cache_v, cache_logf, cache_mem_k, cache_mem_v,
           mem_prompt, g_norm, w_in_a, conv_w, w_in_b, w_out, g_mem, w_mem_kv, g_kv, w_kvf, b_f,
           g_final):
    depth = g_norm.shape[0]
    bp = x_prompt.shape[0]
    bs = x_sample.shape[0]
    w = {
        "g_norm": g_norm, "g_kv": g_kv, "g_final": g_final, "conv_w": conv_w[0],
        "w_in_a": w_in_a[0].astype(BF16),
        "w_q": w_in_b[0][:, :F_DIM].astype(BF16),
        "w_in_b_rest": w_in_b[0][:, F_DIM:].astype(BF16),
        "w_out": w_out.astype(BF16),
        "w_kv": w_kvf[:, :2 * F_DIM].astype(BF16),
        "w_f": jnp.pad(w_kvf[:, 2 * F_DIM:], ((0, 0), (0, LANES - N_HEADS_F))).astype(BF16),
        "b_f": jnp.pad(b_f, (0, LANES - N_HEADS_F)).reshape(1, LANES),
    }

    mem_rows = mem_prompt.reshape(bp * N_MEM, D_MODEL)
    mem_kv = [_norm_proj(mem_rows, g_mem[i], w_mem_kv[i].astype(BF16), tm=512, tn=512)
              for i in range(depth)]
    p_mem_k = jnp.stack([a[:, :M_DIM] for a in mem_kv]).reshape(depth, bp, N_MEM, N_HEADS_M, HEAD_DIM)
    p_mem_v = jnp.stack([a[:, M_DIM:] for a in mem_kv]).reshape(depth, bp, N_MEM, N_HEADS_M, HEAD_DIM)

    zero_conv = jnp.zeros((bp, CONV_W - 1, CONV_DIM), F32)
    y_p, p_state, p_k, p_v, p_logf = _trunk(
        x_prompt, zero_conv, p_mem_k.reshape(depth, bp, N_MEM, M_DIM),
        p_mem_v.reshape(depth, bp, N_MEM, M_DIM), None, w)
    y_s, s_state, s_k, s_v, s_logf = _trunk(
        x_sample, state_conv[0], cache_mem_k.reshape(depth, bs, N_MEM, M_DIM),
        cache_mem_v.reshape(depth, bs, N_MEM, M_DIM), (cache_k, cache_v, cache_logf), w)
    return (y_p, y_s, p_state, p_k, p_v, p_logf, p_mem_k, p_mem_v, s_state, s_k, s_v, s_logf)
```

```python
import functools
import math

import jax
import jax.numpy as jnp
from jax import lax
from jax.experimental import pallas as pl
from jax.experimental.pallas import tpu as pltpu

F32 = jnp.float32
BF16 = jnp.bfloat16

D_MODEL = 2048
CONV_W = 3
CONV_DIM = 1536
HEAD_DIM = 128
N_HEADS_F = 12
F_DIM = N_HEADS_F * HEAD_DIM
N_MEM = 256
N_HEADS_M = 4
M_DIM = N_HEADS_M * HEAD_DIM
MIX_DIM = CONV_DIM + M_DIM
EPS = 1e-6
SCALE = 1.0 / math.sqrt(HEAD_DIM)
LOG2E = math.log2(math.e)

LANES = 128
AUG_DIM = 2 * HEAD_DIM
VT_ROWS = HEAD_DIM + 16
VMEM_LIMIT = 56 * 1024 * 1024


def _params(*sem):
    return pltpu.CompilerParams(dimension_semantics=sem, vmem_limit_bytes=VMEM_LIMIT)


def _resident(*shape):
    return pl.BlockSpec(shape, lambda *_: (0,) * len(shape), pipeline_mode=pl.Buffered(1))


def _tile_lanes(x, n):
    return jnp.concatenate([x] * n, axis=1)


def _silu(z):
    return z * (1.0 / (1.0 + jnp.exp(-z)))


def _split3(c):
    hi = c.astype(BF16).astype(F32)
    r = c - hi
    mid = r.astype(BF16).astype(F32)
    lo = (r - mid).astype(BF16).astype(F32)
    return hi, mid, lo


def _aug_block(c_col, key_side):
    rows = c_col.shape[0]
    hi, mid, lo = _split3(-c_col if key_side else c_col)
    lane = lax.broadcasted_iota(jnp.int32, (rows, LANES), 1)
    term0 = 3 if key_side else 0
    one0 = 0 if key_side else 3
    terms = jnp.where(lane == term0, hi,
                      jnp.where(lane == term0 + 1, mid, jnp.where(lane == term0 + 2, lo, 0.0)))
    return jnp.where((lane >= one0) & (lane < one0 + 3), 1.0, terms).astype(BF16)


def _rmsnorm_rows(x_ref, g_ref, h_ref, rows_per_chunk=64):
    tm = x_ref.shape[0]
    g = g_ref[...]

    def body(r, carry):
        rows = pl.ds(pl.multiple_of(r * rows_per_chunk, rows_per_chunk), rows_per_chunk)
        x = x_ref[rows, :]
        ms = jnp.mean(x * x, axis=-1, keepdims=True)
        h_ref[rows, :] = (x * lax.rsqrt(ms + EPS) * g).astype(h_ref.dtype)
        return carry

    lax.fori_loop(0, tm // rows_per_chunk, body, 0)


def _norm_proj_kernel(x_ref, g_ref, w_ref, o_ref, h_ref):
    @pl.when(pl.program_id(1) == 0)
    def _():
        _rmsnorm_rows(x_ref, g_ref, h_ref)

    o_ref[...] = jnp.dot(h_ref[...], w_ref[...], preferred_element_type=F32).astype(o_ref.dtype)


def _norm_proj(x, g, w, *, tm, tn, out_dtype=F32):
    m, d = x.shape
    n = w.shape[1]
    tm = min(tm, m)
    assert m % tm == 0 and n % tn == 0
    return pl.pallas_call(
        _norm_proj_kernel,
        grid=(m // tm, n // tn),
        in_specs=[
            pl.BlockSpec((tm, d), lambda i, j: (i, 0)),
            pl.BlockSpec((1, d), lambda i, j: (0, 0)),
            pl.BlockSpec((d, tn), lambda i, j: (0, j)),
        ],
        out_specs=pl.BlockSpec((tm, tn), lambda i, j: (i, j)),
        out_shape=jax.ShapeDtypeStruct((m, n), out_dtype),
        scratch_shapes=[pltpu.VMEM((tm, d), BF16)],
        compiler_params=_params("parallel", "arbitrary"),
        name="norm_proj",
    )(x, g.reshape(1, d), w)


K_CHUNK = 512


def _memory_attention_into(qm_ref, zm_ref, mk_ref, mv_ref, mixed_ref, n_seq, seq_rows):
    for s in range(n_seq):
        rows = slice(s * seq_rows, (s + 1) * seq_rows)
        for h in range(N_HEADS_M):
            cols = slice(h * HEAD_DIM, (h + 1) * HEAD_DIM)
            q = (qm_ref[rows, cols].astype(F32) * SCALE).astype(BF16)
            k = mk_ref[s, :, cols].astype(BF16)
            v = mv_ref[s, :, cols].astype(BF16)
            sc = lax.dot_general(q, k, (((1,), (1,)), ((), ())), preferred_element_type=F32)
            p = jnp.exp(sc - jnp.max(sc, axis=-1, keepdims=True))
            l = jnp.sum(p, axis=-1, keepdims=True)
            o = jnp.dot(p.astype(BF16), v, preferred_element_type=F32) / l
            out_cols = slice(CONV_DIM + h * HEAD_DIM, CONV_DIM + (h + 1) * HEAD_DIM)
            mixed_ref[rows, out_cols] = (o * _silu(zm_ref[rows, cols].astype(F32))).astype(
                mixed_ref.dtype)


def _project_chunk(mixed_ref, w_ref, x_ref, o_ref, kc, tn=512):
    krows = slice(kc * K_CHUNK, (kc + 1) * K_CHUNK)
    a = mixed_ref[:, krows]
    for c in range(o_ref.shape[1] // tn):
        cols = slice(c * tn, (c + 1) * tn)
        base = x_ref[:, cols] if kc == 0 else o_ref[:, cols]
        o_ref[:, cols] = base + jnp.dot(a, w_ref[krows, cols], preferred_element_type=F32)


def _finish_rows(o_ref, g_ref, h_refs, final_norm, rows_per_chunk=64):
    def body(r, carry):
        rows = pl.ds(pl.multiple_of(r * rows_per_chunk, rows_per_chunk), rows_per_chunk)
        x = o_ref[rows, :]
        xn = x * lax.rsqrt(jnp.mean(x * x, axis=-1, keepdims=True) + EPS)
        if final_norm:
            o_ref[rows, :] = xn * g_ref[0:1, :]
        for k, h_ref in enumerate(h_refs):
            h_ref[rows, :] = (xn * g_ref[k:k + 1, :]).astype(h_ref.dtype)
        return carry

    lax.fori_loop(0, o_ref.shape[0] // rows_per_chunk, body, 0)


def _tail_a_kernel(bg_ref, cg_ref, u_ref, zc_ref, cgh_ref, uh_ref, st_ref, qm_ref, zm_ref,
                   mk_ref, mv_ref, cw_ref, w_ref, x_ref, g_ref,
                   o_ref, h1_ref, h2_ref, nst_ref, mixed_ref):
    t = pl.program_id(1)
    n_seq, seq_rows = st_ref.shape[0], bg_ref.shape[0] // st_ref.shape[0]
    first = t == 0
    row = lax.broadcasted_iota(jnp.int32, (seq_rows, LANES), 0)
    hr = cgh_ref.shape[0]
    for c in range(CONV_DIM // LANES):
        cols = slice(c * LANES, (c + 1) * LANES)
        w = cw_ref[:, cols]
        halo = cgh_ref[:, cols].astype(F32) * uh_ref[:, cols].astype(F32)
        for s in range(n_seq):
            rows = slice(s * seq_rows, (s + 1) * seq_rows)
            ci = cg_ref[rows, cols].astype(F32) * u_ref[rows, cols].astype(F32)
            st = st_ref[s, :, cols]
            prev1 = jnp.where(first, st[1:2, :], halo[hr - 1:hr, :])
            prev2 = jnp.where(first, st[0:1, :], halo[hr - 2:hr - 1, :])
            s1 = jnp.where(row == 0, prev1, pltpu.roll(ci, 1, axis=0))
            s2 = jnp.where(row == 0, prev2, jnp.where(row == 1, prev1, pltpu.roll(ci, 2, axis=0)))
            conv = w[0:1, :] * s2 + w[1:2, :] * s1 + w[2:3, :] * ci
            branch = bg_ref[rows, cols].astype(F32) * conv * _silu(zc_ref[rows, cols].astype(F32))
            mixed_ref[rows, cols] = branch.astype(mixed_ref.dtype)
            nst_ref[s, :, cols] = ci[seq_rows - 2:seq_rows, :]

        if (c + 1) * LANES % K_CHUNK == 0:
            _project_chunk(mixed_ref, w_ref, x_ref, o_ref, (c + 1) * LANES // K_CHUNK - 1)

    _memory_attention_into(qm_ref, zm_ref, mk_ref, mv_ref, mixed_ref, n_seq, seq_rows)
    _project_chunk(mixed_ref, w_ref, x_ref, o_ref, MIX_DIM // K_CHUNK - 1)
    _finish_rows(o_ref, g_ref, (h1_ref, h2_ref), False)


def _tail_a(proj, state, mem_k, mem_v, conv_w, w_out, x, gains, *, bsz, t_len, tt):
    m = bsz * t_len
    seq_rows = min(tt, t_len)
    n_seq = tt // seq_rows
    nt = t_len // seq_rows
    hr = 16
    assert t_len % seq_rows == 0 and bsz % n_seq == 0 and seq_rows % hr == 0
    tile = lambda b, t: b * nt + t
    wide = lambda k: pl.BlockSpec((tt, CONV_DIM), lambda b, t: (tile(b, t), k))
    halo = lambda k: pl.BlockSpec(
        (hr, CONV_DIM), lambda b, t: (jnp.maximum(tile(b, t) * (tt // hr) - 1, 0), k))
    narrow = lambda k: pl.BlockSpec((tt, M_DIM), lambda b, t: (tile(b, t), k))
    per_seq = lambda *shape: pl.BlockSpec((n_seq,) + shape, lambda b, t: (b,) + (0,) * len(shape))
    row = pl.BlockSpec((tt, D_MODEL), lambda b, t: (tile(b, t), 0))
    q_col = 4 * CONV_DIM // M_DIM
    return pl.pallas_call(
        _tail_a_kernel,
        grid=(bsz // n_seq, nt),
        in_specs=[wide(0), wide(1), wide(2), wide(3), halo(1), halo(2),
                  per_seq(CONV_W - 1, CONV_DIM), narrow(q_col), narrow(q_col + 1),
                  per_seq(N_MEM, M_DIM), per_seq(N_MEM, M_DIM), _resident(CONV_W, CONV_DIM),
                  _resident(MIX_DIM, D_MODEL), row, _resident(2, D_MODEL)],
        out_specs=[row, row, row, per_seq(CONV_W - 1, CONV_DIM)],
        out_shape=[jax.ShapeDtypeStruct((m, D_MODEL), F32),
                   jax.ShapeDtypeStruct((m, D_MODEL), BF16),
                   jax.ShapeDtypeStruct((m, D_MODEL), BF16),
                   jax.ShapeDtypeStruct((bsz, CONV_W - 1, CONV_DIM), F32)],
        scratch_shapes=[pltpu.VMEM((tt, MIX_DIM), BF16)],
        compiler_params=_params("parallel", "arbitrary"),
        name="tail_a",
    )(proj, proj, proj, proj, proj, proj, state, proj, proj, mem_k, mem_v, conv_w, w_out, x, gains)


def _tail_b_kernel(a_ref, zf_ref, qm_ref, zm_ref, mk_ref, mv_ref, w_ref, x_ref, g_ref,
                   o_ref, mixed_ref):
    n_seq = mk_ref.shape[0]
    seq_rows = a_ref.shape[0] // n_seq
    for c in range(F_DIM // LANES):
        cols = slice(c * LANES, (c + 1) * LANES)
        mixed_ref[:, cols] = (a_ref[:, cols].astype(F32)
                              * _silu(zf_ref[:, cols].astype(F32))).astype(mixed_ref.dtype)
        if (c + 1) * LANES % K_CHUNK == 0:
            _project_chunk(mixed_ref, w_ref, x_ref, o_ref, (c + 1) * LANES // K_CHUNK - 1)
    _memory_attention_into(qm_ref, zm_ref, mk_ref, mv_ref, mixed_ref, n_seq, seq_rows)
    _project_chunk(mixed_ref, w_ref, x_ref, o_ref, MIX_DIM // K_CHUNK - 1)
    _finish_rows(o_ref, g_ref, (), True)


def _tail_b(a, rest, mem_k, mem_v, w_out, x, gain, *, bsz, t_len, tt):
    m = bsz * t_len
    seq_rows = min(tt, t_len)
    n_seq = tt // seq_rows
    nt = t_len // seq_rows
    assert t_len % seq_rows == 0 and bsz % n_seq == 0
    tile = lambda b, t: b * nt + t
    narrow = lambda k: pl.BlockSpec((tt, M_DIM), lambda b, t: (tile(b, t), k))
    mem = pl.BlockSpec((n_seq, N_MEM, M_DIM), lambda b, t: (b, 0, 0))
    wide = pl.BlockSpec((tt, F_DIM), lambda b, t: (tile(b, t), 0))
    row = pl.BlockSpec((tt, D_MODEL), lambda b, t: (tile(b, t), 0))
    q_col = F_DIM // M_DIM
    return pl.pallas_call(
        _tail_b_kernel,
        grid=(bsz // n_seq, nt),
        in_specs=[wide, wide, narrow(q_col), narrow(q_col + 1), mem, mem,
                  _resident(MIX_DIM, D_MODEL), row, _resident(1, D_MODEL)],
        out_specs=row,
        out_shape=jax.ShapeDtypeStruct((m, D_MODEL), F32),
        scratch_shapes=[pltpu.VMEM((tt, MIX_DIM), BF16)],
        compiler_params=_params("parallel", "parallel"),
        name="tail_b",
    )(a, rest, rest, rest, mem_k, mem_v, w_out, x, gain)


def _proj_kernel(h_ref, w_ref, o_ref):
    o_ref[...] = jnp.dot(h_ref[...], w_ref[...], preferred_element_type=F32).astype(o_ref.dtype)


def _proj(h, w, *, tm, tn, out_dtype):
    m, d = h.shape
    n = w.shape[1]
    tm = min(tm, m)
    assert m % tm == 0 and n % tn == 0
    return pl.pallas_call(
        _proj_kernel,
        grid=(m // tm, n // tn),
        in_specs=[pl.BlockSpec((tm, d), lambda i, j: (i, 0)),
                  pl.BlockSpec((d, tn), lambda i, j: (0, j))],
        out_specs=pl.BlockSpec((tm, tn), lambda i, j: (i, j)),
        out_shape=jax.ShapeDtypeStruct((m, n), out_dtype),
        compiler_params=_params("parallel", "arbitrary"),
        name="proj",
    )(h, w)


def _kvf_kernel(h_ref, wkv_ref, wf_ref, bf_ref, c0_ref, *refs,
                seg, tiles_per_seq, tn, emit_vt):
    if emit_vt:
        k_ref, v_ref, logf_ref, c_ref, kx_ref, vt_ref, carry_ref = refs
    else:
        k_ref, v_ref, logf_ref, c_ref, kx_ref, carry_ref = refs
    n_seq, seq_rows = k_ref.shape[0], k_ref.shape[2]
    i = pl.program_id(0)
    tm = h_ref.shape[0]
    h = h_ref[...]

    logit = jnp.dot(h, wf_ref[...], preferred_element_type=F32) + bf_ref[...]
    logf = jnp.minimum(logit, 0.0) - jnp.log1p(jnp.exp(-jnp.abs(logit)))
    logf_ref[...] = logf[:, :N_HEADS_F]
    row = lax.broadcasted_iota(jnp.int32, (tm, tm), 0)
    col = lax.broadcasted_iota(jnp.int32, (tm, tm), 1)
    tri = col <= row
    if seg < tm:
        shift = seg.bit_length() - 1
        tri = tri & ((col >> shift) == (row >> shift))
    tri = jnp.where(tri, 1.0, 0.0).astype(BF16)
    hi, mid, lo = _split3(logf)
    c = (jnp.dot(tri, hi.astype(BF16), preferred_element_type=F32)
         + jnp.dot(tri, mid.astype(BF16), preferred_element_type=F32)
         + jnp.dot(tri, lo.astype(BF16), preferred_element_type=F32))
    if tiles_per_seq > 1:
        @pl.when(i % tiles_per_seq == 0)
        def _():
            carry_ref[...] = c0_ref[0:1, :]

        c = c + carry_ref[...]
        carry_ref[...] = c[tm - 1:tm, :]
    else:
        c = c + c0_ref[...]
    c_ref[...] = c

    heads_per_chunk = tn // HEAD_DIM
    c2 = c * LOG2E
    for n in range(2 * F_DIM // tn):
        y = jnp.dot(h, wkv_ref[:, n * tn:(n + 1) * tn], preferred_element_type=F32)
        is_key = n < F_DIM // tn
        for hh in range(heads_per_chunk):
            head = (n % (F_DIM // tn)) * heads_per_chunk + hh
            yh = y[:, hh * HEAD_DIM:(hh + 1) * HEAD_DIM]
            if is_key:
                yb = yh.astype(BF16)
                aug = _aug_block(c2[:, head:head + 1], True)
            elif emit_vt:
                vt_ref[0, head, 0, 0:HEAD_DIM, :] = yh.T.astype(BF16)
                one_row = lax.broadcasted_iota(jnp.int32, (VT_ROWS - HEAD_DIM, tm), 0) == 0
                vt_ref[0, head, 0, HEAD_DIM:VT_ROWS, :] = jnp.where(one_row, 1.0, 0.0).astype(BF16)
            for s in range(n_seq):
                rows = slice(s * seq_rows, (s + 1) * seq_rows)
                if is_key:
                    k_ref[s, head] = yh[rows]
                    kx_ref[s, head, :, 0:HEAD_DIM] = yb[rows]
                    kx_ref[s, head, :, HEAD_DIM:AUG_DIM] = aug[rows]
                else:
                    v_ref[s, head] = yh[rows]


def _kvf(h, w_kv, w_f, b_f, c0, *, bsz, t_len, tm, emit_vt):
    m, d = h.shape
    tm = min(tm, m)
    seg = min(t_len, tm)
    assert m % tm == 0 and seg & (seg - 1) == 0 and (t_len % tm == 0 or tm % t_len == 0)
    tiles_per_seq = max(t_len // tm, 1)
    n_seq = tm // seg
    row = lambda w: pl.BlockSpec((tm, w), lambda i: (i, 0))
    heads = lambda w: pl.BlockSpec((n_seq, N_HEADS_F, seg, w),
                                   lambda i: (i // tiles_per_seq, 0, i % tiles_per_seq, 0))
    out_specs = [heads(HEAD_DIM), heads(HEAD_DIM), row(N_HEADS_F), row(LANES), heads(AUG_DIM)]
    out_shape = [jax.ShapeDtypeStruct((bsz, N_HEADS_F, t_len, HEAD_DIM), F32),
                 jax.ShapeDtypeStruct((bsz, N_HEADS_F, t_len, HEAD_DIM), F32),
                 jax.ShapeDtypeStruct((m, N_HEADS_F), F32),
                 jax.ShapeDtypeStruct((m, LANES), F32),
                 jax.ShapeDtypeStruct((bsz, N_HEADS_F, t_len, AUG_DIM), BF16)]
    if emit_vt:
        assert tiles_per_seq * tm == t_len
        out_specs.append(pl.BlockSpec(
            (1, N_HEADS_F, 1, VT_ROWS, tm),
            lambda i: (i // tiles_per_seq, 0, i % tiles_per_seq, 0, 0)))
        out_shape.append(
            jax.ShapeDtypeStruct((bsz, N_HEADS_F, tiles_per_seq, VT_ROWS, tm), BF16))
    return pl.pallas_call(
        functools.partial(_kvf_kernel, seg=seg, tiles_per_seq=tiles_per_seq, tn=512,
                          emit_vt=emit_vt),
        grid=(m // tm,),
        in_specs=[row(d), _resident(d, 2 * F_DIM), _resident(d, LANES), _resident(1, LANES),
                  row(LANES)],
        out_specs=out_specs,
        out_shape=out_shape,
        scratch_shapes=[pltpu.VMEM((1, LANES), F32)],
        compiler_params=_params("arbitrary"),
        name="kvf",
    )(h, w_kv, w_f, b_f, c0)


def _qproj_kernel(h_ref, wq_ref, c_ref, qx_ref, *, tn):
    h = h_ref[...]
    c = c_ref[...] * LOG2E
    heads_per_chunk = tn // HEAD_DIM
    for n in range(F_DIM // tn):
        y = jnp.dot(h, wq_ref[:, n * tn:(n + 1) * tn],
                    preferred_element_type=F32) * (SCALE * LOG2E)
        for hh in range(heads_per_chunk):
            head = n * heads_per_chunk + hh
            base = head * AUG_DIM
            qx_ref[:, base:base + HEAD_DIM] = y[:, hh * HEAD_DIM:(hh + 1) * HEAD_DIM].astype(BF16)
            qx_ref[:, base + HEAD_DIM:base + AUG_DIM] = _aug_block(c[:, head:head + 1], False)


def _qproj(h, w_q, c, *, tm):
    m, d = h.shape
    tm = min(tm, m)
    assert m % tm == 0
    return pl.pallas_call(
        functools.partial(_qproj_kernel, tn=512),
        grid=(m // tm,),
        in_specs=[pl.BlockSpec((tm, d), lambda i: (i, 0)),
                  _resident(d, F_DIM),
                  pl.BlockSpec((tm, LANES), lambda i: (i, 0))],
        out_specs=pl.BlockSpec((tm, N_HEADS_F * AUG_DIM), lambda i: (i, 0)),
        out_shape=jax.ShapeDtypeStruct((m, N_HEADS_F * AUG_DIM), BF16),
        compiler_params=_params("parallel"),
        name="qproj",
    )(h, w_q, c)


def _fox_prompt_kernel(qx_ref, kx_ref, vt_ref, o_ref, m_ref, acc_ref, sa_ref, sb_ref, ma_ref, mb_ref,
                       *, cw):
    qi = pl.program_id(2)
    tq = qx_ref.shape[0]
    tk = sa_ref.shape[1]
    tkv = vt_ref.shape[4]
    n_chunks = tq // cw
    buf_a, buf_b = (sa_ref, ma_ref), (sb_ref, mb_ref)
    m_ref[...] = jnp.full_like(m_ref, -jnp.inf)
    acc_ref[...] = jnp.zeros_like(acc_ref)

    def scores(kb, offsets, buf):
        s_ref, smax_ref = buf
        k = kx_ref[0, 0, pl.ds(pl.multiple_of(kb * tk, tk), tk), :]
        for c, key_minus_query in offsets.items():
            q = qx_ref[c * cw:(c + 1) * cw, :]
            s = lax.dot_general(k, q, (((1,), (1,)), ((), ())), preferred_element_type=F32)
            if key_minus_query is not None:
                key = lax.broadcasted_iota(jnp.int32, (tk, cw), 0) + key_minus_query
                query = lax.broadcasted_iota(jnp.int32, (tk, cw), 1)
                s = jnp.where(key <= query, s, -jnp.inf)
            s_ref[c] = s
            smax_ref[c] = jnp.max(s, axis=0, keepdims=True)

    def update(kb, chunks, buf):
        s_ref, smax_ref = buf
        for c in chunks:
            m_old = m_ref[c]
            m_new = jnp.maximum(m_old, smax_ref[c])
            alpha = jnp.exp2(m_old - m_new)
            acc = alpha * acc_ref[c]
            for i in range(tk // tkv):
                p = jnp.exp2(s_ref[c, i * tkv:(i + 1) * tkv, :] - m_new).astype(BF16)
                acc = acc + jnp.dot(vt_ref[0, 0, kb * (tk // tkv) + i], p,
                                    preferred_element_type=F32)
            acc_ref[c] = acc
            m_ref[c] = m_new

    every = {c: None for c in range(n_chunks)}
    own = []
    for jj in range(2):
        offsets = {}
        for c in range(n_chunks):
            k_lo, q_lo = jj * tk, c * cw
            if k_lo < q_lo + cw:
                offsets[c] = k_lo - q_lo if k_lo + tk - 1 > q_lo else None
        own.append(offsets)

    @pl.when(qi == 0)
    def _():
        scores(0, own[0], buf_a)
        scores(1, own[1], buf_b)
        update(0, own[0], buf_a)
        update(1, own[1], buf_b)

    @pl.when(qi > 0)
    def _():
        n = 2 * qi
        scores(0, every, buf_a)

        def pair(t, carry):
            j = 2 * t + 1
            scores(j, every, buf_b)
            update(j - 1, every, buf_a)
            scores(j + 1, every, buf_a)
            update(j, every, buf_b)
            return carry

        lax.fori_loop(0, qi - 1, pair, 0)
        scores(n - 1, every, buf_b)
        update(n - 2, every, buf_a)
        scores(n, own[0], buf_a)
        update(n - 1, every, buf_b)
        scores(n + 1, own[1], buf_b)
        update(n, own[0], buf_a)
        update(n + 1, own[1], buf_b)

    for c in range(n_chunks):
        acc = acc_ref[c]
        o = acc[0:HEAD_DIM, :] / acc[HEAD_DIM:HEAD_DIM + 1, :]
        o_ref[c * cw:(c + 1) * cw, :] = o.T.astype(o_ref.dtype)


def _fox_prompt(qx, kx, vt, *, bsz, t_len, tq, cw):
    m = bsz * t_len
    nq = t_len // tq
    tk = tq // 2
    tkv = vt.shape[4]
    n_chunks = tq // cw
    assert t_len % tq == 0 and tk % tkv == 0 and tq % cw == 0
    return pl.pallas_call(
        functools.partial(_fox_prompt_kernel, cw=cw),
        grid=(bsz, N_HEADS_F, nq),
        in_specs=[pl.BlockSpec((tq, AUG_DIM), lambda b, h, qi: (b * nq + qi, h)),
                  pl.BlockSpec((1, 1, t_len, AUG_DIM), lambda b, h, qi: (b, h, 0, 0)),
                  pl.BlockSpec((1, 1, t_len // tkv, VT_ROWS, tkv),
                               lambda b, h, qi: (b, h, 0, 0, 0))],
        out_specs=pl.BlockSpec((tq, HEAD_DIM), lambda b, h, qi: (b * nq + qi, h)),
        out_shape=jax.ShapeDtypeStruct((m, F_DIM), BF16),
        scratch_shapes=[pltpu.VMEM((n_chunks, 1, cw), F32),
                        pltpu.VMEM((n_chunks, VT_ROWS, cw), F32),
                        pltpu.VMEM((n_chunks, tk, cw), F32), pltpu.VMEM((n_chunks, tk, cw), F32),
                        pltpu.VMEM((n_chunks, 1, cw), F32), pltpu.VMEM((n_chunks, 1, cw), F32)],
        compiler_params=_params("parallel", "parallel", "arbitrary"),
        name="fox_prompt",
    )(qx, kx, vt)


def _fox_cached_kernel(qx_ref, kxn_ref, vn_ref, cn_ref, ck_ref, cv_ref, cp_ref, o_ref,
                       m_ref, l_ref, acc_ref, cq_ref, smax_ref, s_ref):
    b = pl.program_id(0)
    ki = pl.program_id(1)
    t_new = qx_ref.shape[0]

    @pl.when(ki == 0)
    def _():
        m_ref[...] = jnp.full_like(m_ref, -jnp.inf)
        l_ref[...] = jnp.zeros_like(l_ref)
        acc_ref[...] = jnp.zeros_like(acc_ref)
        for h in range(N_HEADS_F):
            cq_ref[h] = jnp.broadcast_to(cn_ref[:, h:h + 1] * LOG2E, (t_new, LANES))

    def update(h, s, s_max, v):
        n = s.shape[1]
        across = (lambda x: _tile_lanes(x, n // LANES)) if n >= LANES else (lambda x: x[:, :n])
        m_old = m_ref[h]
        m_new = jnp.maximum(m_old, s_max)
        alpha = jnp.exp2(m_old - m_new)
        p = jnp.exp2(s - across(m_new))
        l_ref[h] = alpha * l_ref[h] + jnp.sum(p, axis=-1, keepdims=True)
        acc_ref[h] = alpha * acc_ref[h] + jnp.dot(p.astype(BF16), v, preferred_element_type=F32)
        m_ref[h] = m_new

    tk = ck_ref.shape[2]
    for h in range(N_HEADS_F):
        q = qx_ref[:, h * AUG_DIM:h * AUG_DIM + HEAD_DIM]
        k = ck_ref[0, h].astype(BF16)
        s = lax.dot_general(q, k, (((1,), (1,)), ((), ())), preferred_element_type=F32)
        decay = _tile_lanes(cq_ref[h], tk // LANES) - cp_ref[h, pl.ds(b, 1), :] * LOG2E
        s = s + decay
        s_ref[h] = s
        smax_ref[h] = jnp.broadcast_to(jnp.max(s, axis=-1, keepdims=True), (t_new, LANES))
    for h in range(N_HEADS_F):
        update(h, s_ref[h], smax_ref[h], cv_ref[0, h].astype(BF16))

    @pl.when(ki == pl.num_programs(1) - 1)
    def _():
        row = lax.broadcasted_iota(jnp.int32, (t_new, t_new), 0)
        col = lax.broadcasted_iota(jnp.int32, (t_new, t_new), 1)
        for h in range(N_HEADS_F):
            cols = slice(h * HEAD_DIM, (h + 1) * HEAD_DIM)
            xcols = slice(h * AUG_DIM, (h + 1) * AUG_DIM)
            s = lax.dot_general(qx_ref[:, xcols], kxn_ref[0, h], (((1,), (1,)), ((), ())),
                                preferred_element_type=F32)
            s = jnp.where(col <= row, s, -jnp.inf)
            update(h, s, jnp.max(s, axis=-1, keepdims=True), vn_ref[0, h].astype(BF16))
            o_ref[:, cols] = (acc_ref[h] / l_ref[h]).astype(o_ref.dtype)


def _fox_cached(qx, kx_new, v_new, c_new, cache_k, cache_v, c_past, *, bsz, t_len, tk):
    m = bsz * t_len
    past = cache_k.shape[2]
    assert past % tk == 0 and past > 0
    seq = lambda w: pl.BlockSpec((t_len, w), lambda b, ki: (b, 0))
    new = lambda w: pl.BlockSpec((1, N_HEADS_F, t_len, w), lambda b, ki: (b, 0, 0, 0))
    cache = pl.BlockSpec((1, N_HEADS_F, tk, HEAD_DIM), lambda b, ki: (b, 0, ki, 0))
    return pl.pallas_call(
        _fox_cached_kernel,
        grid=(bsz, past // tk),
        in_specs=[seq(N_HEADS_F * AUG_DIM), new(AUG_DIM), new(HEAD_DIM), seq(LANES),
                  cache, cache,
                  pl.BlockSpec((N_HEADS_F, bsz, tk), lambda b, ki: (0, 0, ki))],
        out_specs=seq(F_DIM),
        out_shape=jax.ShapeDtypeStruct((m, F_DIM), BF16),
        scratch_shapes=[pltpu.VMEM((N_HEADS_F, t_len, LANES), F32) for _ in range(5)]
        + [pltpu.VMEM((N_HEADS_F, t_len, tk), F32)],
        compiler_params=_params("parallel", "arbitrary"),
        name="fox_cached",
    )(qx, kx_new, v_new, c_new, cache_k, cache_v, c_past)


def _cumsum_lanes_kernel(x_ref, o_ref):
    rows, n = x_ref.shape
    r = lax.broadcasted_iota(jnp.int32, (LANES, LANES), 0)
    c = lax.broadcasted_iota(jnp.int32, (LANES, LANES), 1)
    upper = jnp.where(r <= c, 1.0, 0.0).astype(BF16)
    carry = jnp.zeros((rows, 1), F32)
    for j in range(n // LANES):
        cols = slice(j * LANES, (j + 1) * LANES)
        hi, mid, lo = _split3(x_ref[:, cols])
        local = (jnp.dot(hi.astype(BF16), upper, preferred_element_type=F32)
                 + jnp.dot(mid.astype(BF16), upper, preferred_element_type=F32)
                 + jnp.dot(lo.astype(BF16), upper, preferred_element_type=F32))
        o_ref[:, cols] = local + carry
        carry = carry + local[:, LANES - 1:LANES]


def _cumsum_lanes(x):
    rows, n = x.shape
    return pl.pallas_call(
        _cumsum_lanes_kernel,
        grid=(1,),
        in_specs=[pl.BlockSpec((rows, n), lambda i: (0, 0))],
        out_specs=pl.BlockSpec((rows, n), lambda i: (0, 0)),
        out_shape=jax.ShapeDtypeStruct((rows, n), F32),
        compiler_params=_params("arbitrary"),
        name="cumsum_lanes",
    )(x)


def _trunk(x, conv_state, mem_k, mem_v, past, w):
    bsz, t_len, d = x.shape
    m = bsz * t_len
    x0 = x.reshape(m, d)
    tail_rows = 512 if t_len >= 512 else 256

    proj = _norm_proj(x0, w["g_norm"][0], w["w_in_a"], tm=1024, tn=1024, out_dtype=BF16)
    x1, h1, h_kv, new_state = _tail_a(
        proj, conv_state, mem_k[0], mem_v[0], w["conv_w"], w["w_out"][0], x0,
        jnp.stack([w["g_norm"][1], w["g_kv"]]), bsz=bsz, t_len=t_len, tt=tail_rows)

    if past is None:
        c0 = jnp.zeros((m, LANES), F32)
    else:
        cache_k, cache_v, cache_logf = past
        past_len = cache_k.shape[1]
        cache_k = jnp.transpose(cache_k, (0, 2, 1, 3))
        cache_v = jnp.transpose(cache_v, (0, 2, 1, 3))
        logf_t = jnp.transpose(cache_logf, (2, 0, 1)).reshape(N_HEADS_F * bsz, past_len)
        c_past = _cumsum_lanes(logf_t).reshape(N_HEADS_F, bsz, past_len)
        c_end = jnp.pad(c_past[:, :, past_len - 1].T, ((0, 0), (0, LANES - N_HEADS_F)))
        c0 = jnp.repeat(c_end, t_len, axis=0)
    kvf_out = _kvf(h_kv, w["w_kv"], w["w_f"], w["b_f"], c0,
                   bsz=bsz, t_len=t_len, tm=512, emit_vt=past is None)
    k_new, v_new, logf, c, kx = kvf_out[:5]
    qx = _qproj(h1, w["w_q"], c, tm=512)
    rest = _proj(h1, w["w_in_b_rest"], tm=1024, tn=512, out_dtype=BF16)

    if past is None:
        o = _fox_prompt(qx, kx, kvf_out[5], bsz=bsz, t_len=t_len, tq=1024, cw=256)
    else:
        o = _fox_cached(qx, kx, v_new, c, cache_k, cache_v, c_past,
                        bsz=bsz, t_len=t_len, tk=512)
    y = _tail_b(o, rest, mem_k[1], mem_v[1], w["w_out"][1], x1, w["g_final"][None],
                bsz=bsz, t_len=t_len, tt=tail_rows)
    return (y.reshape(bsz, t_len, d), new_state[None],
            jnp.transpose(k_new, (0, 2, 1, 3)), jnp.transpose(v_new, (0, 2, 1, 3)),
            logf.reshape(bsz, t_len, N_HEADS_F))


def kernel(x_prompt, x_sample, state_conv, cache_k, cache_v, cache_logf, cache_mem_k, cache_mem_v,
           mem_prompt, g_norm, w_in_a, conv_w, w_in_b, w_out, g_mem, w_mem_kv, g_kv, w_kvf, b_f,
           g_final):
    depth = g_norm.shape[0]
    bp = x_prompt.shape[0]
    bs = x_sample.shape[0]
    w = {
        "g_norm": g_norm, "g_kv": g_kv, "g_final": g_final, "conv_w": conv_w[0],
        "w_in_a": w_in_a[0].astype(BF16),
        "w_q": w_in_b[0][:, :F_DIM].astype(BF16),
        "w_in_b_rest": w_in_b[0][:, F_DIM:].astype(BF16),
        "w_out": w_out.astype(BF16),
        "w_kv": w_kvf[:, :2 * F_DIM].astype(BF16),
        "w_f": jnp.pad(w_kvf[:, 2 * F_DIM:], ((0, 0), (0, LANES - N_HEADS_F))).astype(BF16),
        "b_f": jnp.pad(b_f, (0, LANES - N_HEADS_F)).reshape(1, LANES),
    }

    mem_rows = mem_prompt.reshape(bp * N_MEM, D_MODEL)
    mem_kv = [_norm_proj(mem_rows, g_mem[i], w_mem_kv[i].astype(BF16), tm=512, tn=512)
              for i in range(depth)]
    p_mem_k = jnp.stack([a[:, :M_DIM] for a in mem_kv]).reshape(depth, bp, N_MEM, N_HEADS_M, HEAD_DIM)
    p_mem_v = jnp.stack([a[:, M_DIM:] for a in mem_kv]).reshape(depth, bp, N_MEM, N_HEADS_M, HEAD_DIM)

    zero_conv = jnp.zeros((bp, CONV_W - 1, CONV_DIM), F32)
    y_p, p_state, p_k, p_v, p_logf = _trunk(
        x_prompt, zero_conv, p_mem_k.reshape(depth, bp, N_MEM, M_DIM),
        p_mem_v.reshape(depth, bp, N_MEM, M_DIM), None, w)
    y_s, s_state, s_k, s_v, s_logf = _trunk(
        x_sample, state_conv[0], cache_mem_k.reshape(depth, bs, N_MEM, M_DIM),
        cache_mem_v.reshape(depth, bs, N_MEM, M_DIM), (cache_k, cache_v, cache_logf), w)
    return (y_p, y_s, p_state, p_k, p_v, p_logf, p_mem_k, p_mem_v, s_state, s_k, s_v, s_logf)
```

```python
import functools
import math

import jax
import jax.numpy as jnp
from jax import lax
from jax.experimental import pallas as pl
from jax.experimental.pallas import tpu as pltpu

F32 = jnp.float32
BF16 = jnp.bfloat16

D_MODEL = 2048
CONV_W = 3
CONV_DIM = 1536
HEAD_DIM = 128
N_HEADS_F = 12
F_DIM = N_HEADS_F * HEAD_DIM
N_MEM = 256
N_HEADS_M = 4
M_DIM = N_HEADS_M * HEAD_DIM
MIX_DIM = CONV_DIM + M_DIM
EPS = 1e-6
SCALE = 1.0 / math.sqrt(HEAD_DIM)
LOG2E = math.log2(math.e)

LANES = 128
AUG_DIM = 2 * HEAD_DIM
VT_ROWS = HEAD_DIM + 16
VMEM_LIMIT = 56 * 1024 * 1024


def _params(*sem):
    return pltpu.CompilerParams(dimension_semantics=sem, vmem_limit_bytes=VMEM_LIMIT)


def _resident(*shape):
    return pl.BlockSpec(shape, lambda *_: (0,) * len(shape), pipeline_mode=pl.Buffered(1))


def _tile_lanes(x, n):
    return jnp.concatenate([x] * n, axis=1)


def _silu(z):
    return z * (1.0 / (1.0 + jnp.exp(-z)))


def _split3(c):
    hi = c.astype(BF16).astype(F32)
    r = c - hi
    mid = r.astype(BF16).astype(F32)
    lo = (r - mid).astype(BF16).astype(F32)
    return hi, mid, lo


def _aug_block(c_col, key_side):
    rows = c_col.shape[0]
    hi, mid, lo = _split3(-c_col if key_side else c_col)
    lane = lax.broadcasted_iota(jnp.int32, (rows, LANES), 1)
    term0 = 3 if key_side else 0
    one0 = 0 if key_side else 3
    terms = jnp.where(lane == term0, hi,
                      jnp.where(lane == term0 + 1, mid, jnp.where(lane == term0 + 2, lo, 0.0)))
    return jnp.where((lane >= one0) & (lane < one0 + 3), 1.0, terms).astype(BF16)


def _rmsnorm_rows(x_ref, g_ref, h_ref, rows_per_chunk=64):
    tm = x_ref.shape[0]
    g = g_ref[...]

    def body(r, carry):
        rows = pl.ds(pl.multiple_of(r * rows_per_chunk, rows_per_chunk), rows_per_chunk)
        x = x_ref[rows, :]
        ms = jnp.mean(x * x, axis=-1, keepdims=True)
        h_ref[rows, :] = (x * lax.rsqrt(ms + EPS) * g).astype(h_ref.dtype)
        return carry

    lax.fori_loop(0, tm // rows_per_chunk, body, 0)


def _norm_proj_kernel(x_ref, g_ref, w_ref, o_ref, h_ref):
    @pl.when(pl.program_id(1) == 0)
    def _():
        _rmsnorm_rows(x_ref, g_ref, h_ref)

    o_ref[...] = jnp.dot(h_ref[...], w_ref[...], preferred_element_type=F32).astype(o_ref.dtype)


def _norm_proj(x, g, w, *, tm, tn, out_dtype=F32):
    m, d = x.shape
    n = w.shape[1]
    tm = min(tm, m)
    assert m % tm == 0 and n % tn == 0
    return pl.pallas_call(
        _norm_proj_kernel,
        grid=(m // tm, n // tn),
        in_specs=[
            pl.BlockSpec((tm, d), lambda i, j: (i, 0)),
            pl.BlockSpec((1, d), lambda i, j: (0, 0)),
            pl.BlockSpec((d, tn), lambda i, j: (0, j)),
        ],
        out_specs=pl.BlockSpec((tm, tn), lambda i, j: (i, j)),
        out_shape=jax.ShapeDtypeStruct((m, n), out_dtype),
        scratch_shapes=[pltpu.VMEM((tm, d), BF16)],
        compiler_params=_params("parallel", "arbitrary"),
        name="norm_proj",
    )(x, g.reshape(1, d), w)


K_CHUNK = 256


def _memory_attention_into(qm_ref, zm_ref, mk_ref, mv_ref, mixed_ref, n_seq, seq_rows, heads):
    for s in range(n_seq):
        rows = slice(s * seq_rows, (s + 1) * seq_rows)
        for h in heads:
            cols = slice(h * HEAD_DIM, (h + 1) * HEAD_DIM)
            q = (qm_ref[rows, cols].astype(F32) * SCALE).astype(BF16)
            k = mk_ref[s, :, cols].astype(BF16)
            v = mv_ref[s, :, cols].astype(BF16)
            sc = lax.dot_general(q, k, (((1,), (1,)), ((), ())), preferred_element_type=F32)
            p = jnp.exp(sc - jnp.max(sc, axis=-1, keepdims=True))
            l = jnp.sum(p, axis=-1, keepdims=True)
            o = jnp.dot(p.astype(BF16), v, preferred_element_type=F32) / l
            out_cols = slice(CONV_DIM + h * HEAD_DIM, CONV_DIM + (h + 1) * HEAD_DIM)
            mixed_ref[rows, out_cols] = (o * _silu(zm_ref[rows, cols].astype(F32))).astype(
                mixed_ref.dtype)


def _memory_branch(qm_ref, zm_ref, mk_ref, mv_ref, mixed_ref, w_ref, x_ref, o_ref, n_seq, seq_rows):
    heads_per_chunk = K_CHUNK // HEAD_DIM
    for kc in range(CONV_DIM // K_CHUNK, MIX_DIM // K_CHUNK):
        first = (kc * K_CHUNK - CONV_DIM) // HEAD_DIM
        _memory_attention_into(qm_ref, zm_ref, mk_ref, mv_ref, mixed_ref, n_seq, seq_rows,
                               range(first, first + heads_per_chunk))
        _project_chunk(mixed_ref, w_ref, x_ref, o_ref, kc)


def _project_chunk(mixed_ref, w_ref, x_ref, o_ref, kc, tn=512):
    krows = slice(kc * K_CHUNK, (kc + 1) * K_CHUNK)
    a = mixed_ref[:, krows]
    for c in range(o_ref.shape[1] // tn):
        cols = slice(c * tn, (c + 1) * tn)
        base = x_ref[:, cols] if kc == 0 else o_ref[:, cols]
        o_ref[:, cols] = base + jnp.dot(a, w_ref[krows, cols], preferred_element_type=F32)


def _finish_rows(o_ref, g_ref, h_refs, final_norm, rows_per_chunk=64):
    def body(r, carry):
        rows = pl.ds(pl.multiple_of(r * rows_per_chunk, rows_per_chunk), rows_per_chunk)
        x = o_ref[rows, :]
        xn = x * lax.rsqrt(jnp.mean(x * x, axis=-1, keepdims=True) + EPS)
        if final_norm:
            o_ref[rows, :] = xn * g_ref[0:1, :]
        for k, h_ref in enumerate(h_refs):
            h_ref[rows, :] = (xn * g_ref[k:k + 1, :]).astype(h_ref.dtype)
        return carry

    lax.fori_loop(0, o_ref.shape[0] // rows_per_chunk, body, 0)


def _tail_a_kernel(bg_ref, cg_ref, u_ref, zc_ref, cgh_ref, uh_ref, st_ref, qm_ref, zm_ref,
                   mk_ref, mv_ref, cw_ref, w_ref, x_ref, g_ref,
                   o_ref, h1_ref, h2_ref, nst_ref, mixed_ref):
    t = pl.program_id(1)
    n_seq, seq_rows = st_ref.shape[0], bg_ref.shape[0] // st_ref.shape[0]
    first = t == 0
    row = lax.broadcasted_iota(jnp.int32, (seq_rows, LANES), 0)
    hr = cgh_ref.shape[0]
    for c in range(CONV_DIM // LANES):
        cols = slice(c * LANES, (c + 1) * LANES)
        w = cw_ref[:, cols]
        halo = cgh_ref[:, cols].astype(F32) * uh_ref[:, cols].astype(F32)
        for s in range(n_seq):
            rows = slice(s * seq_rows, (s + 1) * seq_rows)
            ci = cg_ref[rows, cols].astype(F32) * u_ref[rows, cols].astype(F32)
            st = st_ref[s, :, cols]
            prev1 = jnp.where(first, st[1:2, :], halo[hr - 1:hr, :])
            prev2 = jnp.where(first, st[0:1, :], halo[hr - 2:hr - 1, :])
            s1 = jnp.where(row == 0, prev1, pltpu.roll(ci, 1, axis=0))
            s2 = jnp.where(row == 0, prev2, jnp.where(row == 1, prev1, pltpu.roll(ci, 2, axis=0)))
            conv = w[0:1, :] * s2 + w[1:2, :] * s1 + w[2:3, :] * ci
            branch = bg_ref[rows, cols].astype(F32) * conv * _silu(zc_ref[rows, cols].astype(F32))
            mixed_ref[rows, cols] = branch.astype(mixed_ref.dtype)
            nst_ref[s, :, cols] = ci[seq_rows - 2:seq_rows, :]

        if (c + 1) * LANES % K_CHUNK == 0:
            _project_chunk(mixed_ref, w_ref, x_ref, o_ref, (c + 1) * LANES // K_CHUNK - 1)

    _memory_branch(qm_ref, zm_ref, mk_ref, mv_ref, mixed_ref, w_ref, x_ref, o_ref, n_seq, seq_rows)
    _finish_rows(o_ref, g_ref, (h1_ref, h2_ref), False)


def _tail_a(proj, state, mem_k, mem_v, conv_w, w_out, x, gains, *, bsz, t_len, tt):
    m = bsz * t_len
    seq_rows = min(tt, t_len)
    n_seq = tt // seq_rows
    nt = t_len // seq_rows
    hr = 16
    assert t_len % seq_rows == 0 and bsz % n_seq == 0 and seq_rows % hr == 0
    tile = lambda b, t: b * nt + t
    wide = lambda k: pl.BlockSpec((tt, CONV_DIM), lambda b, t: (tile(b, t), k))
    halo = lambda k: pl.BlockSpec(
        (hr, CONV_DIM), lambda b, t: (jnp.maximum(tile(b, t) * (tt // hr) - 1, 0), k))
    narrow = lambda k: pl.BlockSpec((tt, M_DIM), lambda b, t: (tile(b, t), k))
    per_seq = lambda *shape: pl.BlockSpec((n_seq,) + shape, lambda b, t: (b,) + (0,) * len(shape))
    row = pl.BlockSpec((tt, D_MODEL), lambda b, t: (tile(b, t), 0))
    q_col = 4 * CONV_DIM // M_DIM
    return pl.pallas_call(
        _tail_a_kernel,
        grid=(bsz // n_seq, nt),
        in_specs=[wide(0), wide(1), wide(2), wide(3), halo(1), halo(2),
                  per_seq(CONV_W - 1, CONV_DIM), narrow(q_col), narrow(q_col + 1),
                  per_seq(N_MEM, M_DIM), per_seq(N_MEM, M_DIM), _resident(CONV_W, CONV_DIM),
                  _resident(MIX_DIM, D_MODEL), row, _resident(2, D_MODEL)],
        out_specs=[row, row, row, per_seq(CONV_W - 1, CONV_DIM)],
        out_shape=[jax.ShapeDtypeStruct((m, D_MODEL), F32),
                   jax.ShapeDtypeStruct((m, D_MODEL), BF16),
                   jax.ShapeDtypeStruct((m, D_MODEL), BF16),
                   jax.ShapeDtypeStruct((bsz, CONV_W - 1, CONV_DIM), F32)],
        scratch_shapes=[pltpu.VMEM((tt, MIX_DIM), BF16)],
        compiler_params=_params("parallel", "arbitrary"),
        name="tail_a",
    )(proj, proj, proj, proj, proj, proj, state, proj, proj, mem_k, mem_v, conv_w, w_out, x, gains)


def _tail_b_kernel(a_ref, zf_ref, qm_ref, zm_ref, mk_ref, mv_ref, w_ref, x_ref, g_ref,
                   o_ref, mixed_ref):
    n_seq = mk_ref.shape[0]
    seq_rows = a_ref.shape[0] // n_seq
    for c in range(F_DIM // LANES):
        cols = slice(c * LANES, (c + 1) * LANES)
        mixed_ref[:, cols] = (a_ref[:, cols].astype(F32)
                              * _silu(zf_ref[:, cols].astype(F32))).astype(mixed_ref.dtype)
        if (c + 1) * LANES % K_CHUNK == 0:
            _project_chunk(mixed_ref, w_ref, x_ref, o_ref, (c + 1) * LANES // K_CHUNK - 1)
    _memory_branch(qm_ref, zm_ref, mk_ref, mv_ref, mixed_ref, w_ref, x_ref, o_ref, n_seq, seq_rows)
    _finish_rows(o_ref, g_ref, (), True)


def _tail_b(a, rest, mem_k, mem_v, w_out, x, gain, *, bsz, t_len, tt):
    m = bsz * t_len
    seq_rows = min(tt, t_len)
    n_seq = tt // seq_rows
    nt = t_len // seq_rows
    assert t_len % seq_rows == 0 and bsz % n_seq == 0
    tile = lambda b, t: b * nt + t
    narrow = lambda k: pl.BlockSpec((tt, M_DIM), lambda b, t: (tile(b, t), k))
    mem = pl.BlockSpec((n_seq, N_MEM, M_DIM), lambda b, t: (b, 0, 0))
    wide = pl.BlockSpec((tt, F_DIM), lambda b, t: (tile(b, t), 0))
    row = pl.BlockSpec((tt, D_MODEL), lambda b, t: (tile(b, t), 0))
    q_col = F_DIM // M_DIM
    return pl.pallas_call(
        _tail_b_kernel,
        grid=(bsz // n_seq, nt),
        in_specs=[wide, wide, narrow(q_col), narrow(q_col + 1), mem, mem,
                  _resident(MIX_DIM, D_MODEL), row, _resident(1, D_MODEL)],
        out_specs=row,
        out_shape=jax.ShapeDtypeStruct((m, D_MODEL), F32),
        scratch_shapes=[pltpu.VMEM((tt, MIX_DIM), BF16)],
        compiler_params=_params("parallel", "parallel"),
        name="tail_b",
    )(a, rest, rest, rest, mem_k, mem_v, w_out, x, gain)


def _proj_kernel(h_ref, w_ref, o_ref):
    o_ref[...] = jnp.dot(h_ref[...], w_ref[...], preferred_element_type=F32).astype(o_ref.dtype)


def _proj(h, w, *, tm, tn, out_dtype):
    m, d = h.shape
    n = w.shape[1]
    tm = min(tm, m)
    assert m % tm == 0 and n % tn == 0
    return pl.pallas_call(
        _proj_kernel,
        grid=(m // tm, n // tn),
        in_specs=[pl.BlockSpec((tm, d), lambda i, j: (i, 0)),
                  pl.BlockSpec((d, tn), lambda i, j: (0, j))],
        out_specs=pl.BlockSpec((tm, tn), lambda i, j: (i, j)),
        out_shape=jax.ShapeDtypeStruct((m, n), out_dtype),
        compiler_params=_params("parallel", "arbitrary"),
        name="proj",
    )(h, w)


def _kvf_kernel(h_ref, wkv_ref, wf_ref, bf_ref, c0_ref, *refs,
                seg, tiles_per_seq, tn, emit_vt):
    if emit_vt:
        k_ref, v_ref, logf_ref, c_ref, kx_ref, vt_ref, carry_ref = refs
    else:
        k_ref, v_ref, logf_ref, c_ref, kx_ref, carry_ref = refs
    n_seq, seq_rows = k_ref.shape[0], k_ref.shape[2]
    i = pl.program_id(0)
    tm = h_ref.shape[0]
    h = h_ref[...]

    heads_per_chunk = tn // HEAD_DIM
    n_chunks = F_DIM // tn

    def project(n):
        return jnp.dot(h, wkv_ref[:, n * tn:(n + 1) * tn], preferred_element_type=F32)

    logit = jnp.dot(h, wf_ref[...], preferred_element_type=F32) + bf_ref[...]

    for n in range(n_chunks, 2 * n_chunks):
        y = project(n)
        for hh in range(heads_per_chunk):
            head = (n - n_chunks) * heads_per_chunk + hh
            yh = y[:, hh * HEAD_DIM:(hh + 1) * HEAD_DIM]
            if emit_vt:
                vt_ref[0, head, 0, 0:HEAD_DIM, :] = yh.T.astype(BF16)
                one_row = lax.broadcasted_iota(jnp.int32, (VT_ROWS - HEAD_DIM, tm), 0) == 0
                vt_ref[0, head, 0, HEAD_DIM:VT_ROWS, :] = jnp.where(one_row, 1.0, 0.0).astype(BF16)
            for s in range(n_seq):
                v_ref[s, head] = yh[s * seq_rows:(s + 1) * seq_rows]

    logf = jnp.minimum(logit, 0.0) - jnp.log1p(jnp.exp(-jnp.abs(logit)))
    logf_ref[...] = logf[:, :N_HEADS_F]
    pos = lax.broadcasted_iota(jnp.int32, (tm, LANES), 0) & (seg - 1)
    c = logf
    step = 1
    while step < seg:
        c = c + jnp.where(pos >= step, pltpu.roll(c, step, axis=0), 0.0)
        step *= 2
    if tiles_per_seq > 1:
        @pl.when(i % tiles_per_seq == 0)
        def _():
            carry_ref[...] = c0_ref[0:1, :]

        c = c + carry_ref[...]
        carry_ref[...] = c[tm - 1:tm, :]
    else:
        c = c + c0_ref[...]
    c_ref[...] = c

    c2 = c * LOG2E
    for n in range(n_chunks):
        y = project(n)
        for hh in range(heads_per_chunk):
            head = n * heads_per_chunk + hh
            yh = y[:, hh * HEAD_DIM:(hh + 1) * HEAD_DIM]
            yb = yh.astype(BF16)
            aug = _aug_block(c2[:, head:head + 1], True)
            for s in range(n_seq):
                rows = slice(s * seq_rows, (s + 1) * seq_rows)
                k_ref[s, head] = yh[rows]
                kx_ref[s, head, :, 0:HEAD_DIM] = yb[rows]
                kx_ref[s, head, :, HEAD_DIM:AUG_DIM] = aug[rows]


def _kvf(h, w_kv, w_f, b_f, c0, *, bsz, t_len, tm, emit_vt):
    m, d = h.shape
    tm = min(tm, m)
    seg = min(t_len, tm)
    assert m % tm == 0 and seg & (seg - 1) == 0 and (t_len % tm == 0 or tm % t_len == 0)
    tiles_per_seq = max(t_len // tm, 1)
    n_seq = tm // seg
    row = lambda w: pl.BlockSpec((tm, w), lambda i: (i, 0))
    heads = lambda w: pl.BlockSpec((n_seq, N_HEADS_F, seg, w),
                                   lambda i: (i // tiles_per_seq, 0, i % tiles_per_seq, 0))
    out_specs = [heads(HEAD_DIM), heads(HEAD_DIM), row(N_HEADS_F), row(LANES), heads(AUG_DIM)]
    out_shape = [jax.ShapeDtypeStruct((bsz, N_HEADS_F, t_len, HEAD_DIM), F32),
                 jax.ShapeDtypeStruct((bsz, N_HEADS_F, t_len, HEAD_DIM), F32),
                 jax.ShapeDtypeStruct((m, N_HEADS_F), F32),
                 jax.ShapeDtypeStruct((m, LANES), F32),
                 jax.ShapeDtypeStruct((bsz, N_HEADS_F, t_len, AUG_DIM), BF16)]
    if emit_vt:
        assert tiles_per_seq * tm == t_len
        out_specs.append(pl.BlockSpec(
            (1, N_HEADS_F, 1, VT_ROWS, tm),
            lambda i: (i // tiles_per_seq, 0, i % tiles_per_seq, 0, 0)))
        out_shape.append(
            jax.ShapeDtypeStruct((bsz, N_HEADS_F, tiles_per_seq, VT_ROWS, tm), BF16))
    return pl.pallas_call(
        functools.partial(_kvf_kernel, seg=seg, tiles_per_seq=tiles_per_seq, tn=512,
                          emit_vt=emit_vt),
        grid=(m // tm,),
        in_specs=[row(d), _resident(d, 2 * F_DIM), _resident(d, LANES), _resident(1, LANES),
                  row(LANES)],
        out_specs=out_specs,
        out_shape=out_shape,
        scratch_shapes=[pltpu.VMEM((1, LANES), F32)],
        compiler_params=_params("arbitrary"),
        name="kvf",
    )(h, w_kv, w_f, b_f, c0)


def _qproj_kernel(h_ref, wq_ref, c_ref, qx_ref, *, tn):
    h = h_ref[...]
    c = c_ref[...] * LOG2E
    heads_per_chunk = tn // HEAD_DIM
    for n in range(F_DIM // tn):
        y = jnp.dot(h, wq_ref[:, n * tn:(n + 1) * tn],
                    preferred_element_type=F32) * (SCALE * LOG2E)
        for hh in range(heads_per_chunk):
            head = n * heads_per_chunk + hh
            base = head * AUG_DIM
            qx_ref[:, base:base + HEAD_DIM] = y[:, hh * HEAD_DIM:(hh + 1) * HEAD_DIM].astype(BF16)
            qx_ref[:, base + HEAD_DIM:base + AUG_DIM] = _aug_block(c[:, head:head + 1], False)


def _qproj(h, w_q, c, *, tm):
    m, d = h.shape
    tm = min(tm, m)
    assert m % tm == 0
    return pl.pallas_call(
        functools.partial(_qproj_kernel, tn=512),
        grid=(m // tm,),
        in_specs=[pl.BlockSpec((tm, d), lambda i: (i, 0)),
                  _resident(d, F_DIM),
                  pl.BlockSpec((tm, LANES), lambda i: (i, 0))],
        out_specs=pl.BlockSpec((tm, N_HEADS_F * AUG_DIM), lambda i: (i, 0)),
        out_shape=jax.ShapeDtypeStruct((m, N_HEADS_F * AUG_DIM), BF16),
        compiler_params=_params("parallel"),
        name="qproj",
    )(h, w_q, c)


def _fox_prompt_kernel(qx_ref, kx_ref, vt_ref, o_ref, m_ref, acc_ref, sa_ref, sb_ref, ma_ref, mb_ref,
                       *, cw):
    qi = pl.program_id(2)
    tq = qx_ref.shape[0]
    tk = sa_ref.shape[1]
    tkv = vt_ref.shape[4]
    n_chunks = tq // cw
    buf_a, buf_b = (sa_ref, ma_ref), (sb_ref, mb_ref)
    m_ref[...] = jnp.full_like(m_ref, -jnp.inf)
    acc_ref[...] = jnp.zeros_like(acc_ref)

    def scores(kb, visible, buf):
        s_ref, smax_ref = buf
        for c, (keys, key_minus_query) in visible.items():
            k = kx_ref[0, 0, pl.ds(pl.multiple_of(kb * tk, tk), keys), :]
            q = qx_ref[c * cw:(c + 1) * cw, :]
            s = lax.dot_general(k, q, (((1,), (1,)), ((), ())), preferred_element_type=F32)
            if key_minus_query is not None:
                key = lax.broadcasted_iota(jnp.int32, (keys, cw), 0) + key_minus_query
                query = lax.broadcasted_iota(jnp.int32, (keys, cw), 1)
                s = jnp.where(key <= query, s, -jnp.inf)
            s_ref[c, 0:keys, :] = s
            smax_ref[c] = jnp.max(s, axis=0, keepdims=True)

    def update(kb, visible, buf):
        s_ref, smax_ref = buf
        for c, (keys, _) in visible.items():
            m_old = m_ref[c]
            m_new = jnp.maximum(m_old, smax_ref[c])
            alpha = jnp.exp2(m_old - m_new)
            acc = alpha * acc_ref[c]
            for lo in range(0, keys, tkv):
                n = min(tkv, keys - lo)
                p = jnp.exp2(s_ref[c, lo:lo + n, :] - m_new).astype(BF16)
                acc = acc + jnp.dot(vt_ref[0, 0, kb * (tk // tkv) + lo // tkv, :, 0:n], p,
                                    preferred_element_type=F32)
            acc_ref[c] = acc
            m_ref[c] = m_new

    every = {c: (tk, None) for c in range(n_chunks)}
    own = []
    for jj in range(2):
        visible = {}
        for c in range(n_chunks):
            k_lo, q_lo = jj * tk, c * cw
            keys = min(tk, q_lo + cw - k_lo)
            if keys > 0:
                visible[c] = (keys, k_lo - q_lo if k_lo + keys - 1 > q_lo else None)
        own.append(visible)

    @pl.when(qi == 0)
    def _():
        scores(0, own[0], buf_a)
        scores(1, own[1], buf_b)
        update(0, own[0], buf_a)
        update(1, own[1], buf_b)

    @pl.when(qi > 0)
    def _():
        n = 2 * qi
        scores(0, every, buf_a)

        def pair(j):
            scores(j, every, buf_b)
            update(j - 1, every, buf_a)
            scores(j + 1, every, buf_a)
            update(j, every, buf_b)

        def two_pairs(t, carry):
            pair(4 * t + 1)
            pair(4 * t + 3)
            return carry

        n_pairs = qi - 1
        lax.fori_loop(0, n_pairs // 2, two_pairs, 0)

        @pl.when(n_pairs % 2 == 1)
        def _():
            pair(n - 3)

        scores(n - 1, every, buf_b)
        update(n - 2, every, buf_a)
        scores(n, own[0], buf_a)
        update(n - 1, every, buf_b)
        scores(n + 1, own[1], buf_b)
        update(n, own[0], buf_a)
        update(n + 1, own[1], buf_b)

    for c in range(n_chunks):
        acc = acc_ref[c]
        o = acc[0:HEAD_DIM, :] / acc[HEAD_DIM:HEAD_DIM + 1, :]
        o_ref[c * cw:(c + 1) * cw, :] = o.T.astype(o_ref.dtype)


def _fox_prompt(qx, kx, vt, *, bsz, t_len, tq, cw):
    m = bsz * t_len
    nq = t_len // tq
    tk = tq // 2
    tkv = vt.shape[4]
    n_chunks = tq // cw
    assert t_len % tq == 0 and tk % tkv == 0 and tq % cw == 0
    return pl.pallas_call(
        functools.partial(_fox_prompt_kernel, cw=cw),
        grid=(bsz, N_HEADS_F, nq),
        in_specs=[pl.BlockSpec((tq, AUG_DIM), lambda b, h, qi: (b * nq + qi, h)),
                  pl.BlockSpec((1, 1, t_len, AUG_DIM), lambda b, h, qi: (b, h, 0, 0)),
                  pl.BlockSpec((1, 1, t_len // tkv, VT_ROWS, tkv),
                               lambda b, h, qi: (b, h, 0, 0, 0))],
        out_specs=pl.BlockSpec((tq, HEAD_DIM), lambda b, h, qi: (b * nq + qi, h)),
        out_shape=jax.ShapeDtypeStruct((m, F_DIM), BF16),
        scratch_shapes=[pltpu.VMEM((n_chunks, 1, cw), F32),
                        pltpu.VMEM((n_chunks, VT_ROWS, cw), F32),
                        pltpu.VMEM((n_chunks, tk, cw), F32), pltpu.VMEM((n_chunks, tk, cw), F32),
                        pltpu.VMEM((n_chunks, 1, cw), F32), pltpu.VMEM((n_chunks, 1, cw), F32)],
        compiler_params=_params("parallel", "parallel", "arbitrary"),
        name="fox_prompt",
    )(qx, kx, vt)


def _fox_cached_kernel(qx_ref, kxn_ref, vn_ref, cn_ref, ck_ref, cv_ref, cp_ref, o_ref,
                       m_ref, l_ref, acc_ref, cq_ref, smax_ref, s_ref):
    b = pl.program_id(0)
    ki = pl.program_id(1)
    t_new = qx_ref.shape[0]

    @pl.when(ki == 0)
    def _():
        m_ref[...] = jnp.full_like(m_ref, -jnp.inf)
        l_ref[...] = jnp.zeros_like(l_ref)
        acc_ref[...] = jnp.zeros_like(acc_ref)
        for h in range(N_HEADS_F):
            cq_ref[h] = jnp.broadcast_to(cn_ref[:, h:h + 1] * LOG2E, (t_new, LANES))

    def update(h, s, s_max, v):
        n = s.shape[1]
        across = (lambda x: _tile_lanes(x, n // LANES)) if n >= LANES else (lambda x: x[:, :n])
        m_old = m_ref[h]
        m_new = jnp.maximum(m_old, s_max)
        alpha = jnp.exp2(m_old - m_new)
        p = jnp.exp2(s - across(m_new))
        l_ref[h] = alpha * l_ref[h] + jnp.sum(p, axis=-1, keepdims=True)
        acc_ref[h] = alpha * acc_ref[h] + jnp.dot(p.astype(BF16), v, preferred_element_type=F32)
        m_ref[h] = m_new

    tk = ck_ref.shape[2]
    for h in range(N_HEADS_F):
        q = qx_ref[:, h * AUG_DIM:h * AUG_DIM + HEAD_DIM]
        k = ck_ref[0, h].astype(BF16)
        s = lax.dot_general(q, k, (((1,), (1,)), ((), ())), preferred_element_type=F32)
        decay = _tile_lanes(cq_ref[h], tk // LANES) - cp_ref[h, pl.ds(b, 1), :] * LOG2E
        s = s + decay
        s_ref[h] = s
        smax_ref[h] = jnp.broadcast_to(jnp.max(s, axis=-1, keepdims=True), (t_new, LANES))
    for h in range(N_HEADS_F):
        update(h, s_ref[h], smax_ref[h], cv_ref[0, h].astype(BF16))

    @pl.when(ki == pl.num_programs(1) - 1)
    def _():
        row = lax.broadcasted_iota(jnp.int32, (t_new, t_new), 0)
        col = lax.broadcasted_iota(jnp.int32, (t_new, t_new), 1)
        for h in range(N_HEADS_F):
            cols = slice(h * HEAD_DIM, (h + 1) * HEAD_DIM)
            xcols = slice(h * AUG_DIM, (h + 1) * AUG_DIM)
            s = lax.dot_general(qx_ref[:, xcols], kxn_ref[0, h], (((1,), (1,)), ((), ())),
                                preferred_element_type=F32)
            s = jnp.where(col <= row, s, -jnp.inf)
            update(h, s, jnp.max(s, axis=-1, keepdims=True), vn_ref[0, h].astype(BF16))
            o_ref[:, cols] = (acc_ref[h] / l_ref[h]).astype(o_ref.dtype)


def _fox_cached(qx, kx_new, v_new, c_new, cache_k, cache_v, c_past, *, bsz, t_len, tk):
    m = bsz * t_len
    past = cache_k.shape[2]
    assert past % tk == 0 and past > 0
    seq = lambda w: pl.BlockSpec((t_len, w), lambda b, ki: (b, 0))
    new = lambda w: pl.BlockSpec((1, N_HEADS_F, t_len, w), lambda b, ki: (b, 0, 0, 0))
    cache = pl.BlockSpec((1, N_HEADS_F, tk, HEAD_DIM), lambda b, ki: (b, 0, ki, 0))
    return pl.pallas_call(
        _fox_cached_kernel,
        grid=(bsz, past // tk),
        in_specs=[seq(N_HEADS_F * AUG_DIM), new(AUG_DIM), new(HEAD_DIM), seq(LANES),
                  cache, cache,
                  pl.BlockSpec((N_HEADS_F, bsz, tk), lambda b, ki: (0, 0, ki))],
        out_specs=seq(F_DIM),
        out_shape=jax.ShapeDtypeStruct((m, F_DIM), BF16),
        scratch_shapes=[pltpu.VMEM((N_HEADS_F, t_len, LANES), F32) for _ in range(5)]
        + [pltpu.VMEM((N_HEADS_F, t_len, tk), F32)],
        compiler_params=_params("parallel", "arbitrary"),
        name="fox_cached",
    )(qx, kx_new, v_new, c_new, cache_k, cache_v, c_past)


def _cumsum_lanes_kernel(x_ref, o_ref):
    rows, n = x_ref.shape
    r = lax.broadcasted_iota(jnp.int32, (LANES, LANES), 0)
    c = lax.broadcasted_iota(jnp.int32, (LANES, LANES), 1)
    upper = jnp.where(r <= c, 1.0, 0.0).astype(BF16)
    carry = jnp.zeros((rows, 1), F32)
    for j in range(n // LANES):
        cols = slice(j * LANES, (j + 1) * LANES)
        hi, mid, lo = _split3(x_ref[:, cols])
        local = (jnp.dot(hi.astype(BF16), upper, preferred_element_type=F32)
                 + jnp.dot(mid.astype(BF16), upper, preferred_element_type=F32)
                 + jnp.dot(lo.astype(BF16), upper, preferred_element_type=F32))
        o_ref[:, cols] = local + carry
        carry = carry + local[:, LANES - 1:LANES]


def _cumsum_lanes(x):
    rows, n = x.shape
    return pl.pallas_call(
        _cumsum_lanes_kernel,
        grid=(1,),
        in_specs=[pl.BlockSpec((rows, n), lambda i: (0, 0))],
        out_specs=pl.BlockSpec((rows, n), lambda i: (0, 0)),
        out_shape=jax.ShapeDtypeStruct((rows, n), F32),
        compiler_params=_params("arbitrary"),
        name="cumsum_lanes",
    )(x)


def _trunk(x, conv_state, mem_k, mem_v, past, w):
    bsz, t_len, d = x.shape
    m = bsz * t_len
    x0 = x.reshape(m, d)
    tail_rows = 512 if t_len >= 512 else 256

    proj = _norm_proj(x0, w["g_norm"][0], w["w_in_a"], tm=1024, tn=1024, out_dtype=BF16)
    x1, h1, h_kv, new_state = _tail_a(
        proj, conv_state, mem_k[0], mem_v[0], w["conv_w"], w["w_out"][0], x0,
        jnp.stack([w["g_norm"][1], w["g_kv"]]), bsz=bsz, t_len=t_len, tt=tail_rows)

    if past is None:
        c0 = jnp.zeros((m, LANES), F32)
    else:
        cache_k, cache_v, cache_logf = past
        past_len = cache_k.shape[1]
        cache_k = jnp.transpose(cache_k, (0, 2, 1, 3))
        cache_v = jnp.transpose(cache_v, (0, 2, 1, 3))
        logf_t = jnp.transpose(cache_logf, (2, 0, 1)).reshape(N_HEADS_F * bsz, past_len)
        c_past = _cumsum_lanes(logf_t).reshape(N_HEADS_F, bsz, past_len)
        c_end = jnp.pad(c_past[:, :, past_len - 1].T, ((0, 0), (0, LANES - N_HEADS_F)))
        c0 = jnp.repeat(c_end, t_len, axis=0)
    kvf_out = _kvf(h_kv, w["w_kv"], w["w_f"], w["b_f"], c0,
                   bsz=bsz, t_len=t_len, tm=512, emit_vt=past is None)
    k_new, v_new, logf, c, kx = kvf_out[:5]
    qx = _qproj(h1, w["w_q"], c, tm=512)
    rest = _proj(h1, w["w_in_b_rest"], tm=1024, tn=512, out_dtype=BF16)

    if past is None:
        o = _fox_prompt(qx, kx, kvf_out[5], bsz=bsz, t_len=t_len, tq=1024, cw=256)
    else:
        o = _fox_cached(qx, kx, v_new, c, cache_k, cache_v, c_past,
                        bsz=bsz, t_len=t_len, tk=1024)
    y = _tail_b(o, rest, mem_k[1], mem_v[1], w["w_out"][1], x1, w["g_final"][None],
                bsz=bsz, t_len=t_len, tt=tail_rows)
    return (y.reshape(bsz, t_len, d), new_state[None],
            jnp.transpose(k_new, (0, 2, 1, 3)), jnp.transpose(v_new, (0, 2, 1, 3)),
            logf.reshape(bsz, t_len, N_HEADS_F))


def kernel(x_prompt, x_sample, state_conv, cache_k, cache_v, cache_logf, cache_mem_k, cache_mem_v,
           mem_prompt, g_norm, w_in_a, conv_w, w_in_b, w_out, g_mem, w_mem_kv, g_kv, w_kvf, b_f,
           g_final):
    depth = g_norm.shape[0]
    bp = x_prompt.shape[0]
    bs = x_sample.shape[0]
    w = {
        "g_norm": g_norm, "g_kv": g_kv, "g_final": g_final, "conv_w": conv_w[0],
        "w_in_a": w_in_a[0].astype(BF16),
        "w_q": w_in_b[0][:, :F_DIM].astype(BF16),
        "w_in_b_rest": w_in_b[0][:, F_DIM:].astype(BF16),
        "w_out": w_out.astype(BF16),
        "w_kv": w_kvf[:, :2 * F_DIM].astype(BF16),
        "w_f": jnp.pad(w_kvf[:, 2 * F_DIM:], ((0, 0), (0, LANES - N_HEADS_F))).astype(BF16),
        "b_f": jnp.pad(b_f, (0, LANES - N_HEADS_F)).reshape(1, LANES),
    }

    mem_rows = mem_prompt.reshape(bp * N_MEM, D_MODEL)
    mem_kv = [_norm_proj(mem_rows, g_mem[i], w_mem_kv[i].astype(BF16), tm=512, tn=512)
              for i in range(depth)]
    p_mem_k = jnp.stack([a[:, :M_DIM] for a in mem_kv]).reshape(depth, bp, N_MEM, N_HEADS_M, HEAD_DIM)
    p_mem_v = jnp.stack([a[:, M_DIM:] for a in mem_kv]).reshape(depth, bp, N_MEM, N_HEADS_M, HEAD_DIM)

    zero_conv = jnp.zeros((bp, CONV_W - 1, CONV_DIM), F32)
    y_p, p_state, p_k, p_v, p_logf = _trunk(
        x_prompt, zero_conv, p_mem_k.reshape(depth, bp, N_MEM, M_DIM),
        p_mem_v.reshape(depth, bp, N_MEM, M_DIM), None, w)
    y_s, s_state, s_k, s_v, s_logf = _trunk(
        x_sample, state_conv[0], cache_mem_k.reshape(depth, bs, N_MEM, M_DIM),
        cache_mem_v.reshape(depth, bs, N_MEM, M_DIM), (cache_k, cache_v, cache_logf), w)
    return (y_p, y_s, p_state, p_k, p_v, p_logf, p_mem_k, p_mem_v, s_state, s_k, s_v, s_logf)
```

```python
import functools
import math

import jax
import jax.numpy as jnp
from jax import lax
from jax.experimental import pallas as pl
from jax.experimental.pallas import tpu as pltpu

F32 = jnp.float32
BF16 = jnp.bfloat16

D_MODEL = 2048
CONV_W = 3
CONV_DIM = 1536
HEAD_DIM = 128
N_HEADS_F = 12
F_DIM = N_HEADS_F * HEAD_DIM
N_MEM = 256
N_HEADS_M = 4
M_DIM = N_HEADS_M * HEAD_DIM
MIX_DIM = CONV_DIM + M_DIM
EPS = 1e-6
SCALE = 1.0 / math.sqrt(HEAD_DIM)
LOG2E = math.log2(math.e)

LANES = 128
AUG_DIM = 2 * HEAD_DIM
VT_ROWS = HEAD_DIM + 16
VMEM_LIMIT = 56 * 1024 * 1024
NORM_ROWS = 64


def _params(*sem):
    return pltpu.CompilerParams(dimension_semantics=sem, vmem_limit_bytes=VMEM_LIMIT)


def _resident(*shape):
    return pl.BlockSpec(shape, lambda *_: (0,) * len(shape), pipeline_mode=pl.Buffered(1))


def _tile_lanes(x, n):
    return jnp.concatenate([x] * n, axis=1)


def _silu(z):
    return z * (1.0 / (1.0 + jnp.exp(-z)))


def _split3(c):
    hi = c.astype(BF16).astype(F32)
    r = c - hi
    mid = r.astype(BF16).astype(F32)
    lo = (r - mid).astype(BF16).astype(F32)
    return hi, mid, lo


def _aug_block(c_col, key_side):
    rows = c_col.shape[0]
    hi, mid, lo = _split3(-c_col if key_side else c_col)
    lane = lax.broadcasted_iota(jnp.int32, (rows, LANES), 1)
    term0 = 3 if key_side else 0
    one0 = 0 if key_side else 3
    terms = jnp.where(lane == term0, hi,
                      jnp.where(lane == term0 + 1, mid, jnp.where(lane == term0 + 2, lo, 0.0)))
    return jnp.where((lane >= one0) & (lane < one0 + 3), 1.0, terms).astype(BF16)


def _rmsnorm_rows(x_ref, g_ref, h_ref, rows_per_chunk=NORM_ROWS):
    tm = x_ref.shape[0]
    g = g_ref[...]

    def body(r, carry):
        rows = pl.ds(pl.multiple_of(r * rows_per_chunk, rows_per_chunk), rows_per_chunk)
        x = x_ref[rows, :]
        ms = jnp.mean(x * x, axis=-1, keepdims=True)
        h_ref[rows, :] = (x * lax.rsqrt(ms + EPS) * g).astype(h_ref.dtype)
        return carry

    lax.fori_loop(0, tm // rows_per_chunk, body, 0)


def _norm_proj_kernel(x_ref, g_ref, w_ref, o_ref, h_ref):
    @pl.when(pl.program_id(1) == 0)
    def _():
        _rmsnorm_rows(x_ref, g_ref, h_ref)

    o_ref[...] = jnp.dot(h_ref[...], w_ref[...].astype(BF16),
                         preferred_element_type=F32).astype(o_ref.dtype)


def _norm_proj(x, g, w, *, tm, tn, out_dtype=F32):
    m, d = x.shape
    n = w.shape[1]
    tm = min(tm, m)
    assert m % tm == 0 and n % tn == 0
    return pl.pallas_call(
        _norm_proj_kernel,
        grid=(m // tm, n // tn),
        in_specs=[
            pl.BlockSpec((tm, d), lambda i, j: (i, 0)),
            pl.BlockSpec((1, d), lambda i, j: (0, 0)),
            pl.BlockSpec((d, tn), lambda i, j: (0, j)),
        ],
        out_specs=pl.BlockSpec((tm, tn), lambda i, j: (i, j)),
        out_shape=jax.ShapeDtypeStruct((m, n), out_dtype),
        scratch_shapes=[pltpu.VMEM((tm, d), BF16)],
        compiler_params=_params("parallel", "arbitrary"),
        name="norm_proj",
    )(x, g.reshape(1, d), w)


K_CHUNK = 512


def _memory_attention_into(qm_ref, zm_ref, mk_ref, mv_ref, mixed_ref, n_seq, seq_rows, heads):
    for s in range(n_seq):
        rows = slice(s * seq_rows, (s + 1) * seq_rows)
        for h in heads:
            cols = slice(h * HEAD_DIM, (h + 1) * HEAD_DIM)
            q = (qm_ref[rows, cols].astype(F32) * SCALE).astype(BF16)
            k = mk_ref[s, :, cols].astype(BF16)
            v = mv_ref[s, :, cols].astype(BF16)
            sc = lax.dot_general(q, k, (((1,), (1,)), ((), ())), preferred_element_type=F32)
            p = jnp.exp(sc - jnp.max(sc, axis=-1, keepdims=True))
            l = jnp.sum(p, axis=-1, keepdims=True)
            o = jnp.dot(p.astype(BF16), v, preferred_element_type=F32) / l
            out_cols = slice(CONV_DIM + h * HEAD_DIM, CONV_DIM + (h + 1) * HEAD_DIM)
            mixed_ref[rows, out_cols] = (o * _silu(zm_ref[rows, cols].astype(F32))).astype(
                mixed_ref.dtype)


def _memory_branch(qm_ref, zm_ref, mk_ref, mv_ref, mixed_ref, w_ref, x_ref, o_ref, n_seq, seq_rows):
    heads_per_chunk = K_CHUNK // HEAD_DIM
    for kc in range(CONV_DIM // K_CHUNK, MIX_DIM // K_CHUNK):
        first = (kc * K_CHUNK - CONV_DIM) // HEAD_DIM
        _memory_attention_into(qm_ref, zm_ref, mk_ref, mv_ref, mixed_ref, n_seq, seq_rows,
                               range(first, first + heads_per_chunk))
        _project_chunk(mixed_ref, w_ref, x_ref, o_ref, kc)


def _project_chunk(mixed_ref, w_ref, x_ref, o_ref, kc, tn=512):
    krows = slice(kc * K_CHUNK, (kc + 1) * K_CHUNK)
    a = mixed_ref[:, krows]
    for c in range(o_ref.shape[1] // tn):
        cols = slice(c * tn, (c + 1) * tn)
        base = x_ref[:, cols] if kc == 0 else o_ref[:, cols]
        o_ref[:, cols] = base + jnp.dot(a, w_ref[krows, cols], preferred_element_type=F32)


def _finish_rows(o_ref, g_ref, h_refs, inv_ref, final_norm, rows_per_chunk=NORM_ROWS):
    n_chunks = o_ref.shape[0] // rows_per_chunk
    chunk = lambda r: pl.ds(pl.multiple_of(r * rows_per_chunk, rows_per_chunk), rows_per_chunk)

    def stats(r, carry):
        x = o_ref[chunk(r), :]
        inv = lax.rsqrt(jnp.mean(x * x, axis=-1, keepdims=True) + EPS)
        inv_ref[chunk(r), :] = jnp.broadcast_to(inv, (rows_per_chunk, LANES))
        return carry

    def scale(r, carry):
        xn = o_ref[chunk(r), :] * _tile_lanes(inv_ref[chunk(r), :], o_ref.shape[1] // LANES)
        if final_norm:
            o_ref[chunk(r), :] = xn * g_ref[0:1, :]
        for k, h_ref in enumerate(h_refs):
            h_ref[chunk(r), :] = (xn * g_ref[k:k + 1, :]).astype(h_ref.dtype)
        return carry

    lax.fori_loop(0, n_chunks, stats, 0, unroll=True)
    lax.fori_loop(0, n_chunks, scale, 0, unroll=True)


def _tail_a_kernel(bg_ref, cg_ref, u_ref, zc_ref, cgh_ref, uh_ref, st_ref, qm_ref, zm_ref,
                   mk_ref, mv_ref, cw_ref, w_ref, x_ref, g_ref,
                   o_ref, h1_ref, h2_ref, nst_ref, mixed_ref, inv_ref):
    t = pl.program_id(1)
    n_seq, seq_rows = st_ref.shape[0], bg_ref.shape[0] // st_ref.shape[0]
    first = t == 0
    row = lax.broadcasted_iota(jnp.int32, (seq_rows, LANES), 0)
    hr = cgh_ref.shape[0]
    for c in range(CONV_DIM // LANES):
        cols = slice(c * LANES, (c + 1) * LANES)
        w = cw_ref[:, cols]
        halo = cgh_ref[:, cols].astype(F32) * uh_ref[:, cols].astype(F32)
        for s in range(n_seq):
            rows = slice(s * seq_rows, (s + 1) * seq_rows)
            ci = cg_ref[rows, cols].astype(F32) * u_ref[rows, cols].astype(F32)
            st = st_ref[s, :, cols]
            prev1 = jnp.where(first, st[1:2, :], halo[hr - 1:hr, :])
            prev2 = jnp.where(first, st[0:1, :], halo[hr - 2:hr - 1, :])
            s1 = jnp.where(row == 0, prev1, pltpu.roll(ci, 1, axis=0))
            s2 = jnp.where(row == 0, prev2, jnp.where(row == 1, prev1, pltpu.roll(ci, 2, axis=0)))
            conv = w[0:1, :] * s2 + w[1:2, :] * s1 + w[2:3, :] * ci
            branch = bg_ref[rows, cols].astype(F32) * conv * _silu(zc_ref[rows, cols].astype(F32))
            mixed_ref[rows, cols] = branch.astype(mixed_ref.dtype)
            nst_ref[s, :, cols] = ci[seq_rows - 2:seq_rows, :]

        if (c + 1) * LANES % K_CHUNK == 0:
            _project_chunk(mixed_ref, w_ref, x_ref, o_ref, (c + 1) * LANES // K_CHUNK - 1)

    _memory_branch(qm_ref, zm_ref, mk_ref, mv_ref, mixed_ref, w_ref, x_ref, o_ref, n_seq, seq_rows)
    _finish_rows(o_ref, g_ref, (h1_ref, h2_ref), inv_ref, False)


def _tail_a(proj, state, mem_k, mem_v, conv_w, w_out, x, gains, *, bsz, t_len, tt):
    m = bsz * t_len
    seq_rows = min(tt, t_len)
    n_seq = tt // seq_rows
    nt = t_len // seq_rows
    hr = 16
    assert t_len % seq_rows == 0 and bsz % n_seq == 0 and seq_rows % hr == 0
    tile = lambda b, t: b * nt + t
    wide = lambda k: pl.BlockSpec((tt, CONV_DIM), lambda b, t: (tile(b, t), k))
    halo = lambda k: pl.BlockSpec(
        (hr, CONV_DIM), lambda b, t: (jnp.maximum(tile(b, t) * (tt // hr) - 1, 0), k))
    narrow = lambda k: pl.BlockSpec((tt, M_DIM), lambda b, t: (tile(b, t), k))
    per_seq = lambda *shape: pl.BlockSpec((n_seq,) + shape, lambda b, t: (b,) + (0,) * len(shape))
    row = pl.BlockSpec((tt, D_MODEL), lambda b, t: (tile(b, t), 0))
    q_col = 4 * CONV_DIM // M_DIM
    return pl.pallas_call(
        _tail_a_kernel,
        grid=(bsz // n_seq, nt),
        in_specs=[wide(0), wide(1), wide(2), wide(3), halo(1), halo(2),
                  per_seq(CONV_W - 1, CONV_DIM), narrow(q_col), narrow(q_col + 1),
                  per_seq(N_MEM, M_DIM), per_seq(N_MEM, M_DIM), _resident(CONV_W, CONV_DIM),
                  _resident(MIX_DIM, D_MODEL), row, _resident(2, D_MODEL)],
        out_specs=[row, row, row, per_seq(CONV_W - 1, CONV_DIM)],
        out_shape=[jax.ShapeDtypeStruct((m, D_MODEL), F32),
                   jax.ShapeDtypeStruct((m, D_MODEL), BF16),
                   jax.ShapeDtypeStruct((m, D_MODEL), BF16),
                   jax.ShapeDtypeStruct((bsz, CONV_W - 1, CONV_DIM), F32)],
        scratch_shapes=[pltpu.VMEM((tt, MIX_DIM), BF16), pltpu.VMEM((tt, LANES), F32)],
        compiler_params=_params("parallel", "arbitrary"),
        name="tail_a",
    )(proj, proj, proj, proj, proj, proj, state, proj, proj, mem_k, mem_v, conv_w, w_out, x, gains)


def _tail_b_kernel(a_ref, zf_ref, qm_ref, zm_ref, mk_ref, mv_ref, w_ref, x_ref, g_ref,
                   o_ref, mixed_ref, inv_ref):
    n_seq = mk_ref.shape[0]
    seq_rows = a_ref.shape[0] // n_seq
    for c in range(F_DIM // LANES):
        cols = slice(c * LANES, (c + 1) * LANES)
        mixed_ref[:, cols] = (a_ref[:, cols].astype(F32)
                              * _silu(zf_ref[:, cols].astype(F32))).astype(mixed_ref.dtype)
        if (c + 1) * LANES % K_CHUNK == 0:
            _project_chunk(mixed_ref, w_ref, x_ref, o_ref, (c + 1) * LANES // K_CHUNK - 1)
    _memory_branch(qm_ref, zm_ref, mk_ref, mv_ref, mixed_ref, w_ref, x_ref, o_ref, n_seq, seq_rows)
    _finish_rows(o_ref, g_ref, (), inv_ref, True)


def _tail_b(a, rest, mem_k, mem_v, w_out, x, gain, *, bsz, t_len, tt):
    m = bsz * t_len
    seq_rows = min(tt, t_len)
    n_seq = tt // seq_rows
    nt = t_len // seq_rows
    assert t_len % seq_rows == 0 and bsz % n_seq == 0
    tile = lambda b, t: b * nt + t
    narrow = lambda k: pl.BlockSpec((tt, M_DIM), lambda b, t: (tile(b, t), k))
    mem = pl.BlockSpec((n_seq, N_MEM, M_DIM), lambda b, t: (b, 0, 0))
    wide = pl.BlockSpec((tt, F_DIM), lambda b, t: (tile(b, t), 0))
    row = pl.BlockSpec((tt, D_MODEL), lambda b, t: (tile(b, t), 0))
    q_col = F_DIM // M_DIM
    return pl.pallas_call(
        _tail_b_kernel,
        grid=(bsz // n_seq, nt),
        in_specs=[wide, wide, narrow(q_col), narrow(q_col + 1), mem, mem,
                  _resident(MIX_DIM, D_MODEL), row, _resident(1, D_MODEL)],
        out_specs=row,
        out_shape=jax.ShapeDtypeStruct((m, D_MODEL), F32),
        scratch_shapes=[pltpu.VMEM((tt, MIX_DIM), BF16), pltpu.VMEM((tt, LANES), F32)],
        compiler_params=_params("parallel", "parallel"),
        name="tail_b",
    )(a, rest, rest, rest, mem_k, mem_v, w_out, x, gain)


def _proj_kernel(h_ref, w_ref, o_ref):
    o_ref[...] = jnp.dot(h_ref[...], w_ref[...].astype(BF16),
                         preferred_element_type=F32).astype(o_ref.dtype)


def _proj(h, w, *, first_col, tm, tn, out_dtype):
    m, d = h.shape
    n = w.shape[1] - first_col
    tm = min(tm, m)
    assert m % tm == 0 and n % tn == 0 and first_col % tn == 0
    return pl.pallas_call(
        _proj_kernel,
        grid=(m // tm, n // tn),
        in_specs=[pl.BlockSpec((tm, d), lambda i, j: (i, 0)),
                  pl.BlockSpec((d, tn), lambda i, j: (0, j + first_col // tn))],
        out_specs=pl.BlockSpec((tm, tn), lambda i, j: (i, j)),
        out_shape=jax.ShapeDtypeStruct((m, n), out_dtype),
        compiler_params=_params("parallel", "arbitrary"),
        name="proj",
    )(h, w)


def _kvf_kernel(h_ref, wkv_ref, wf_ref, bf_ref, c0_ref, *refs,
                seg, tiles_per_seq, tn, emit_vt):
    if emit_vt:
        k_ref, v_ref, logf_ref, c_ref, kx_ref, vt_ref, carry_ref = refs
    else:
        k_ref, v_ref, logf_ref, c_ref, kx_ref, carry_ref = refs
    n_seq, seq_rows = k_ref.shape[0], k_ref.shape[2]
    i = pl.program_id(0)
    tm = h_ref.shape[0]
    h = h_ref[...]

    heads_per_chunk = tn // HEAD_DIM
    n_chunks = F_DIM // tn

    def project(n):
        return jnp.dot(h, wkv_ref[:, n * tn:(n + 1) * tn], preferred_element_type=F32)

    logit = jnp.dot(h, wf_ref[...], preferred_element_type=F32) + bf_ref[...]

    for n in range(n_chunks, 2 * n_chunks):
        y = project(n)
        for hh in range(heads_per_chunk):
            head = (n - n_chunks) * heads_per_chunk + hh
            yh = y[:, hh * HEAD_DIM:(hh + 1) * HEAD_DIM]
            if emit_vt:
                vt_ref[0, head, 0, 0:HEAD_DIM, :] = yh.T.astype(BF16)
                one_row = lax.broadcasted_iota(jnp.int32, (VT_ROWS - HEAD_DIM, tm), 0) == 0
                vt_ref[0, head, 0, HEAD_DIM:VT_ROWS, :] = jnp.where(one_row, 1.0, 0.0).astype(BF16)
            for s in range(n_seq):
                v_ref[s, head] = yh[s * seq_rows:(s + 1) * seq_rows]

    logf = jnp.minimum(logit, 0.0) - jnp.log1p(jnp.exp(-jnp.abs(logit)))
    logf_ref[...] = logf[:, :N_HEADS_F]
    pos = lax.broadcasted_iota(jnp.int32, (tm, LANES), 0) & (seg - 1)
    c = logf
    step = 1
    while step < seg:
        c = c + jnp.where(pos >= step, pltpu.roll(c, step, axis=0), 0.0)
        step *= 2
    if tiles_per_seq > 1:
        @pl.when(i % tiles_per_seq == 0)
        def _():
            carry_ref[...] = c0_ref[0:1, :]

        c = c + carry_ref[...]
        carry_ref[...] = c[tm - 1:tm, :]
    else:
        c = c + c0_ref[...]
    c_ref[...] = c

    c2 = c * LOG2E
    for n in range(n_chunks):
        y = project(n)
        for hh in range(heads_per_chunk):
            head = n * heads_per_chunk + hh
            yh = y[:, hh * HEAD_DIM:(hh + 1) * HEAD_DIM]
            yb = yh.astype(BF16)
            aug = _aug_block(c2[:, head:head + 1], True)
            for s in range(n_seq):
                rows = slice(s * seq_rows, (s + 1) * seq_rows)
                k_ref[s, head] = yh[rows]
                kx_ref[s, head, :, 0:HEAD_DIM] = yb[rows]
                kx_ref[s, head, :, HEAD_DIM:AUG_DIM] = aug[rows]


def _kvf(h, w_kv, w_f, b_f, c0, *, bsz, t_len, tm, emit_vt):
    m, d = h.shape
    tm = min(tm, m)
    seg = min(t_len, tm)
    assert m % tm == 0 and seg & (seg - 1) == 0 and (t_len % tm == 0 or tm % t_len == 0)
    tiles_per_seq = max(t_len // tm, 1)
    n_seq = tm // seg
    row = lambda w: pl.BlockSpec((tm, w), lambda i: (i, 0))
    heads = lambda w: pl.BlockSpec((n_seq, N_HEADS_F, seg, w),
                                   lambda i: (i // tiles_per_seq, 0, i % tiles_per_seq, 0))
    out_specs = [heads(HEAD_DIM), heads(HEAD_DIM), row(N_HEADS_F), row(LANES), heads(AUG_DIM)]
    out_shape = [jax.ShapeDtypeStruct((bsz, N_HEADS_F, t_len, HEAD_DIM), F32),
                 jax.ShapeDtypeStruct((bsz, N_HEADS_F, t_len, HEAD_DIM), F32),
                 jax.ShapeDtypeStruct((m, N_HEADS_F), F32),
                 jax.ShapeDtypeStruct((m, LANES), F32),
                 jax.ShapeDtypeStruct((bsz, N_HEADS_F, t_len, AUG_DIM), BF16)]
    if emit_vt:
        assert tiles_per_seq * tm == t_len
        out_specs.append(pl.BlockSpec(
            (1, N_HEADS_F, 1, VT_ROWS, tm),
            lambda i: (i // tiles_per_seq, 0, i % tiles_per_seq, 0, 0)))
        out_shape.append(
            jax.ShapeDtypeStruct((bsz, N_HEADS_F, tiles_per_seq, VT_ROWS, tm), BF16))
    return pl.pallas_call(
        functools.partial(_kvf_kernel, seg=seg, tiles_per_seq=tiles_per_seq, tn=512,
                          emit_vt=emit_vt),
        grid=(m // tm,),
        in_specs=[row(d), _resident(d, 2 * F_DIM), _resident(d, LANES), _resident(1, LANES),
                  row(LANES)],
        out_specs=out_specs,
        out_shape=out_shape,
        scratch_shapes=[pltpu.VMEM((1, LANES), F32)],
        compiler_params=_params("arbitrary"),
        name="kvf",
    )(h, w_kv, w_f, b_f, c0)


def _qproj_kernel(h_ref, wq_ref, c_ref, qx_ref, *, tn):
    h = h_ref[...]
    c = c_ref[...] * LOG2E
    heads_per_chunk = tn // HEAD_DIM
    for n in range(F_DIM // tn):
        y = jnp.dot(h, wq_ref[:, n * tn:(n + 1) * tn],
                    preferred_element_type=F32) * (SCALE * LOG2E)
        for hh in range(heads_per_chunk):
            head = n * heads_per_chunk + hh
            base = head * AUG_DIM
            qx_ref[:, base:base + HEAD_DIM] = y[:, hh * HEAD_DIM:(hh + 1) * HEAD_DIM].astype(BF16)
            qx_ref[:, base + HEAD_DIM:base + AUG_DIM] = _aug_block(c[:, head:head + 1], False)


def _qproj(h, w_q, c, *, tm):
    m, d = h.shape
    tm = min(tm, m)
    assert m % tm == 0
    return pl.pallas_call(
        functools.partial(_qproj_kernel, tn=512),
        grid=(m // tm,),
        in_specs=[pl.BlockSpec((tm, d), lambda i: (i, 0)),
                  _resident(d, F_DIM),
                  pl.BlockSpec((tm, LANES), lambda i: (i, 0))],
        out_specs=pl.BlockSpec((tm, N_HEADS_F * AUG_DIM), lambda i: (i, 0)),
        out_shape=jax.ShapeDtypeStruct((m, N_HEADS_F * AUG_DIM), BF16),
        compiler_params=_params("parallel"),
        name="qproj",
    )(h, w_q, c)


def _fox_prompt_kernel(qx_ref, kx_ref, vt_ref, o_ref, m_ref, acc_ref, sa_ref, sb_ref, ma_ref, mb_ref,
                       *, cw):
    qi = pl.program_id(2)
    tq = qx_ref.shape[0]
    tk = sa_ref.shape[1]
    tkv = vt_ref.shape[4]
    n_chunks = tq // cw
    buf_a, buf_b = (sa_ref, ma_ref), (sb_ref, mb_ref)
    m_ref[...] = jnp.full_like(m_ref, -jnp.inf)
    acc_ref[...] = jnp.zeros_like(acc_ref)

    def scores(kb, visible, buf):
        s_ref, smax_ref = buf
        for c, (keys, key_minus_query) in visible.items():
            k = kx_ref[0, 0, pl.ds(pl.multiple_of(kb * tk, tk), keys), :]
            q = qx_ref[c * cw:(c + 1) * cw, :]
            s = lax.dot_general(k, q, (((1,), (1,)), ((), ())), preferred_element_type=F32)
            if key_minus_query is not None:
                key = lax.broadcasted_iota(jnp.int32, (keys, cw), 0) + key_minus_query
                query = lax.broadcasted_iota(jnp.int32, (keys, cw), 1)
                s = jnp.where(key <= query, s, -jnp.inf)
            s_ref[c, 0:keys, :] = s
            smax_ref[c] = jnp.max(s, axis=0, keepdims=True)

    def update(kb, visible, buf):
        s_ref, smax_ref = buf
        for c, (keys, _) in visible.items():
            m_old = m_ref[c]
            m_new = jnp.maximum(m_old, smax_ref[c])
            alpha = jnp.exp2(m_old - m_new)
            acc = alpha * acc_ref[c]
            for lo in range(0, keys, tkv):
                n = min(tkv, keys - lo)
                p = jnp.exp2(s_ref[c, lo:lo + n, :] - m_new).astype(BF16)
                acc = acc + jnp.dot(vt_ref[0, 0, kb * (tk // tkv) + lo // tkv, :, 0:n], p,
                                    preferred_element_type=F32)
            acc_ref[c] = acc
            m_ref[c] = m_new

    every = {c: (tk, None) for c in range(n_chunks)}
    own = []
    for jj in range(2):
        visible = {}
        for c in range(n_chunks):
            k_lo, q_lo = jj * tk, c * cw
            keys = min(tk, q_lo + cw - k_lo)
            if keys > 0:
                visible[c] = (keys, k_lo - q_lo if k_lo + keys - 1 > q_lo else None)
        own.append(visible)

    @pl.when(qi == 0)
    def _():
        scores(0, own[0], buf_a)
        scores(1, own[1], buf_b)
        update(0, own[0], buf_a)
        update(1, own[1], buf_b)

    @pl.when(qi > 0)
    def _():
        n = 2 * qi
        scores(0, every, buf_a)

        def pair(j):
            scores(j, every, buf_b)
            update(j - 1, every, buf_a)
            scores(j + 1, every, buf_a)
            update(j, every, buf_b)

        def two_pairs(t, carry):
            pair(4 * t + 1)
            pair(4 * t + 3)
            return carry

        n_pairs = qi - 1
        lax.fori_loop(0, n_pairs // 2, two_pairs, 0)

        @pl.when(n_pairs % 2 == 1)
        def _():
            pair(n - 3)

        scores(n - 1, every, buf_b)
        update(n - 2, every, buf_a)
        scores(n, own[0], buf_a)
        update(n - 1, every, buf_b)
        scores(n + 1, own[1], buf_b)
        update(n, own[0], buf_a)
        update(n + 1, own[1], buf_b)

    for c in range(n_chunks):
        acc = acc_ref[c]
        o = acc[0:HEAD_DIM, :] / acc[HEAD_DIM:HEAD_DIM + 1, :]
        o_ref[c * cw:(c + 1) * cw, :] = o.T.astype(o_ref.dtype)


def _fox_prompt(qx, kx, vt, *, bsz, t_len, tq, cw):
    m = bsz * t_len
    nq = t_len // tq
    tk = tq // 2
    tkv = vt.shape[4]
    n_chunks = tq // cw
    assert t_len % tq == 0 and tk % tkv == 0 and tq % cw == 0
    return pl.pallas_call(
        functools.partial(_fox_prompt_kernel, cw=cw),
        grid=(bsz, N_HEADS_F, nq),
        in_specs=[pl.BlockSpec((tq, AUG_DIM), lambda b, h, qi: (b * nq + qi, h)),
                  pl.BlockSpec((1, 1, t_len, AUG_DIM), lambda b, h, qi: (b, h, 0, 0)),
                  pl.BlockSpec((1, 1, t_len // tkv, VT_ROWS, tkv),
                               lambda b, h, qi: (b, h, 0, 0, 0))],
        out_specs=pl.BlockSpec((tq, HEAD_DIM), lambda b, h, qi: (b * nq + qi, h)),
        out_shape=jax.ShapeDtypeStruct((m, F_DIM), BF16),
        scratch_shapes=[pltpu.VMEM((n_chunks, 1, cw), F32),
                        pltpu.VMEM((n_chunks, VT_ROWS, cw), F32),
                        pltpu.VMEM((n_chunks, tk, cw), F32), pltpu.VMEM((n_chunks, tk, cw), F32),
                        pltpu.VMEM((n_chunks, 1, cw), F32), pltpu.VMEM((n_chunks, 1, cw), F32)],
        compiler_params=_params("parallel", "parallel", "arbitrary"),
        name="fox_prompt",
    )(qx, kx, vt)


def _fox_cached_kernel(qx_ref, kxn_ref, vn_ref, cn_ref, ck_ref, cv_ref, cp_ref, o_ref,
                       m_ref, l_ref, acc_ref, cq_ref, smax_ref, s_ref):
    b = pl.program_id(0)
    ki = pl.program_id(1)
    t_new = qx_ref.shape[0]

    @pl.when(ki == 0)
    def _():
        m_ref[...] = jnp.full_like(m_ref, -jnp.inf)
        l_ref[...] = jnp.zeros_like(l_ref)
        acc_ref[...] = jnp.zeros_like(acc_ref)
        for h in range(N_HEADS_F):
            cq_ref[h] = jnp.broadcast_to(cn_ref[:, h:h + 1] * LOG2E, (t_new, LANES))

    def update(h, s, s_max, v):
        n = s.shape[1]
        across = (lambda x: _tile_lanes(x, n // LANES)) if n >= LANES else (lambda x: x[:, :n])
        m_old = m_ref[h]
        m_new = jnp.maximum(m_old, s_max)
        alpha = jnp.exp2(m_old - m_new)
        p = jnp.exp2(s - across(m_new))
        l_ref[h] = alpha * l_ref[h] + jnp.sum(p, axis=-1, keepdims=True)
        acc_ref[h] = alpha * acc_ref[h] + jnp.dot(p.astype(BF16), v, preferred_element_type=F32)
        m_ref[h] = m_new

    tk = ck_ref.shape[2]
    for h in range(N_HEADS_F):
        q = qx_ref[:, h * AUG_DIM:h * AUG_DIM + HEAD_DIM]
        k = ck_ref[0, h].astype(BF16)
        s = lax.dot_general(q, k, (((1,), (1,)), ((), ())), preferred_element_type=F32)
        decay = _tile_lanes(cq_ref[h], tk // LANES) - cp_ref[h, pl.ds(b, 1), :] * LOG2E
        s = s + decay
        s_ref[h] = s
        smax_ref[h] = jnp.broadcast_to(jnp.max(s, axis=-1, keepdims=True), (t_new, LANES))
    for h in range(N_HEADS_F):
        update(h, s_ref[h], smax_ref[h], cv_ref[0, h].astype(BF16))

    @pl.when(ki == pl.num_programs(1) - 1)
    def _():
        row = lax.broadcasted_iota(jnp.int32, (t_new, t_new), 0)
        col = lax.broadcasted_iota(jnp.int32, (t_new, t_new), 1)
        for h in range(N_HEADS_F):
            cols = slice(h * HEAD_DIM, (h + 1) * HEAD_DIM)
            xcols = slice(h * AUG_DIM, (h + 1) * AUG_DIM)
            s = lax.dot_general(qx_ref[:, xcols], kxn_ref[0, h], (((1,), (1,)), ((), ())),
                                preferred_element_type=F32)
            s = jnp.where(col <= row, s, -jnp.inf)
            update(h, s, jnp.max(s, axis=-1, keepdims=True), vn_ref[0, h].astype(BF16))
            o_ref[:, cols] = (acc_ref[h] / l_ref[h]).astype(o_ref.dtype)


def _fox_cached(qx, kx_new, v_new, c_new, cache_k, cache_v, c_past, *, bsz, t_len, tk):
    m = bsz * t_len
    past = cache_k.shape[2]
    assert past % tk == 0 and past > 0
    seq = lambda w: pl.BlockSpec((t_len, w), lambda b, ki: (b, 0))
    new = lambda w: pl.BlockSpec((1, N_HEADS_F, t_len, w), lambda b, ki: (b, 0, 0, 0))
    cache = pl.BlockSpec((1, N_HEADS_F, tk, HEAD_DIM), lambda b, ki: (b, 0, ki, 0))
    return pl.pallas_call(
        _fox_cached_kernel,
        grid=(bsz, past // tk),
        in_specs=[seq(N_HEADS_F * AUG_DIM), new(AUG_DIM), new(HEAD_DIM), seq(LANES),
                  cache, cache,
                  pl.BlockSpec((N_HEADS_F, bsz, tk), lambda b, ki: (0, 0, ki))],
        out_specs=seq(F_DIM),
        out_shape=jax.ShapeDtypeStruct((m, F_DIM), BF16),
        scratch_shapes=[pltpu.VMEM((N_HEADS_F, t_len, LANES), F32) for _ in range(5)]
        + [pltpu.VMEM((N_HEADS_F, t_len, tk), F32)],
        compiler_params=_params("parallel", "arbitrary"),
        name="fox_cached",
    )(qx, kx_new, v_new, c_new, cache_k, cache_v, c_past)


def _cumsum_lanes_kernel(x_ref, o_ref):
    rows, n = x_ref.shape
    r = lax.broadcasted_iota(jnp.int32, (LANES, LANES), 0)
    c = lax.broadcasted_iota(jnp.int32, (LANES, LANES), 1)
    upper = jnp.where(r <= c, 1.0, 0.0).astype(BF16)
    carry = jnp.zeros((rows, 1), F32)
    for j in range(n // LANES):
        cols = slice(j * LANES, (j + 1) * LANES)
        hi, mid, lo = _split3(x_ref[:, cols])
        local = (jnp.dot(hi.astype(BF16), upper, preferred_element_type=F32)
                 + jnp.dot(mid.astype(BF16), upper, preferred_element_type=F32)
                 + jnp.dot(lo.astype(BF16), upper, preferred_element_type=F32))
        o_ref[:, cols] = local + carry
        carry = carry + local[:, LANES - 1:LANES]


def _cumsum_lanes(x):
    rows, n = x.shape
    return pl.pallas_call(
        _cumsum_lanes_kernel,
        grid=(1,),
        in_specs=[pl.BlockSpec((rows, n), lambda i: (0, 0))],
        out_specs=pl.BlockSpec((rows, n), lambda i: (0, 0)),
        out_shape=jax.ShapeDtypeStruct((rows, n), F32),
        compiler_params=_params("arbitrary"),
        name="cumsum_lanes",
    )(x)


def _trunk(x, conv_state, mem_k, mem_v, past, w):
    bsz, t_len, d = x.shape
    m = bsz * t_len
    x0 = x.reshape(m, d)
    tail_rows = 512 if t_len >= 512 else 256

    proj = _norm_proj(x0, w["g_norm"][0], w["w_in_a"], tm=1024, tn=1024, out_dtype=BF16)
    x1, h1, h_kv, new_state = _tail_a(
        proj, conv_state, mem_k[0], mem_v[0], w["conv_w"], w["w_out"][0], x0,
        jnp.stack([w["g_norm"][1], w["g_kv"]]), bsz=bsz, t_len=t_len, tt=tail_rows)

    if past is None:
        c0 = jnp.zeros((m, LANES), F32)
    else:
        cache_k, cache_v, cache_logf = past
        past_len = cache_k.shape[1]
        cache_k = jnp.transpose(cache_k, (0, 2, 1, 3))
        cache_v = jnp.transpose(cache_v, (0, 2, 1, 3))
        logf_t = jnp.transpose(cache_logf, (2, 0, 1)).reshape(N_HEADS_F * bsz, past_len)
        c_past = _cumsum_lanes(logf_t).reshape(N_HEADS_F, bsz, past_len)
        c_end = jnp.pad(c_past[:, :, past_len - 1].T, ((0, 0), (0, LANES - N_HEADS_F)))
        c0 = jnp.repeat(c_end, t_len, axis=0)
    kvf_out = _kvf(h_kv, w["w_kv"], w["w_f"], w["b_f"], c0,
                   bsz=bsz, t_len=t_len, tm=512, emit_vt=past is None)
    k_new, v_new, logf, c, kx = kvf_out[:5]
    qx = _qproj(h1, w["w_q"], c, tm=512)
    rest = _proj(h1, w["w_in_b"], first_col=F_DIM, tm=1024, tn=512, out_dtype=BF16)

    if past is None:
        o = _fox_prompt(qx, kx, kvf_out[5], bsz=bsz, t_len=t_len, tq=1024, cw=256)
    else:
        o = _fox_cached(qx, kx, v_new, c, cache_k, cache_v, c_past,
                        bsz=bsz, t_len=t_len, tk=1024)
    y = _tail_b(o, rest, mem_k[1], mem_v[1], w["w_out"][1], x1, w["g_final"][None],
                bsz=bsz, t_len=t_len, tt=tail_rows)
    return (y.reshape(bsz, t_len, d), new_state[None],
            jnp.transpose(k_new, (0, 2, 1, 3)), jnp.transpose(v_new, (0, 2, 1, 3)),
            logf.reshape(bsz, t_len, N_HEADS_F))


def kernel(x_prompt, x_sample, state_conv, cache_k, cache_v, cache_logf, cache_mem_k, cache_mem_v,
           mem_prompt, g_norm, w_in_a, conv_w, w_in_b, w_out, g_mem, w_mem_kv, g_kv, w_kvf, b_f,
           g_final):
    depth = g_norm.shape[0]
    bp = x_prompt.shape[0]
    bs = x_sample.shape[0]
    w = {
        "g_norm": g_norm, "g_kv": g_kv, "g_final": g_final, "conv_w": conv_w[0],
        "w_in_a": w_in_a[0],
        "w_q": w_in_b[0][:, :F_DIM].astype(BF16),
        "w_in_b": w_in_b[0],
        "w_out": w_out.astype(BF16),
        "w_kv": w_kvf[:, :2 * F_DIM].astype(BF16),
        "w_f": jnp.pad(w_kvf[:, 2 * F_DIM:], ((0, 0), (0, LANES - N_HEADS_F))).astype(BF16),
        "b_f": jnp.pad(b_f, (0, LANES - N_HEADS_F)).reshape(1, LANES),
    }

    mem_rows = mem_prompt.reshape(bp * N_MEM, D_MODEL)
    mem_kv = [_norm_proj(mem_rows, g_mem[i], w_mem_kv[i], tm=512, tn=512)
              for i in range(depth)]
    p_mem_k = jnp.stack([a[:, :M_DIM] for a in mem_kv]).reshape(depth, bp, N_MEM, N_HEADS_M, HEAD_DIM)
    p_mem_v = jnp.stack([a[:, M_DIM:] for a in mem_kv]).reshape(depth, bp, N_MEM, N_HEADS_M, HEAD_DIM)

    zero_conv = jnp.zeros((bp, CONV_W - 1, CONV_DIM), F32)
    y_p, p_state, p_k, p_v, p_logf = _trunk(
        x_prompt, zero_conv, p_mem_k.reshape(depth, bp, N_MEM, M_DIM),
        p_mem_v.reshape(depth, bp, N_MEM, M_DIM), None, w)
    y_s, s_state, s_k, s_v, s_logf = _trunk(
        x_sample, state_conv[0], cache_mem_k.reshape(depth, bs, N_MEM, M_DIM),
        cache_mem_v.reshape(depth, bs, N_MEM, M_DIM), (cache_k, cache_v, cache_logf), w)
    return (y_p, y_s, p_state, p_k, p_v, p_logf, p_mem_k, p_mem_v, s_state, s_k, s_v, s_logf)
```

```python
import functools
import math
from typing import NamedTuple

import jax
import jax.numpy as jnp
from jax import lax
from jax.experimental import pallas as pl
from jax.experimental.pallas import tpu as pltpu

F32 = jnp.float32
BF16 = jnp.bfloat16

D_MODEL = 2048
CONV_W = 3
CONV_DIM = 1536
HEAD_DIM = 128
N_HEADS_F = 12
F_DIM = N_HEADS_F * HEAD_DIM
N_MEM = 256
N_HEADS_M = 4
M_DIM = N_HEADS_M * HEAD_DIM
MIX_DIM = CONV_DIM + M_DIM
EPS = 1e-6
SCALE = 1.0 / math.sqrt(HEAD_DIM)
LOG2E = math.log2(math.e)

LANES = 128
AUG_DIM = 2 * HEAD_DIM
VT_ROWS = HEAD_DIM + 16
VMEM_LIMIT = 56 * 1024 * 1024
NORM_ROWS = 64


def _params(*sem):
    return pltpu.CompilerParams(dimension_semantics=sem, vmem_limit_bytes=VMEM_LIMIT)


def _resident(*shape):
    return pl.BlockSpec(shape, lambda *_: (0,) * len(shape), pipeline_mode=pl.Buffered(1))


def _tile_lanes(x, n):
    return jnp.concatenate([x] * n, axis=1)


def _silu(z):
    return z * (1.0 / (1.0 + jnp.exp(-z)))


def _split3(c):
    hi = c.astype(BF16).astype(F32)
    r = c - hi
    mid = r.astype(BF16).astype(F32)
    lo = (r - mid).astype(BF16).astype(F32)
    return hi, mid, lo


def _aug_block(c_col, key_side):
    rows = c_col.shape[0]
    hi, mid, lo = _split3(-c_col if key_side else c_col)
    lane = lax.broadcasted_iota(jnp.int32, (rows, LANES), 1)
    term0 = 3 if key_side else 0
    one0 = 0 if key_side else 3
    terms = jnp.where(lane == term0, hi,
                      jnp.where(lane == term0 + 1, mid, jnp.where(lane == term0 + 2, lo, 0.0)))
    return jnp.where((lane >= one0) & (lane < one0 + 3), 1.0, terms).astype(BF16)


def _rmsnorm_rows(x_ref, g_ref, h_ref, rows_per_chunk=NORM_ROWS):
    tm = x_ref.shape[0]
    g = g_ref[...]

    def body(r, carry):
        rows = pl.ds(pl.multiple_of(r * rows_per_chunk, rows_per_chunk), rows_per_chunk)
        x = x_ref[rows, :]
        ms = jnp.mean(x * x, axis=-1, keepdims=True)
        h_ref[rows, :] = (x * lax.rsqrt(ms + EPS) * g).astype(h_ref.dtype)
        return carry

    lax.fori_loop(0, tm // rows_per_chunk, body, 0)


def _norm_proj_kernel(x_ref, g_ref, w_ref, o_ref, h_ref):
    @pl.when(pl.program_id(1) == 0)
    def _():
        _rmsnorm_rows(x_ref, g_ref, h_ref)

    o_ref[...] = jnp.dot(h_ref[...], w_ref[...].astype(BF16),
                         preferred_element_type=F32).astype(o_ref.dtype)


def _norm_proj(x, g, w, *, tm, tn, out_dtype=F32):
    m, d = x.shape
    n = w.shape[1]
    tm = min(tm, m)
    assert m % tm == 0 and n % tn == 0
    return pl.pallas_call(
        _norm_proj_kernel,
        grid=(m // tm, n // tn),
        in_specs=[
            pl.BlockSpec((tm, d), lambda i, j: (i, 0)),
            pl.BlockSpec((1, d), lambda i, j: (0, 0)),
            pl.BlockSpec((d, tn), lambda i, j: (0, j)),
        ],
        out_specs=pl.BlockSpec((tm, tn), lambda i, j: (i, j)),
        out_shape=jax.ShapeDtypeStruct((m, n), out_dtype),
        scratch_shapes=[pltpu.VMEM((tm, d), BF16)],
        compiler_params=_params("parallel", "arbitrary"),
        name="norm_proj",
    )(x, g.reshape(1, d), w)


K_CHUNK = 512


def _memory_attention_into(qm_ref, zm_ref, mk_ref, mv_ref, mixed_ref, n_seq, seq_rows, heads):
    for s in range(n_seq):
        rows = slice(s * seq_rows, (s + 1) * seq_rows)
        for h in heads:
            cols = slice(h * HEAD_DIM, (h + 1) * HEAD_DIM)
            q = (qm_ref[rows, cols].astype(F32) * SCALE).astype(BF16)
            k = mk_ref[s, :, cols].astype(BF16)
            v = mv_ref[s, :, cols].astype(BF16)
            sc = lax.dot_general(q, k, (((1,), (1,)), ((), ())), preferred_element_type=F32)
            p = jnp.exp(sc - jnp.max(sc, axis=-1, keepdims=True))
            l = jnp.sum(p, axis=-1, keepdims=True)
            o = jnp.dot(p.astype(BF16), v, preferred_element_type=F32) / l
            out_cols = slice(CONV_DIM + h * HEAD_DIM, CONV_DIM + (h + 1) * HEAD_DIM)
            mixed_ref[rows, out_cols] = (o * _silu(zm_ref[rows, cols].astype(F32))).astype(
                mixed_ref.dtype)


def _memory_branch(qm_ref, zm_ref, mk_ref, mv_ref, mixed_ref, w_ref, x_ref, o_ref, n_seq, seq_rows):
    heads_per_chunk = K_CHUNK // HEAD_DIM
    for kc in range(CONV_DIM // K_CHUNK, MIX_DIM // K_CHUNK):
        first = (kc * K_CHUNK - CONV_DIM) // HEAD_DIM
        _memory_attention_into(qm_ref, zm_ref, mk_ref, mv_ref, mixed_ref, n_seq, seq_rows,
                               range(first, first + heads_per_chunk))
        _project_chunk(mixed_ref, w_ref, x_ref, o_ref, kc)


def _project_chunk(mixed_ref, w_ref, x_ref, o_ref, kc, tn=512):
    krows = slice(kc * K_CHUNK, (kc + 1) * K_CHUNK)
    a = mixed_ref[:, krows]
    for c in range(o_ref.shape[1] // tn):
        cols = slice(c * tn, (c + 1) * tn)
        base = x_ref[:, cols] if kc == 0 else o_ref[:, cols]
        o_ref[:, cols] = base + jnp.dot(a, w_ref[krows, cols], preferred_element_type=F32)


def _finish_rows(o_ref, g_ref, h_refs, inv_ref, final_norm, rows_per_chunk=NORM_ROWS):
    n_chunks = o_ref.shape[0] // rows_per_chunk
    chunk = lambda r: pl.ds(pl.multiple_of(r * rows_per_chunk, rows_per_chunk), rows_per_chunk)

    def stats(r, carry):
        x = o_ref[chunk(r), :]
        inv = lax.rsqrt(jnp.mean(x * x, axis=-1, keepdims=True) + EPS)
        inv_ref[chunk(r), :] = jnp.broadcast_to(inv, (rows_per_chunk, LANES))
        return carry

    def scale(r, carry):
        xn = o_ref[chunk(r), :] * _tile_lanes(inv_ref[chunk(r), :], o_ref.shape[1] // LANES)
        if final_norm:
            o_ref[chunk(r), :] = xn * g_ref[0:1, :]
        for k, h_ref in enumerate(h_refs):
            h_ref[chunk(r), :] = (xn * g_ref[k:k + 1, :]).astype(h_ref.dtype)
        return carry

    lax.fori_loop(0, n_chunks, stats, 0, unroll=True)
    lax.fori_loop(0, n_chunks, scale, 0, unroll=True)


def _tail_a_kernel(bg_ref, cg_ref, u_ref, zc_ref, cgh_ref, uh_ref, st_ref, qm_ref, zm_ref,
                   mk_ref, mv_ref, cw_ref, w_ref, x_ref, g_ref,
                   o_ref, h1_ref, h2_ref, nst_ref, mixed_ref, inv_ref):
    t = pl.program_id(1)
    n_seq, seq_rows = st_ref.shape[0], bg_ref.shape[0] // st_ref.shape[0]
    first = t == 0
    row = lax.broadcasted_iota(jnp.int32, (seq_rows, LANES), 0)
    hr = cgh_ref.shape[0]
    for c in range(CONV_DIM // LANES):
        cols = slice(c * LANES, (c + 1) * LANES)
        w = cw_ref[:, cols]
        halo = cgh_ref[:, cols].astype(F32) * uh_ref[:, cols].astype(F32)
        for s in range(n_seq):
            rows = slice(s * seq_rows, (s + 1) * seq_rows)
            ci = cg_ref[rows, cols].astype(F32) * u_ref[rows, cols].astype(F32)
            st = st_ref[s, :, cols]
            prev1 = jnp.where(first, st[1:2, :], halo[hr - 1:hr, :])
            prev2 = jnp.where(first, st[0:1, :], halo[hr - 2:hr - 1, :])
            s1 = jnp.where(row == 0, prev1, pltpu.roll(ci, 1, axis=0))
            s2 = jnp.where(row == 0, prev2, jnp.where(row == 1, prev1, pltpu.roll(ci, 2, axis=0)))
            conv = w[0:1, :] * s2 + w[1:2, :] * s1 + w[2:3, :] * ci
            branch = bg_ref[rows, cols].astype(F32) * conv * _silu(zc_ref[rows, cols].astype(F32))
            mixed_ref[rows, cols] = branch.astype(mixed_ref.dtype)
            nst_ref[s, :, cols] = ci[seq_rows - 2:seq_rows, :]

        if (c + 1) * LANES % K_CHUNK == 0:
            _project_chunk(mixed_ref, w_ref, x_ref, o_ref, (c + 1) * LANES // K_CHUNK - 1)

    _memory_branch(qm_ref, zm_ref, mk_ref, mv_ref, mixed_ref, w_ref, x_ref, o_ref, n_seq, seq_rows)
    _finish_rows(o_ref, g_ref, (h1_ref, h2_ref), inv_ref, False)


def _tail_a(proj, state, mem_k, mem_v, conv_w, w_out, x, gains, *, bsz, t_len, tt):
    m = bsz * t_len
    seq_rows = min(tt, t_len)
    n_seq = tt // seq_rows
    nt = t_len // seq_rows
    hr = 16
    assert t_len % seq_rows == 0 and bsz % n_seq == 0 and seq_rows % hr == 0
    tile = lambda b, t: b * nt + t
    wide = lambda k: pl.BlockSpec((tt, CONV_DIM), lambda b, t: (tile(b, t), k))
    halo = lambda k: pl.BlockSpec(
        (hr, CONV_DIM), lambda b, t: (jnp.maximum(tile(b, t) * (tt // hr) - 1, 0), k))
    narrow = lambda k: pl.BlockSpec((tt, M_DIM), lambda b, t: (tile(b, t), k))
    per_seq = lambda *shape: pl.BlockSpec((n_seq,) + shape, lambda b, t: (b,) + (0,) * len(shape))
    row = pl.BlockSpec((tt, D_MODEL), lambda b, t: (tile(b, t), 0))
    q_col = 4 * CONV_DIM // M_DIM
    return pl.pallas_call(
        _tail_a_kernel,
        grid=(bsz // n_seq, nt),
        in_specs=[wide(0), wide(1), wide(2), wide(3), halo(1), halo(2),
                  per_seq(CONV_W - 1, CONV_DIM), narrow(q_col), narrow(q_col + 1),
                  per_seq(N_MEM, M_DIM), per_seq(N_MEM, M_DIM), _resident(CONV_W, CONV_DIM),
                  _resident(MIX_DIM, D_MODEL), row, _resident(2, D_MODEL)],
        out_specs=[row, row, row, per_seq(CONV_W - 1, CONV_DIM)],
        out_shape=[jax.ShapeDtypeStruct((m, D_MODEL), F32),
                   jax.ShapeDtypeStruct((m, D_MODEL), BF16),
                   jax.ShapeDtypeStruct((m, D_MODEL), BF16),
                   jax.ShapeDtypeStruct((bsz, CONV_W - 1, CONV_DIM), F32)],
        scratch_shapes=[pltpu.VMEM((tt, MIX_DIM), BF16), pltpu.VMEM((tt, LANES), F32)],
        compiler_params=_params("parallel", "arbitrary"),
        name="tail_a",
    )(proj, proj, proj, proj, proj, proj, state, proj, proj, mem_k, mem_v, conv_w, w_out, x, gains)


def _tail_b_kernel(a_ref, zf_ref, qm_ref, zm_ref, mk_ref, mv_ref, w_ref, x_ref, g_ref,
                   o_ref, mixed_ref, inv_ref):
    n_seq = mk_ref.shape[0]
    seq_rows = a_ref.shape[0] // n_seq
    for c in range(F_DIM // LANES):
        cols = slice(c * LANES, (c + 1) * LANES)
        mixed_ref[:, cols] = (a_ref[:, cols].astype(F32)
                              * _silu(zf_ref[:, cols].astype(F32))).astype(mixed_ref.dtype)
        if (c + 1) * LANES % K_CHUNK == 0:
            _project_chunk(mixed_ref, w_ref, x_ref, o_ref, (c + 1) * LANES // K_CHUNK - 1)
    _memory_branch(qm_ref, zm_ref, mk_ref, mv_ref, mixed_ref, w_ref, x_ref, o_ref, n_seq, seq_rows)
    _finish_rows(o_ref, g_ref, (), inv_ref, True)


def _tail_b(a, rest, mem_k, mem_v, w_out, x, gain, *, bsz, t_len, tt):
    m = bsz * t_len
    seq_rows = min(tt, t_len)
    n_seq = tt // seq_rows
    nt = t_len // seq_rows
    assert t_len % seq_rows == 0 and bsz % n_seq == 0
    tile = lambda b, t: b * nt + t
    narrow = lambda k: pl.BlockSpec((tt, M_DIM), lambda b, t: (tile(b, t), k))
    mem = pl.BlockSpec((n_seq, N_MEM, M_DIM), lambda b, t: (b, 0, 0))
    wide = pl.BlockSpec((tt, F_DIM), lambda b, t: (tile(b, t), 0))
    row = pl.BlockSpec((tt, D_MODEL), lambda b, t: (tile(b, t), 0))
    q_col = F_DIM // M_DIM
    return pl.pallas_call(
        _tail_b_kernel,
        grid=(bsz // n_seq, nt),
        in_specs=[wide, wide, narrow(q_col), narrow(q_col + 1), mem, mem,
                  _resident(MIX_DIM, D_MODEL), row, _resident(1, D_MODEL)],
        out_specs=row,
        out_shape=jax.ShapeDtypeStruct((m, D_MODEL), F32),
        scratch_shapes=[pltpu.VMEM((tt, MIX_DIM), BF16), pltpu.VMEM((tt, LANES), F32)],
        compiler_params=_params("parallel", "parallel"),
        name="tail_b",
    )(a, rest, rest, rest, mem_k, mem_v, w_out, x, gain)


def _proj_kernel(h_ref, w_ref, o_ref):
    o_ref[...] = jnp.dot(h_ref[...], w_ref[...].astype(BF16),
                         preferred_element_type=F32).astype(o_ref.dtype)


def _proj(h, w, *, first_col, tm, tn, out_dtype):
    m, d = h.shape
    n = w.shape[1] - first_col
    tm = min(tm, m)
    assert m % tm == 0 and n % tn == 0 and first_col % tn == 0
    return pl.pallas_call(
        _proj_kernel,
        grid=(m // tm, n // tn),
        in_specs=[pl.BlockSpec((tm, d), lambda i, j: (i, 0)),
                  pl.BlockSpec((d, tn), lambda i, j: (0, j + first_col // tn))],
        out_specs=pl.BlockSpec((tm, tn), lambda i, j: (i, j)),
        out_shape=jax.ShapeDtypeStruct((m, n), out_dtype),
        compiler_params=_params("parallel", "arbitrary"),
        name="proj",
    )(h, w)


def _kvf_kernel(h_ref, wkv_ref, wf_ref, bf_ref, c0_ref, *refs,
                seg, tiles_per_seq, tn, emit_vt):
    if emit_vt:
        k_ref, v_ref, logf_ref, c_ref, kx_ref, vt_ref, carry_ref = refs
    else:
        k_ref, v_ref, logf_ref, c_ref, kx_ref, carry_ref = refs
    n_seq, seq_rows = k_ref.shape[0], k_ref.shape[2]
    i = pl.program_id(0)
    tm = h_ref.shape[0]
    h = h_ref[...]

    heads_per_chunk = tn // HEAD_DIM
    n_chunks = F_DIM // tn

    def project(n):
        return jnp.dot(h, wkv_ref[:, n * tn:(n + 1) * tn], preferred_element_type=F32)

    logit = jnp.dot(h, wf_ref[...], preferred_element_type=F32) + bf_ref[...]

    for n in range(n_chunks, 2 * n_chunks):
        y = project(n)
        for hh in range(heads_per_chunk):
            head = (n - n_chunks) * heads_per_chunk + hh
            yh = y[:, hh * HEAD_DIM:(hh + 1) * HEAD_DIM]
            if emit_vt:
                vt_ref[0, head, 0, 0:HEAD_DIM, :] = yh.T.astype(BF16)
                one_row = lax.broadcasted_iota(jnp.int32, (VT_ROWS - HEAD_DIM, tm), 0) == 0
                vt_ref[0, head, 0, HEAD_DIM:VT_ROWS, :] = jnp.where(one_row, 1.0, 0.0).astype(BF16)
            for s in range(n_seq):
                v_ref[s, head] = yh[s * seq_rows:(s + 1) * seq_rows]

    logf = jnp.minimum(logit, 0.0) - jnp.log1p(jnp.exp(-jnp.abs(logit)))
    logf_ref[...] = logf[:, :N_HEADS_F]
    pos = lax.broadcasted_iota(jnp.int32, (tm, LANES), 0) & (seg - 1)
    c = logf
    step = 1
    while step < seg:
        c = c + jnp.where(pos >= step, pltpu.roll(c, step, axis=0), 0.0)
        step *= 2
    if tiles_per_seq > 1:
        @pl.when(i % tiles_per_seq == 0)
        def _():
            carry_ref[...] = c0_ref[0:1, :]

        c = c + carry_ref[...]
        carry_ref[...] = c[tm - 1:tm, :]
    else:
        c = c + c0_ref[...]
    c_ref[...] = c

    c2 = c * LOG2E
    for n in range(n_chunks):
        y = project(n)
        for hh in range(heads_per_chunk):
            head = n * heads_per_chunk + hh
            yh = y[:, hh * HEAD_DIM:(hh + 1) * HEAD_DIM]
            yb = yh.astype(BF16)
            aug = _aug_block(c2[:, head:head + 1], True)
            for s in range(n_seq):
                rows = slice(s * seq_rows, (s + 1) * seq_rows)
                k_ref[s, head] = yh[rows]
                kx_ref[s, head, :, 0:HEAD_DIM] = yb[rows]
                kx_ref[s, head, :, HEAD_DIM:AUG_DIM] = aug[rows]


def _kvf(h, w_kv, w_f, b_f, c0, *, bsz, t_len, tm, emit_vt):
    m, d = h.shape
    tm = min(tm, m)
    seg = min(t_len, tm)
    assert m % tm == 0 and seg & (seg - 1) == 0 and (t_len % tm == 0 or tm % t_len == 0)
    tiles_per_seq = max(t_len // tm, 1)
    n_seq = tm // seg
    row = lambda w: pl.BlockSpec((tm, w), lambda i: (i, 0))
    heads = lambda w: pl.BlockSpec((n_seq, N_HEADS_F, seg, w),
                                   lambda i: (i // tiles_per_seq, 0, i % tiles_per_seq, 0))
    out_specs = [heads(HEAD_DIM), heads(HEAD_DIM), row(N_HEADS_F), row(LANES), heads(AUG_DIM)]
    out_shape = [jax.ShapeDtypeStruct((bsz, N_HEADS_F, t_len, HEAD_DIM), F32),
                 jax.ShapeDtypeStruct((bsz, N_HEADS_F, t_len, HEAD_DIM), F32),
                 jax.ShapeDtypeStruct((m, N_HEADS_F), F32),
                 jax.ShapeDtypeStruct((m, LANES), F32),
                 jax.ShapeDtypeStruct((bsz, N_HEADS_F, t_len, AUG_DIM), BF16)]
    if emit_vt:
        assert tiles_per_seq * tm == t_len
        out_specs.append(pl.BlockSpec(
            (1, N_HEADS_F, 1, VT_ROWS, tm),
            lambda i: (i // tiles_per_seq, 0, i % tiles_per_seq, 0, 0)))
        out_shape.append(
            jax.ShapeDtypeStruct((bsz, N_HEADS_F, tiles_per_seq, VT_ROWS, tm), BF16))
    return pl.pallas_call(
        functools.partial(_kvf_kernel, seg=seg, tiles_per_seq=tiles_per_seq, tn=512,
                          emit_vt=emit_vt),
        grid=(m // tm,),
        in_specs=[row(d), _resident(d, 2 * F_DIM), _resident(d, LANES), _resident(1, LANES),
                  row(LANES)],
        out_specs=out_specs,
        out_shape=out_shape,
        scratch_shapes=[pltpu.VMEM((1, LANES), F32)],
        compiler_params=_params("arbitrary"),
        name="kvf",
    )(h, w_kv, w_f, b_f, c0)


def _qproj_kernel(h_ref, wq_ref, c_ref, qx_ref, *, tn):
    h = h_ref[...]
    c = c_ref[...] * LOG2E
    heads_per_chunk = tn // HEAD_DIM
    for n in range(F_DIM // tn):
        y = jnp.dot(h, wq_ref[:, n * tn:(n + 1) * tn],
                    preferred_element_type=F32) * (SCALE * LOG2E)
        for hh in range(heads_per_chunk):
            head = n * heads_per_chunk + hh
            base = head * AUG_DIM
            qx_ref[:, base:base + HEAD_DIM] = y[:, hh * HEAD_DIM:(hh + 1) * HEAD_DIM].astype(BF16)
            qx_ref[:, base + HEAD_DIM:base + AUG_DIM] = _aug_block(c[:, head:head + 1], False)


def _qproj(h, w_q, c, *, tm):
    m, d = h.shape
    tm = min(tm, m)
    assert m % tm == 0
    return pl.pallas_call(
        functools.partial(_qproj_kernel, tn=512),
        grid=(m // tm,),
        in_specs=[pl.BlockSpec((tm, d), lambda i: (i, 0)),
                  _resident(d, F_DIM),
                  pl.BlockSpec((tm, LANES), lambda i: (i, 0))],
        out_specs=pl.BlockSpec((tm, N_HEADS_F * AUG_DIM), lambda i: (i, 0)),
        out_shape=jax.ShapeDtypeStruct((m, N_HEADS_F * AUG_DIM), BF16),
        compiler_params=_params("parallel"),
        name="qproj",
    )(h, w_q, c)


def _fox_prompt_kernel(qx_ref, kx_ref, vt_ref, o_ref, m_ref, acc_ref, sa_ref, sb_ref, ma_ref, mb_ref,
                       *, cw):
    qi = pl.program_id(2)
    tq = qx_ref.shape[0]
    tk = sa_ref.shape[1]
    tkv = vt_ref.shape[4]
    n_chunks = tq // cw
    buf_a, buf_b = (sa_ref, ma_ref), (sb_ref, mb_ref)
    m_ref[...] = jnp.full_like(m_ref, -jnp.inf)
    acc_ref[...] = jnp.zeros_like(acc_ref)

    def scores(kb, visible, buf):
        s_ref, smax_ref = buf
        for c, (keys, key_minus_query) in visible.items():
            k = kx_ref[0, 0, pl.ds(pl.multiple_of(kb * tk, tk), keys), :]
            q = qx_ref[c * cw:(c + 1) * cw, :]
            s = lax.dot_general(k, q, (((1,), (1,)), ((), ())), preferred_element_type=F32)
            if key_minus_query is not None:
                key = lax.broadcasted_iota(jnp.int32, (keys, cw), 0) + key_minus_query
                query = lax.broadcasted_iota(jnp.int32, (keys, cw), 1)
                s = jnp.where(key <= query, s, -jnp.inf)
            s_ref[c, 0:keys, :] = s
            smax_ref[c] = jnp.max(s, axis=0, keepdims=True)

    def update(kb, visible, buf):
        s_ref, smax_ref = buf
        for c, (keys, _) in visible.items():
            m_old = m_ref[c]
            m_new = jnp.maximum(m_old, smax_ref[c])
            alpha = jnp.exp2(m_old - m_new)
            acc = alpha * acc_ref[c]
            for lo in range(0, keys, tkv):
                n = min(tkv, keys - lo)
                p = jnp.exp2(s_ref[c, lo:lo + n, :] - m_new).astype(BF16)
                acc = acc + jnp.dot(vt_ref[0, 0, kb * (tk // tkv) + lo // tkv, :, 0:n], p,
                                    preferred_element_type=F32)
            acc_ref[c] = acc
            m_ref[c] = m_new

    every = {c: (tk, None) for c in range(n_chunks)}
    own = []
    for jj in range(2):
        visible = {}
        for c in range(n_chunks):
            k_lo, q_lo = jj * tk, c * cw
            keys = min(tk, q_lo + cw - k_lo)
            if keys > 0:
                visible[c] = (keys, k_lo - q_lo if k_lo + keys - 1 > q_lo else None)
        own.append(visible)

    @pl.when(qi == 0)
    def _():
        scores(0, own[0], buf_a)
        scores(1, own[1], buf_b)
        update(0, own[0], buf_a)
        update(1, own[1], buf_b)

    @pl.when(qi > 0)
    def _():
        n = 2 * qi
        scores(0, every, buf_a)

        def pair(j):
            scores(j, every, buf_b)
            update(j - 1, every, buf_a)
            scores(j + 1, every, buf_a)
            update(j, every, buf_b)

        def two_pairs(t, carry):
            pair(4 * t + 1)
            pair(4 * t + 3)
            return carry

        n_pairs = qi - 1
        lax.fori_loop(0, n_pairs // 2, two_pairs, 0)

        @pl.when(n_pairs % 2 == 1)
        def _():
            pair(n - 3)

        scores(n - 1, every, buf_b)
        update(n - 2, every, buf_a)
        scores(n, own[0], buf_a)
        update(n - 1, every, buf_b)
        scores(n + 1, own[1], buf_b)
        update(n, own[0], buf_a)
        update(n + 1, own[1], buf_b)

    for c in range(n_chunks):
        acc = acc_ref[c]
        o = acc[0:HEAD_DIM, :] / acc[HEAD_DIM:HEAD_DIM + 1, :]
        o_ref[c * cw:(c + 1) * cw, :] = o.T.astype(o_ref.dtype)


def _fox_prompt(qx, kx, vt, *, bsz, t_len, tq, cw):
    m = bsz * t_len
    nq = t_len // tq
    tk = tq // 2
    tkv = vt.shape[4]
    n_chunks = tq // cw
    assert t_len % tq == 0 and tk % tkv == 0 and tq % cw == 0
    return pl.pallas_call(
        functools.partial(_fox_prompt_kernel, cw=cw),
        grid=(bsz, N_HEADS_F, nq),
        in_specs=[pl.BlockSpec((tq, AUG_DIM), lambda b, h, qi: (b * nq + qi, h)),
                  pl.BlockSpec((1, 1, t_len, AUG_DIM), lambda b, h, qi: (b, h, 0, 0)),
                  pl.BlockSpec((1, 1, t_len // tkv, VT_ROWS, tkv),
                               lambda b, h, qi: (b, h, 0, 0, 0))],
        out_specs=pl.BlockSpec((tq, HEAD_DIM), lambda b, h, qi: (b * nq + qi, h)),
        out_shape=jax.ShapeDtypeStruct((m, F_DIM), BF16),
        scratch_shapes=[pltpu.VMEM((n_chunks, 1, cw), F32),
                        pltpu.VMEM((n_chunks, VT_ROWS, cw), F32),
                        pltpu.VMEM((n_chunks, tk, cw), F32), pltpu.VMEM((n_chunks, tk, cw), F32),
                        pltpu.VMEM((n_chunks, 1, cw), F32), pltpu.VMEM((n_chunks, 1, cw), F32)],
        compiler_params=_params("parallel", "parallel", "arbitrary"),
        name="fox_prompt",
    )(qx, kx, vt)


def _fox_cached_kernel(qx_ref, kxn_ref, vn_ref, cn_ref, ck_ref, cv_ref, cp_ref, o_ref,
                       m_ref, l_ref, acc_ref, cq_ref, smax_ref, s_ref):
    b = pl.program_id(0)
    ki = pl.program_id(1)
    t_new = qx_ref.shape[0]

    @pl.when(ki == 0)
    def _():
        m_ref[...] = jnp.full_like(m_ref, -jnp.inf)
        l_ref[...] = jnp.zeros_like(l_ref)
        acc_ref[...] = jnp.zeros_like(acc_ref)
        for h in range(N_HEADS_F):
            cq_ref[h] = jnp.broadcast_to(cn_ref[:, h:h + 1] * LOG2E, (t_new, LANES))

    def update(h, s, s_max, v):
        n = s.shape[1]
        across = (lambda x: _tile_lanes(x, n // LANES)) if n >= LANES else (lambda x: x[:, :n])
        m_old = m_ref[h]
        m_new = jnp.maximum(m_old, s_max)
        alpha = jnp.exp2(m_old - m_new)
        p = jnp.exp2(s - across(m_new))
        l_ref[h] = alpha * l_ref[h] + jnp.sum(p, axis=-1, keepdims=True)
        acc_ref[h] = alpha * acc_ref[h] + jnp.dot(p.astype(BF16), v, preferred_element_type=F32)
        m_ref[h] = m_new

    tk = ck_ref.shape[2]
    for h in range(N_HEADS_F):
        q = qx_ref[:, h * AUG_DIM:h * AUG_DIM + HEAD_DIM]
        k = ck_ref[0, h].astype(BF16)
        s = lax.dot_general(q, k, (((1,), (1,)), ((), ())), preferred_element_type=F32)
        decay = _tile_lanes(cq_ref[h], tk // LANES) - cp_ref[h, pl.ds(b, 1), :] * LOG2E
        s = s + decay
        s_ref[h] = s
        smax_ref[h] = jnp.broadcast_to(jnp.max(s, axis=-1, keepdims=True), (t_new, LANES))
    for h in range(N_HEADS_F):
        update(h, s_ref[h], smax_ref[h], cv_ref[0, h].astype(BF16))

    @pl.when(ki == pl.num_programs(1) - 1)
    def _():
        row = lax.broadcasted_iota(jnp.int32, (t_new, t_new), 0)
        col = lax.broadcasted_iota(jnp.int32, (t_new, t_new), 1)
        for h in range(N_HEADS_F):
            cols = slice(h * HEAD_DIM, (h + 1) * HEAD_DIM)
            xcols = slice(h * AUG_DIM, (h + 1) * AUG_DIM)
            s = lax.dot_general(qx_ref[:, xcols], kxn_ref[0, h], (((1,), (1,)), ((), ())),
                                preferred_element_type=F32)
            s = jnp.where(col <= row, s, -jnp.inf)
            update(h, s, jnp.max(s, axis=-1, keepdims=True), vn_ref[0, h].astype(BF16))
            o_ref[:, cols] = (acc_ref[h] / l_ref[h]).astype(o_ref.dtype)


def _fox_cached(qx, kx_new, v_new, c_new, cache_k, cache_v, c_past, *, bsz, t_len, tk):
    m = bsz * t_len
    past = cache_k.shape[2]
    assert past % tk == 0 and past > 0
    seq = lambda w: pl.BlockSpec((t_len, w), lambda b, ki: (b, 0))
    new = lambda w: pl.BlockSpec((1, N_HEADS_F, t_len, w), lambda b, ki: (b, 0, 0, 0))
    cache = pl.BlockSpec((1, N_HEADS_F, tk, HEAD_DIM), lambda b, ki: (b, 0, ki, 0))
    return pl.pallas_call(
        _fox_cached_kernel,
        grid=(bsz, past // tk),
        in_specs=[seq(N_HEADS_F * AUG_DIM), new(AUG_DIM), new(HEAD_DIM), seq(LANES),
                  cache, cache,
                  pl.BlockSpec((N_HEADS_F, bsz, tk), lambda b, ki: (0, 0, ki))],
        out_specs=seq(F_DIM),
        out_shape=jax.ShapeDtypeStruct((m, F_DIM), BF16),
        scratch_shapes=[pltpu.VMEM((N_HEADS_F, t_len, LANES), F32) for _ in range(5)]
        + [pltpu.VMEM((N_HEADS_F, t_len, tk), F32)],
        compiler_params=_params("parallel", "arbitrary"),
        name="fox_cached",
    )(qx, kx_new, v_new, c_new, cache_k, cache_v, c_past)


def _cumsum_lanes_kernel(x_ref, o_ref):
    rows, n = x_ref.shape
    r = lax.broadcasted_iota(jnp.int32, (LANES, LANES), 0)
    c = lax.broadcasted_iota(jnp.int32, (LANES, LANES), 1)
    upper = jnp.where(r <= c, 1.0, 0.0).astype(BF16)
    carry = jnp.zeros((rows, 1), F32)
    for j in range(n // LANES):
        cols = slice(j * LANES, (j + 1) * LANES)
        hi, mid, lo = _split3(x_ref[:, cols])
        local = (jnp.dot(hi.astype(BF16), upper, preferred_element_type=F32)
                 + jnp.dot(mid.astype(BF16), upper, preferred_element_type=F32)
                 + jnp.dot(lo.astype(BF16), upper, preferred_element_type=F32))
        o_ref[:, cols] = local + carry
        carry = carry + local[:, LANES - 1:LANES]


def _cumsum_lanes(x):
    rows, n = x.shape
    return pl.pallas_call(
        _cumsum_lanes_kernel,
        grid=(1,),
        in_specs=[pl.BlockSpec((rows, n), lambda i: (0, 0))],
        out_specs=pl.BlockSpec((rows, n), lambda i: (0, 0)),
        out_shape=jax.ShapeDtypeStruct((rows, n), F32),
        compiler_params=_params("arbitrary"),
        name="cumsum_lanes",
    )(x)


class _Tiles(NamedTuple):
    proj_rows: int
    proj_cols: int
    rest_cols: int
    head_rows: int
    tail_rows: int
    attn_queries: int
    attn_chunk: int
    cache_keys: int


def _tiles(t_len):
    tail_rows = 512 if t_len >= 512 else 256
    return _Tiles(proj_rows=1024, proj_cols=1024, rest_cols=512, head_rows=512,
                  tail_rows=tail_rows, attn_queries=1024, attn_chunk=256, cache_keys=1024)


def _trunk(x, conv_state, mem_k, mem_v, past, w):
    bsz, t_len, d = x.shape
    m = bsz * t_len
    x0 = x.reshape(m, d)
    tiles = _tiles(t_len)

    proj = _norm_proj(x0, w["g_norm"][0], w["w_in_a"], tm=tiles.proj_rows, tn=tiles.proj_cols,
                      out_dtype=BF16)
    x1, h1, h_kv, new_state = _tail_a(
        proj, conv_state, mem_k[0], mem_v[0], w["conv_w"], w["w_out"][0], x0,
        jnp.stack([w["g_norm"][1], w["g_kv"]]), bsz=bsz, t_len=t_len, tt=tiles.tail_rows)

    if past is None:
        c0 = jnp.zeros((m, LANES), F32)
    else:
        cache_k, cache_v, cache_logf = past
        past_len = cache_k.shape[1]
        cache_k = jnp.transpose(cache_k, (0, 2, 1, 3))
        cache_v = jnp.transpose(cache_v, (0, 2, 1, 3))
        logf_t = jnp.transpose(cache_logf, (2, 0, 1)).reshape(N_HEADS_F * bsz, past_len)
        c_past = _cumsum_lanes(logf_t).reshape(N_HEADS_F, bsz, past_len)
        c_end = jnp.pad(c_past[:, :, past_len - 1].T, ((0, 0), (0, LANES - N_HEADS_F)))
        c0 = jnp.repeat(c_end, t_len, axis=0)
    kvf_out = _kvf(h_kv, w["w_kv"], w["w_f"], w["b_f"], c0,
                   bsz=bsz, t_len=t_len, tm=tiles.head_rows, emit_vt=past is None)
    k_new, v_new, logf, c, kx = kvf_out[:5]
    qx = _qproj(h1, w["w_q"], c, tm=tiles.head_rows)
    rest = _proj(h1, w["w_in_b"], first_col=F_DIM, tm=tiles.proj_rows, tn=tiles.rest_cols,
                 out_dtype=BF16)

    if past is None:
        o = _fox_prompt(qx, kx, kvf_out[5], bsz=bsz, t_len=t_len, tq=tiles.attn_queries,
                        cw=tiles.attn_chunk)
    else:
        o = _fox_cached(qx, kx, v_new, c, cache_k, cache_v, c_past,
                        bsz=bsz, t_len=t_len, tk=tiles.cache_keys)
    y = _tail_b(o, rest, mem_k[1], mem_v[1], w["w_out"][1], x1, w["g_final"][None],
                bsz=bsz, t_len=t_len, tt=tiles.tail_rows)
    return (y.reshape(bsz, t_len, d), new_state[None],
            jnp.transpose(k_new, (0, 2, 1, 3)), jnp.transpose(v_new, (0, 2, 1, 3)),
            logf.reshape(bsz, t_len, N_HEADS_F))


def kernel(x_prompt, x_sample, state_conv, cache_k, cache_v, cache_logf, cache_mem_k, cache_mem_v,
           mem_prompt, g_norm, w_in_a, conv_w, w_in_b, w_out, g_mem, w_mem_kv, g_kv, w_kvf, b_f,
           g_final):
    depth = g_norm.shape[0]
    bp = x_prompt.shape[0]
    bs = x_sample.shape[0]
    w = {
        "g_norm": g_norm, "g_kv": g_kv, "g_final": g_final, "conv_w": conv_w[0],
        "w_in_a": w_in_a[0],
        "w_q": w_in_b[0][:, :F_DIM].astype(BF16),
        "w_in_b": w_in_b[0],
        "w_out": w_out.astype(BF16),
        "w_kv": w_kvf[:, :2 * F_DIM].astype(BF16),
        "w_f": jnp.pad(w_kvf[:, 2 * F_DIM:], ((0, 0), (0, LANES - N_HEADS_F))).astype(BF16),
        "b_f": jnp.pad(b_f, (0, LANES - N_HEADS_F)).reshape(1, LANES),
    }

    mem_rows = mem_prompt.reshape(bp * N_MEM, D_MODEL)
    mem_kv = [_norm_proj(mem_rows, g_mem[i], w_mem_kv[i], tm=512, tn=512)
              for i in range(depth)]
    p_mem_k = jnp.stack([a[:, :M_DIM] for a in mem_kv]).reshape(depth, bp, N_MEM, N_HEADS_M, HEAD_DIM)
    p_mem_v = jnp.stack([a[:, M_DIM:] for a in mem_kv]).reshape(depth, bp, N_MEM, N_HEADS_M, HEAD_DIM)

    zero_conv = jnp.zeros((bp, CONV_W - 1, CONV_DIM), F32)
    y_p, p_state, p_k, p_v, p_logf = _trunk(
        x_prompt, zero_conv, p_mem_k.reshape(depth, bp, N_MEM, M_DIM),
        p_mem_v.reshape(depth, bp, N_MEM, M_DIM), None, w)
    y_s, s_state, s_k, s_v, s_logf = _trunk(
        x_sample, state_conv[0], cache_mem_k.reshape(depth, bs, N_MEM, M_DIM),
        cache_mem_v.reshape(depth, bs, N_MEM, M_DIM), (cache_k, cache_v, cache_logf), w)
    return (y_p, y_s, p_state, p_k, p_v, p_logf, p_mem_k, p_mem_v, s_state, s_k, s_v, s_logf)
```

```python
import functools
import math
from typing import NamedTuple

import jax
import jax.numpy as jnp
from jax import lax
from jax.experimental import pallas as pl
from jax.experimental.pallas import tpu as pltpu

F32 = jnp.float32
BF16 = jnp.bfloat16

D_MODEL = 2048
CONV_W = 3
CONV_DIM = 1536
HEAD_DIM = 128
N_HEADS_F = 12
F_DIM = N_HEADS_F * HEAD_DIM
N_MEM = 256
N_HEADS_M = 4
M_DIM = N_HEADS_M * HEAD_DIM
MIX_DIM = CONV_DIM + M_DIM
EPS = 1e-6
SCALE = 1.0 / math.sqrt(HEAD_DIM)
LOG2E = math.log2(math.e)

LANES = 128
AUG_DIM = 2 * HEAD_DIM
VT_ROWS = HEAD_DIM + 16
VMEM_LIMIT = 56 * 1024 * 1024
NORM_ROWS = 64


def _params(*sem):
    return pltpu.CompilerParams(dimension_semantics=sem, vmem_limit_bytes=VMEM_LIMIT)


def _resident(*shape):
    return pl.BlockSpec(shape, lambda *_: (0,) * len(shape), pipeline_mode=pl.Buffered(1))


def _tile_lanes(x, n):
    return jnp.concatenate([x] * n, axis=1)


def _silu(z):
    return z * (1.0 / (1.0 + jnp.exp(-z)))


def _split3(c):
    hi = c.astype(BF16).astype(F32)
    r = c - hi
    mid = r.astype(BF16).astype(F32)
    lo = (r - mid).astype(BF16).astype(F32)
    return hi, mid, lo


def _aug_block(c_col, key_side):
    rows = c_col.shape[0]
    hi, mid, lo = _split3(-c_col if key_side else c_col)
    lane = lax.broadcasted_iota(jnp.int32, (rows, LANES), 1)
    term0 = 3 if key_side else 0
    one0 = 0 if key_side else 3
    terms = jnp.where(lane == term0, hi,
                      jnp.where(lane == term0 + 1, mid, jnp.where(lane == term0 + 2, lo, 0.0)))
    return jnp.where((lane >= one0) & (lane < one0 + 3), 1.0, terms).astype(BF16)


def _rmsnorm_rows(x_ref, g_ref, h_ref, rows_per_chunk=NORM_ROWS):
    tm = x_ref.shape[0]
    g = g_ref[...]

    def body(r, carry):
        rows = pl.ds(pl.multiple_of(r * rows_per_chunk, rows_per_chunk), rows_per_chunk)
        x = x_ref[rows, :]
        ms = jnp.mean(x * x, axis=-1, keepdims=True)
        h_ref[rows, :] = (x * lax.rsqrt(ms + EPS) * g).astype(h_ref.dtype)
        return carry

    lax.fori_loop(0, tm // rows_per_chunk, body, 0)


def _norm_proj_kernel(x_ref, g_ref, w_ref, o_ref, h_ref):
    @pl.when(pl.program_id(1) == 0)
    def _():
        _rmsnorm_rows(x_ref, g_ref, h_ref)

    o_ref[...] = jnp.dot(h_ref[...], w_ref[...].astype(BF16),
                         preferred_element_type=F32).astype(o_ref.dtype)


def _norm_proj(x, g, w, *, tm, tn, out_dtype=F32):
    m, d = x.shape
    n = w.shape[1]
    tm = min(tm, m)
    assert m % tm == 0 and n % tn == 0
    return pl.pallas_call(
        _norm_proj_kernel,
        grid=(m // tm, n // tn),
        in_specs=[
            pl.BlockSpec((tm, d), lambda i, j: (i, 0)),
            pl.BlockSpec((1, d), lambda i, j: (0, 0)),
            pl.BlockSpec((d, tn), lambda i, j: (0, j)),
        ],
        out_specs=pl.BlockSpec((tm, tn), lambda i, j: (i, j)),
        out_shape=jax.ShapeDtypeStruct((m, n), out_dtype),
        scratch_shapes=[pltpu.VMEM((tm, d), BF16)],
        compiler_params=_params("parallel", "arbitrary"),
        name="norm_proj",
    )(x, g.reshape(1, d), w)


K_CHUNK = 512


def _memory_attention_into(qm_ref, zm_ref, mk_ref, mv_ref, mixed_ref, n_seq, seq_rows, heads):
    for s in range(n_seq):
        rows = slice(s * seq_rows, (s + 1) * seq_rows)
        for h in heads:
            cols = slice(h * HEAD_DIM, (h + 1) * HEAD_DIM)
            q = (qm_ref[rows, cols].astype(F32) * SCALE).astype(BF16)
            k = mk_ref[s, :, cols].astype(BF16)
            v = mv_ref[s, :, cols].astype(BF16)
            sc = lax.dot_general(q, k, (((1,), (1,)), ((), ())), preferred_element_type=F32)
            p = jnp.exp(sc - jnp.max(sc, axis=-1, keepdims=True))
            l = jnp.sum(p, axis=-1, keepdims=True)
            o = jnp.dot(p.astype(BF16), v, preferred_element_type=F32) / l
            out_cols = slice(CONV_DIM + h * HEAD_DIM, CONV_DIM + (h + 1) * HEAD_DIM)
            mixed_ref[rows, out_cols] = (o * _silu(zm_ref[rows, cols].astype(F32))).astype(
                mixed_ref.dtype)


def _memory_branch(qm_ref, zm_ref, mk_ref, mv_ref, mixed_ref, w_ref, x_ref, o_ref, n_seq, seq_rows):
    heads_per_chunk = K_CHUNK // HEAD_DIM
    for kc in range(CONV_DIM // K_CHUNK, MIX_DIM // K_CHUNK):
        first = (kc * K_CHUNK - CONV_DIM) // HEAD_DIM
        _memory_attention_into(qm_ref, zm_ref, mk_ref, mv_ref, mixed_ref, n_seq, seq_rows,
                               range(first, first + heads_per_chunk))
        _project_chunk(mixed_ref, w_ref, x_ref, o_ref, kc)


def _project_chunk(mixed_ref, w_ref, x_ref, o_ref, kc, tn=512):
    krows = slice(kc * K_CHUNK, (kc + 1) * K_CHUNK)
    a = mixed_ref[:, krows]
    for c in range(o_ref.shape[1] // tn):
        cols = slice(c * tn, (c + 1) * tn)
        base = x_ref[:, cols] if kc == 0 else o_ref[:, cols]
        o_ref[:, cols] = base + jnp.dot(a, w_ref[krows, cols], preferred_element_type=F32)


def _finish_rows(o_ref, g_ref, h_refs, inv_ref, final_norm, rows_per_chunk=NORM_ROWS):
    n_chunks = o_ref.shape[0] // rows_per_chunk
    chunk = lambda r: pl.ds(pl.multiple_of(r * rows_per_chunk, rows_per_chunk), rows_per_chunk)

    def stats(r, carry):
        x = o_ref[chunk(r), :]
        inv = lax.rsqrt(jnp.mean(x * x, axis=-1, keepdims=True) + EPS)
        inv_ref[chunk(r), :] = jnp.broadcast_to(inv, (rows_per_chunk, LANES))
        return carry

    def scale(r, carry):
        xn = o_ref[chunk(r), :] * _tile_lanes(inv_ref[chunk(r), :], o_ref.shape[1] // LANES)
        if final_norm:
            o_ref[chunk(r), :] = xn * g_ref[0:1, :]
        for k, h_ref in enumerate(h_refs):
            h_ref[chunk(r), :] = (xn * g_ref[k:k + 1, :]).astype(h_ref.dtype)
        return carry

    lax.fori_loop(0, n_chunks, stats, 0, unroll=True)
    lax.fori_loop(0, n_chunks, scale, 0, unroll=True)


def _tail_a_kernel(bg_ref, cg_ref, u_ref, zc_ref, cgh_ref, uh_ref, st_ref, qm_ref, zm_ref,
                   mk_ref, mv_ref, cw_ref, w_ref, x_ref, g_ref,
                   o_ref, h1_ref, h2_ref, nst_ref, mixed_ref, inv_ref):
    t = pl.program_id(1)
    n_seq, seq_rows = st_ref.shape[0], bg_ref.shape[0] // st_ref.shape[0]
    first = t == 0
    row = lax.broadcasted_iota(jnp.int32, (seq_rows, LANES), 0)
    hr = cgh_ref.shape[0]
    for c in range(CONV_DIM // LANES):
        cols = slice(c * LANES, (c + 1) * LANES)
        w = cw_ref[:, cols]
        halo = cgh_ref[:, cols].astype(F32) * uh_ref[:, cols].astype(F32)
        for s in range(n_seq):
            rows = slice(s * seq_rows, (s + 1) * seq_rows)
            ci = cg_ref[rows, cols].astype(F32) * u_ref[rows, cols].astype(F32)
            st = st_ref[s, :, cols]
            prev1 = jnp.where(first, st[1:2, :], halo[hr - 1:hr, :])
            prev2 = jnp.where(first, st[0:1, :], halo[hr - 2:hr - 1, :])
            s1 = jnp.where(row == 0, prev1, pltpu.roll(ci, 1, axis=0))
            s2 = jnp.where(row == 0, prev2, jnp.where(row == 1, prev1, pltpu.roll(ci, 2, axis=0)))
            conv = w[0:1, :] * s2 + w[1:2, :] * s1 + w[2:3, :] * ci
            branch = bg_ref[rows, cols].astype(F32) * conv * _silu(zc_ref[rows, cols].astype(F32))
            mixed_ref[rows, cols] = branch.astype(mixed_ref.dtype)
            nst_ref[s, :, cols] = ci[seq_rows - 2:seq_rows, :]

        if (c + 1) * LANES % K_CHUNK == 0:
            _project_chunk(mixed_ref, w_ref, x_ref, o_ref, (c + 1) * LANES // K_CHUNK - 1)

    _memory_branch(qm_ref, zm_ref, mk_ref, mv_ref, mixed_ref, w_ref, x_ref, o_ref, n_seq, seq_rows)
    _finish_rows(o_ref, g_ref, (h1_ref, h2_ref), inv_ref, False)


def _tail_a(proj, state, mem_k, mem_v, conv_w, w_out, x, gains, *, bsz, t_len, tt):
    m = bsz * t_len
    seq_rows = min(tt, t_len)
    n_seq = tt // seq_rows
    nt = t_len // seq_rows
    hr = 16
    assert t_len % seq_rows == 0 and bsz % n_seq == 0 and seq_rows % hr == 0
    tile = lambda b, t: b * nt + t
    wide = lambda k: pl.BlockSpec((tt, CONV_DIM), lambda b, t: (tile(b, t), k))
    halo = lambda k: pl.BlockSpec(
        (hr, CONV_DIM), lambda b, t: (jnp.maximum(tile(b, t) * (tt // hr) - 1, 0), k))
    narrow = lambda k: pl.BlockSpec((tt, M_DIM), lambda b, t: (tile(b, t), k))
    per_seq = lambda *shape: pl.BlockSpec((n_seq,) + shape, lambda b, t: (b,) + (0,) * len(shape))
    row = pl.BlockSpec((tt, D_MODEL), lambda b, t: (tile(b, t), 0))
    q_col = 4 * CONV_DIM // M_DIM
    return pl.pallas_call(
        _tail_a_kernel,
        grid=(bsz // n_seq, nt),
        in_specs=[wide(0), wide(1), wide(2), wide(3), halo(1), halo(2),
                  per_seq(CONV_W - 1, CONV_DIM), narrow(q_col), narrow(q_col + 1),
                  per_seq(N_MEM, M_DIM), per_seq(N_MEM, M_DIM), _resident(CONV_W, CONV_DIM),
                  _resident(MIX_DIM, D_MODEL), row, _resident(2, D_MODEL)],
        out_specs=[row, row, row, per_seq(CONV_W - 1, CONV_DIM)],
        out_shape=[jax.ShapeDtypeStruct((m, D_MODEL), F32),
                   jax.ShapeDtypeStruct((m, D_MODEL), BF16),
                   jax.ShapeDtypeStruct((m, D_MODEL), BF16),
                   jax.ShapeDtypeStruct((bsz, CONV_W - 1, CONV_DIM), F32)],
        scratch_shapes=[pltpu.VMEM((tt, MIX_DIM), BF16), pltpu.VMEM((tt, LANES), F32)],
        compiler_params=_params("parallel", "arbitrary"),
        name="tail_a",
    )(proj, proj, proj, proj, proj, proj, state, proj, proj, mem_k, mem_v, conv_w, w_out, x, gains)


def _tail_b_kernel(a_ref, zf_ref, qm_ref, zm_ref, mk_ref, mv_ref, w_ref, x_ref, g_ref,
                   o_ref, mixed_ref, inv_ref):
    n_seq = mk_ref.shape[0]
    seq_rows = a_ref.shape[0] // n_seq
    for c in range(F_DIM // LANES):
        cols = slice(c * LANES, (c + 1) * LANES)
        mixed_ref[:, cols] = (a_ref[:, cols].astype(F32)
                              * _silu(zf_ref[:, cols].astype(F32))).astype(mixed_ref.dtype)
        if (c + 1) * LANES % K_CHUNK == 0:
            _project_chunk(mixed_ref, w_ref, x_ref, o_ref, (c + 1) * LANES // K_CHUNK - 1)
    _memory_branch(qm_ref, zm_ref, mk_ref, mv_ref, mixed_ref, w_ref, x_ref, o_ref, n_seq, seq_rows)
    _finish_rows(o_ref, g_ref, (), inv_ref, True)


def _tail_b(a, rest, mem_k, mem_v, w_out, x, gain, *, bsz, t_len, tt):
    m = bsz * t_len
    seq_rows = min(tt, t_len)
    n_seq = tt // seq_rows
    nt = t_len // seq_rows
    assert t_len % seq_rows == 0 and bsz % n_seq == 0
    tile = lambda b, t: b * nt + t
    narrow = lambda k: pl.BlockSpec((tt, M_DIM), lambda b, t: (tile(b, t), k))
    mem = pl.BlockSpec((n_seq, N_MEM, M_DIM), lambda b, t: (b, 0, 0))
    wide = pl.BlockSpec((tt, F_DIM), lambda b, t: (tile(b, t), 0))
    row = pl.BlockSpec((tt, D_MODEL), lambda b, t: (tile(b, t), 0))
    q_col = F_DIM // M_DIM
    return pl.pallas_call(
        _tail_b_kernel,
        grid=(bsz // n_seq, nt),
        in_specs=[wide, wide, narrow(q_col), narrow(q_col + 1), mem, mem,
                  _resident(MIX_DIM, D_MODEL), row, _resident(1, D_MODEL)],
        out_specs=row,
        out_shape=jax.ShapeDtypeStruct((m, D_MODEL), F32),
        scratch_shapes=[pltpu.VMEM((tt, MIX_DIM), BF16), pltpu.VMEM((tt, LANES), F32)],
        compiler_params=_params("parallel", "parallel"),
        name="tail_b",
    )(a, rest, rest, rest, mem_k, mem_v, w_out, x, gain)


def _proj_kernel(h_ref, w_ref, o_ref):
    o_ref[...] = jnp.dot(h_ref[...], w_ref[...].astype(BF16),
                         preferred_element_type=F32).astype(o_ref.dtype)


def _proj(h, w, *, first_col, tm, tn, out_dtype):
    m, d = h.shape
    n = w.shape[1] - first_col
    tm = min(tm, m)
    assert m % tm == 0 and n % tn == 0 and first_col % tn == 0
    return pl.pallas_call(
        _proj_kernel,
        grid=(m // tm, n // tn),
        in_specs=[pl.BlockSpec((tm, d), lambda i, j: (i, 0)),
                  pl.BlockSpec((d, tn), lambda i, j: (0, j + first_col // tn))],
        out_specs=pl.BlockSpec((tm, tn), lambda i, j: (i, j)),
        out_shape=jax.ShapeDtypeStruct((m, n), out_dtype),
        compiler_params=_params("parallel", "arbitrary"),
        name="proj",
    )(h, w)


def _kvf_kernel(h_ref, wkv_ref, wf_ref, bf_ref, c0_ref, *refs,
                seg, tiles_per_seq, tn, emit_vt):
    if emit_vt:
        k_ref, v_ref, logf_ref, c_ref, kx_ref, vt_ref, carry_ref = refs
    else:
        k_ref, v_ref, logf_ref, c_ref, kx_ref, carry_ref = refs
    n_seq, seq_rows = k_ref.shape[0], k_ref.shape[2]
    i = pl.program_id(0)
    tm = h_ref.shape[0]
    h = h_ref[...]

    heads_per_chunk = tn // HEAD_DIM
    n_chunks = F_DIM // tn

    def project(n):
        return jnp.dot(h, wkv_ref[:, n * tn:(n + 1) * tn], preferred_element_type=F32)

    logit = jnp.dot(h, wf_ref[...], preferred_element_type=F32) + bf_ref[...]

    for n in range(n_chunks, 2 * n_chunks):
        y = project(n)
        for hh in range(heads_per_chunk):
            head = (n - n_chunks) * heads_per_chunk + hh
            yh = y[:, hh * HEAD_DIM:(hh + 1) * HEAD_DIM]
            if emit_vt:
                vt_ref[0, head, 0, 0:HEAD_DIM, :] = yh.T.astype(BF16)
                one_row = lax.broadcasted_iota(jnp.int32, (VT_ROWS - HEAD_DIM, tm), 0) == 0
                vt_ref[0, head, 0, HEAD_DIM:VT_ROWS, :] = jnp.where(one_row, 1.0, 0.0).astype(BF16)
            for s in range(n_seq):
                v_ref[s, head] = yh[s * seq_rows:(s + 1) * seq_rows]

    logf = jnp.minimum(logit, 0.0) - jnp.log1p(jnp.exp(-jnp.abs(logit)))
    logf_ref[...] = logf[:, :N_HEADS_F]
    pos = lax.broadcasted_iota(jnp.int32, (tm, LANES), 0) & (seg - 1)
    c = logf
    step = 1
    while step < seg:
        c = c + jnp.where(pos >= step, pltpu.roll(c, step, axis=0), 0.0)
        step *= 2
    if tiles_per_seq > 1:
        @pl.when(i % tiles_per_seq == 0)
        def _():
            carry_ref[...] = c0_ref[0:1, :]

        c = c + carry_ref[...]
        carry_ref[...] = c[tm - 1:tm, :]
    else:
        c = c + c0_ref[...]
    c_ref[...] = c

    c2 = c * LOG2E
    for n in range(n_chunks):
        y = project(n)
        for hh in range(heads_per_chunk):
            head = n * heads_per_chunk + hh
            yh = y[:, hh * HEAD_DIM:(hh + 1) * HEAD_DIM]
            yb = yh.astype(BF16)
            aug = _aug_block(c2[:, head:head + 1], True)
            for s in range(n_seq):
                rows = slice(s * seq_rows, (s + 1) * seq_rows)
                k_ref[s, head] = yh[rows]
                kx_ref[s, head, :, 0:HEAD_DIM] = yb[rows]
                kx_ref[s, head, :, HEAD_DIM:AUG_DIM] = aug[rows]


def _kvf(h, w_kv, w_f, b_f, c0, *, bsz, t_len, tm, emit_vt):
    m, d = h.shape
    tm = min(tm, m)
    seg = min(t_len, tm)
    assert m % tm == 0 and seg & (seg - 1) == 0 and (t_len % tm == 0 or tm % t_len == 0)
    tiles_per_seq = max(t_len // tm, 1)
    n_seq = tm // seg
    row = lambda w: pl.BlockSpec((tm, w), lambda i: (i, 0))
    heads = lambda w: pl.BlockSpec((n_seq, N_HEADS_F, seg, w),
                                   lambda i: (i // tiles_per_seq, 0, i % tiles_per_seq, 0))
    out_specs = [heads(HEAD_DIM), heads(HEAD_DIM), row(N_HEADS_F), row(LANES), heads(AUG_DIM)]
    out_shape = [jax.ShapeDtypeStruct((bsz, N_HEADS_F, t_len, HEAD_DIM), F32),
                 jax.ShapeDtypeStruct((bsz, N_HEADS_F, t_len, HEAD_DIM), F32),
                 jax.ShapeDtypeStruct((m, N_HEADS_F), F32),
                 jax.ShapeDtypeStruct((m, LANES), F32),
                 jax.ShapeDtypeStruct((bsz, N_HEADS_F, t_len, AUG_DIM), BF16)]
    if emit_vt:
        assert tiles_per_seq * tm == t_len
        out_specs.append(pl.BlockSpec(
            (1, N_HEADS_F, 1, VT_ROWS, tm),
            lambda i: (i // tiles_per_seq, 0, i % tiles_per_seq, 0, 0)))
        out_shape.append(
            jax.ShapeDtypeStruct((bsz, N_HEADS_F, tiles_per_seq, VT_ROWS, tm), BF16))
    return pl.pallas_call(
        functools.partial(_kvf_kernel, seg=seg, tiles_per_seq=tiles_per_seq, tn=512,
                          emit_vt=emit_vt),
        grid=(m // tm,),
        in_specs=[row(d), _resident(d, 2 * F_DIM), _resident(d, LANES), _resident(1, LANES),
                  row(LANES)],
        out_specs=out_specs,
        out_shape=out_shape,
        scratch_shapes=[pltpu.VMEM((1, LANES), F32)],
        compiler_params=_params("arbitrary"),
        name="kvf",
    )(h, w_kv, w_f, b_f, c0)


def _qproj_kernel(h_ref, wq_ref, c_ref, qx_ref, *, tn):
    h = h_ref[...]
    c = c_ref[...] * LOG2E
    heads_per_chunk = tn // HEAD_DIM
    for n in range(F_DIM // tn):
        y = jnp.dot(h, wq_ref[:, n * tn:(n + 1) * tn],
                    preferred_element_type=F32) * (SCALE * LOG2E)
        for hh in range(heads_per_chunk):
            head = n * heads_per_chunk + hh
            base = head * AUG_DIM
            qx_ref[:, base:base + HEAD_DIM] = y[:, hh * HEAD_DIM:(hh + 1) * HEAD_DIM].astype(BF16)
            qx_ref[:, base + HEAD_DIM:base + AUG_DIM] = _aug_block(c[:, head:head + 1], False)


def _qproj(h, w_q, c, *, tm):
    m, d = h.shape
    tm = min(tm, m)
    assert m % tm == 0
    return pl.pallas_call(
        functools.partial(_qproj_kernel, tn=512),
        grid=(m // tm,),
        in_specs=[pl.BlockSpec((tm, d), lambda i: (i, 0)),
                  _resident(d, F_DIM),
                  pl.BlockSpec((tm, LANES), lambda i: (i, 0))],
        out_specs=pl.BlockSpec((tm, N_HEADS_F * AUG_DIM), lambda i: (i, 0)),
        out_shape=jax.ShapeDtypeStruct((m, N_HEADS_F * AUG_DIM), BF16),
        compiler_params=_params("parallel"),
        name="qproj",
    )(h, w_q, c)


def _fox_prompt_kernel(qx_ref, kx_ref, vt_ref, o_ref, m_ref, acc_ref, sa_ref, sb_ref, ma_ref, mb_ref,
                       *, cw):
    qi = pl.program_id(2)
    tq = qx_ref.shape[0]
    tk = sa_ref.shape[1]
    tkv = vt_ref.shape[4]
    n_chunks = tq // cw
    buf_a, buf_b = (sa_ref, ma_ref), (sb_ref, mb_ref)
    m_ref[...] = jnp.full_like(m_ref, -jnp.inf)
    acc_ref[...] = jnp.zeros_like(acc_ref)

    def scores(kb, visible, buf):
        s_ref, smax_ref = buf
        for c, (keys, key_minus_query) in visible.items():
            k = kx_ref[0, 0, pl.ds(pl.multiple_of(kb * tk, tk), keys), :]
            q = qx_ref[c * cw:(c + 1) * cw, :]
            s = lax.dot_general(k, q, (((1,), (1,)), ((), ())), preferred_element_type=F32)
            if key_minus_query is not None:
                key = lax.broadcasted_iota(jnp.int32, (keys, cw), 0) + key_minus_query
                query = lax.broadcasted_iota(jnp.int32, (keys, cw), 1)
                s = jnp.where(key <= query, s, -jnp.inf)
            s_ref[c, 0:keys, :] = s
            smax_ref[c] = jnp.max(s, axis=0, keepdims=True)

    def update(kb, visible, buf):
        s_ref, smax_ref = buf
        for c, (keys, _) in visible.items():
            m_old = m_ref[c]
            m_new = jnp.maximum(m_old, smax_ref[c])
            alpha = jnp.exp2(m_old - m_new)
            acc = alpha * acc_ref[c]
            for lo in range(0, keys, tkv):
                n = min(tkv, keys - lo)
                p = jnp.exp2(s_ref[c, lo:lo + n, :] - m_new).astype(BF16)
                acc = acc + jnp.dot(vt_ref[0, 0, kb * (tk // tkv) + lo // tkv, :, 0:n], p,
                                    preferred_element_type=F32)
            acc_ref[c] = acc
            m_ref[c] = m_new

    every = {c: (tk, None) for c in range(n_chunks)}
    own = []
    for jj in range(2):
        visible = {}
        for c in range(n_chunks):
            k_lo, q_lo = jj * tk, c * cw
            keys = min(tk, q_lo + cw - k_lo)
            if keys > 0:
                visible[c] = (keys, k_lo - q_lo if k_lo + keys - 1 > q_lo else None)
        own.append(visible)

    @pl.when(qi == 0)
    def _():
        scores(0, own[0], buf_a)
        scores(1, own[1], buf_b)
        update(0, own[0], buf_a)
        update(1, own[1], buf_b)

    @pl.when(qi > 0)
    def _():
        n = 2 * qi
        scores(0, every, buf_a)

        def pair(j):
            scores(j, every, buf_b)
            update(j - 1, every, buf_a)
            scores(j + 1, every, buf_a)
            update(j, every, buf_b)

        def two_pairs(t, carry):
            pair(4 * t + 1)
            pair(4 * t + 3)
            return carry

        n_pairs = qi - 1
        lax.fori_loop(0, n_pairs // 2, two_pairs, 0)

        @pl.when(n_pairs % 2 == 1)
        def _():
            pair(n - 3)

        scores(n - 1, every, buf_b)
        update(n - 2, every, buf_a)
        scores(n, own[0], buf_a)
        update(n - 1, every, buf_b)
        scores(n + 1, own[1], buf_b)
        update(n, own[0], buf_a)
        update(n + 1, own[1], buf_b)

    for c in range(n_chunks):
        acc = acc_ref[c]
        o = acc[0:HEAD_DIM, :] / acc[HEAD_DIM:HEAD_DIM + 1, :]
        o_ref[c * cw:(c + 1) * cw, :] = o.T.astype(o_ref.dtype)


def _fox_prompt(qx, kx, vt, *, bsz, t_len, tq, cw):
    m = bsz * t_len
    nq = t_len // tq
    tk = tq // 2
    tkv = vt.shape[4]
    n_chunks = tq // cw
    assert t_len % tq == 0 and tk % tkv == 0 and tq % cw == 0
    return pl.pallas_call(
        functools.partial(_fox_prompt_kernel, cw=cw),
        grid=(bsz, N_HEADS_F, nq),
        in_specs=[pl.BlockSpec((tq, AUG_DIM), lambda b, h, qi: (b * nq + qi, h)),
                  pl.BlockSpec((1, 1, t_len, AUG_DIM), lambda b, h, qi: (b, h, 0, 0)),
                  pl.BlockSpec((1, 1, t_len // tkv, VT_ROWS, tkv),
                               lambda b, h, qi: (b, h, 0, 0, 0))],
        out_specs=pl.BlockSpec((tq, HEAD_DIM), lambda b, h, qi: (b * nq + qi, h)),
        out_shape=jax.ShapeDtypeStruct((m, F_DIM), BF16),
        scratch_shapes=[pltpu.VMEM((n_chunks, 1, cw), F32),
                        pltpu.VMEM((n_chunks, VT_ROWS, cw), F32),
                        pltpu.VMEM((n_chunks, tk, cw), F32), pltpu.VMEM((n_chunks, tk, cw), F32),
                        pltpu.VMEM((n_chunks, 1, cw), F32), pltpu.VMEM((n_chunks, 1, cw), F32)],
        compiler_params=_params("parallel", "parallel", "arbitrary"),
        name="fox_prompt",
    )(qx, kx, vt)


def _fox_cached_kernel(qx_ref, kxn_ref, vn_ref, cn_ref, ck_ref, cv_ref, cp_ref, o_ref,
                       m_ref, l_ref, acc_ref, cq_ref, smax_ref, s_ref):
    b = pl.program_id(0)
    ki = pl.program_id(1)
    t_new = qx_ref.shape[0]

    @pl.when(ki == 0)
    def _():
        m_ref[...] = jnp.full_like(m_ref, -jnp.inf)
        l_ref[...] = jnp.zeros_like(l_ref)
        acc_ref[...] = jnp.zeros_like(acc_ref)
        for h in range(N_HEADS_F):
            cq_ref[h] = jnp.broadcast_to(cn_ref[:, h:h + 1] * LOG2E, (t_new, LANES))

    def update(h, s, s_max, v):
        n = s.shape[1]
        across = (lambda x: _tile_lanes(x, n // LANES)) if n >= LANES else (lambda x: x[:, :n])
        m_old = m_ref[h]
        m_new = jnp.maximum(m_old, s_max)
        alpha = jnp.exp2(m_old - m_new)
        p = jnp.exp2(s - across(m_new))
        l_ref[h] = alpha * l_ref[h] + jnp.sum(p, axis=-1, keepdims=True)
        acc_ref[h] = alpha * acc_ref[h] + jnp.dot(p.astype(BF16), v, preferred_element_type=F32)
        m_ref[h] = m_new

    tk = ck_ref.shape[2]
    for h in range(N_HEADS_F):
        q = qx_ref[:, h * AUG_DIM:h * AUG_DIM + HEAD_DIM]
        k = ck_ref[0, h].astype(BF16)
        s = lax.dot_general(q, k, (((1,), (1,)), ((), ())), preferred_element_type=F32)
        decay = _tile_lanes(cq_ref[h], tk // LANES) - cp_ref[h, pl.ds(b, 1), :] * LOG2E
        s = s + decay
        s_ref[h] = s
        smax_ref[h] = jnp.broadcast_to(jnp.max(s, axis=-1, keepdims=True), (t_new, LANES))
    for h in range(N_HEADS_F):
        update(h, s_ref[h], smax_ref[h], cv_ref[0, h].astype(BF16))

    @pl.when(ki == pl.num_programs(1) - 1)
    def _():
        row = lax.broadcasted_iota(jnp.int32, (t_new, t_new), 0)
        col = lax.broadcasted_iota(jnp.int32, (t_new, t_new), 1)
        for h in range(N_HEADS_F):
            xcols = slice(h * AUG_DIM, (h + 1) * AUG_DIM)
            s = lax.dot_general(qx_ref[:, xcols], kxn_ref[0, h], (((1,), (1,)), ((), ())),
                                preferred_element_type=F32)
            s = jnp.where(col <= row, s, -jnp.inf)
            s_ref[h, :, 0:t_new] = s
            smax_ref[h] = jnp.broadcast_to(jnp.max(s, axis=-1, keepdims=True), (t_new, LANES))
        for h in range(N_HEADS_F):
            cols = slice(h * HEAD_DIM, (h + 1) * HEAD_DIM)
            update(h, s_ref[h, :, 0:t_new], smax_ref[h], vn_ref[0, h].astype(BF16))
            o_ref[:, cols] = (acc_ref[h] / l_ref[h]).astype(o_ref.dtype)


def _fox_cached(qx, kx_new, v_new, c_new, cache_k, cache_v, c_past, *, bsz, t_len, tk):
    m = bsz * t_len
    past = cache_k.shape[2]
    assert past % tk == 0 and past > 0
    seq = lambda w: pl.BlockSpec((t_len, w), lambda b, ki: (b, 0))
    new = lambda w: pl.BlockSpec((1, N_HEADS_F, t_len, w), lambda b, ki: (b, 0, 0, 0))
    cache = pl.BlockSpec((1, N_HEADS_F, tk, HEAD_DIM), lambda b, ki: (b, 0, ki, 0))
    return pl.pallas_call(
        _fox_cached_kernel,
        grid=(bsz, past // tk),
        in_specs=[seq(N_HEADS_F * AUG_DIM), new(AUG_DIM), new(HEAD_DIM), seq(LANES),
                  cache, cache,
                  pl.BlockSpec((N_HEADS_F, bsz, tk), lambda b, ki: (0, 0, ki))],
        out_specs=seq(F_DIM),
        out_shape=jax.ShapeDtypeStruct((m, F_DIM), BF16),
        scratch_shapes=[pltpu.VMEM((N_HEADS_F, t_len, LANES), F32) for _ in range(5)]
        + [pltpu.VMEM((N_HEADS_F, t_len, tk), F32)],
        compiler_params=_params("parallel", "arbitrary"),
        name="fox_cached",
    )(qx, kx_new, v_new, c_new, cache_k, cache_v, c_past)


def _cumsum_lanes_kernel(x_ref, o_ref):
    rows, n = x_ref.shape
    r = lax.broadcasted_iota(jnp.int32, (LANES, LANES), 0)
    c = lax.broadcasted_iota(jnp.int32, (LANES, LANES), 1)
    upper = jnp.where(r <= c, 1.0, 0.0).astype(BF16)
    carry = jnp.zeros((rows, 1), F32)
    for j in range(n // LANES):
        cols = slice(j * LANES, (j + 1) * LANES)
        hi, mid, lo = _split3(x_ref[:, cols])
        local = (jnp.dot(hi.astype(BF16), upper, preferred_element_type=F32)
                 + jnp.dot(mid.astype(BF16), upper, preferred_element_type=F32)
                 + jnp.dot(lo.astype(BF16), upper, preferred_element_type=F32))
        o_ref[:, cols] = local + carry
        carry = carry + local[:, LANES - 1:LANES]


def _cumsum_lanes(x):
    rows, n = x.shape
    return pl.pallas_call(
        _cumsum_lanes_kernel,
        grid=(1,),
        in_specs=[pl.BlockSpec((rows, n), lambda i: (0, 0))],
        out_specs=pl.BlockSpec((rows, n), lambda i: (0, 0)),
        out_shape=jax.ShapeDtypeStruct((rows, n), F32),
        compiler_params=_params("arbitrary"),
        name="cumsum_lanes",
    )(x)


class _Tiles(NamedTuple):
    proj_rows: int
    proj_cols: int
    rest_cols: int
    head_rows: int
    tail_rows: int
    attn_queries: int
    attn_chunk: int
    cache_keys: int


def _tiles(t_len):
    tail_rows = 512 if t_len >= 512 else 256
    return _Tiles(proj_rows=1024, proj_cols=1024, rest_cols=512, head_rows=512,
                  tail_rows=tail_rows, attn_queries=1024, attn_chunk=256, cache_keys=1024)


def _trunk(x, conv_state, mem_k, mem_v, past, w):
    bsz, t_len, d = x.shape
    m = bsz * t_len
    x0 = x.reshape(m, d)
    tiles = _tiles(t_len)

    proj = _norm_proj(x0, w["g_norm"][0], w["w_in_a"], tm=tiles.proj_rows, tn=tiles.proj_cols,
                      out_dtype=BF16)
    x1, h1, h_kv, new_state = _tail_a(
        proj, conv_state, mem_k[0], mem_v[0], w["conv_w"], w["w_out"][0], x0,
        jnp.stack([w["g_norm"][1], w["g_kv"]]), bsz=bsz, t_len=t_len, tt=tiles.tail_rows)

    if past is None:
        c0 = jnp.zeros((m, LANES), F32)
    else:
        cache_k, cache_v, cache_logf = past
        past_len = cache_k.shape[1]
        cache_k = jnp.transpose(cache_k, (0, 2, 1, 3))
        cache_v = jnp.transpose(cache_v, (0, 2, 1, 3))
        logf_t = jnp.transpose(cache_logf, (2, 0, 1)).reshape(N_HEADS_F * bsz, past_len)
        c_past = _cumsum_lanes(logf_t).reshape(N_HEADS_F, bsz, past_len)
        c_end = jnp.pad(c_past[:, :, past_len - 1].T, ((0, 0), (0, LANES - N_HEADS_F)))
        c0 = jnp.repeat(c_end, t_len, axis=0)
    kvf_out = _kvf(h_kv, w["w_kv"], w["w_f"], w["b_f"], c0,
                   bsz=bsz, t_len=t_len, tm=tiles.head_rows, emit_vt=past is None)
    k_new, v_new, logf, c, kx = kvf_out[:5]
    qx = _qproj(h1, w["w_q"], c, tm=tiles.head_rows)
    rest = _proj(h1, w["w_in_b"], first_col=F_DIM, tm=tiles.proj_rows, tn=tiles.rest_cols,
                 out_dtype=BF16)

    if past is None:
        o = _fox_prompt(qx, kx, kvf_out[5], bsz=bsz, t_len=t_len, tq=tiles.attn_queries,
                        cw=tiles.attn_chunk)
    else:
        o = _fox_cached(qx, kx, v_new, c, cache_k, cache_v, c_past,
                        bsz=bsz, t_len=t_len, tk=tiles.cache_keys)
    y = _tail_b(o, rest, mem_k[1], mem_v[1], w["w_out"][1], x1, w["g_final"][None],
                bsz=bsz, t_len=t_len, tt=tiles.tail_rows)
    return (y.reshape(bsz, t_len, d), new_state[None],
            jnp.transpose(k_new, (0, 2, 1, 3)), jnp.transpose(v_new, (0, 2, 1, 3)),
            logf.reshape(bsz, t_len, N_HEADS_F))


def kernel(x_prompt, x_sample, state_conv, cache_k, cache_v, cache_logf, cache_mem_k, cache_mem_v,
           mem_prompt, g_norm, w_in_a, conv_w, w_in_b, w_out, g_mem, w_mem_kv, g_kv, w_kvf, b_f,
           g_final):
    depth = g_norm.shape[0]
    bp = x_prompt.shape[0]
    bs = x_sample.shape[0]
    w = {
        "g_norm": g_norm, "g_kv": g_kv, "g_final": g_final, "conv_w": conv_w[0],
        "w_in_a": w_in_a[0],
        "w_q": w_in_b[0][:, :F_DIM].astype(BF16),
        "w_in_b": w_in_b[0],
        "w_out": w_out.astype(BF16),
        "w_kv": w_kvf[:, :2 * F_DIM].astype(BF16),
        "w_f": jnp.pad(w_kvf[:, 2 * F_DIM:], ((0, 0), (0, LANES - N_HEADS_F))).astype(BF16),
        "b_f": jnp.pad(b_f, (0, LANES - N_HEADS_F)).reshape(1, LANES),
    }

    mem_rows = mem_prompt.reshape(bp * N_MEM, D_MODEL)
    mem_kv = [_norm_proj(mem_rows, g_mem[i], w_mem_kv[i], tm=512, tn=512)
              for i in range(depth)]
    p_mem_k = jnp.stack([a[:, :M_DIM] for a in mem_kv]).reshape(depth, bp, N_MEM, N_HEADS_M, HEAD_DIM)
    p_mem_v = jnp.stack([a[:, M_DIM:] for a in mem_kv]).reshape(depth, bp, N_MEM, N_HEADS_M, HEAD_DIM)

    zero_conv = jnp.zeros((bp, CONV_W - 1, CONV_DIM), F32)
    y_p, p_state, p_k, p_v, p_logf = _trunk(
        x_prompt, zero_conv, p_mem_k.reshape(depth, bp, N_MEM, M_DIM),
        p_mem_v.reshape(depth, bp, N_MEM, M_DIM), None, w)
    y_s, s_state, s_k, s_v, s_logf = _trunk(
        x_sample, state_conv[0], cache_mem_k.reshape(depth, bs, N_MEM, M_DIM),
        cache_mem_v.reshape(depth, bs, N_MEM, M_DIM), (cache_k, cache_v, cache_logf), w)
    return (y_p, y_s, p_state, p_k, p_v, p_logf, p_mem_k, p_mem_v, s_state, s_k, s_v, s_logf)
```

```python
import functools
import math
from typing import NamedTuple

import jax
import jax.numpy as jnp
from jax import lax
from jax.experimental import pallas as pl
from jax.experimental.pallas import tpu as pltpu

F32 = jnp.float32
BF16 = jnp.bfloat16

D_MODEL = 2048
CONV_W = 3
CONV_DIM = 1536
HEAD_DIM = 128
N_HEADS_F = 12
F_DIM = N_HEADS_F * HEAD_DIM
N_MEM = 256
N_HEADS_M = 4
M_DIM = N_HEADS_M * HEAD_DIM
MIX_DIM = CONV_DIM + M_DIM
EPS = 1e-6
SCALE = 1.0 / math.sqrt(HEAD_DIM)
LOG2E = math.log2(math.e)

LANES = 128
AUG_DIM = 2 * HEAD_DIM
VT_ROWS = HEAD_DIM + 16
VMEM_LIMIT = 56 * 1024 * 1024
NORM_ROWS = 64


def _params(*sem):
    return pltpu.CompilerParams(dimension_semantics=sem, vmem_limit_bytes=VMEM_LIMIT)


def _resident(*shape):
    return pl.BlockSpec(shape, lambda *_: (0,) * len(shape), pipeline_mode=pl.Buffered(1))


def _tile_lanes(x, n):
    return jnp.concatenate([x] * n, axis=1)


def _silu(z):
    return z * (1.0 / (1.0 + jnp.exp(-z)))


def _split3(c):
    hi = c.astype(BF16).astype(F32)
    r = c - hi
    mid = r.astype(BF16).astype(F32)
    lo = (r - mid).astype(BF16).astype(F32)
    return hi, mid, lo


def _aug_block(c_col, key_side):
    rows = c_col.shape[0]
    hi, mid, lo = _split3(-c_col if key_side else c_col)
    lane = lax.broadcasted_iota(jnp.int32, (rows, LANES), 1)
    term0 = 3 if key_side else 0
    one0 = 0 if key_side else 3
    terms = jnp.where(lane == term0, hi,
                      jnp.where(lane == term0 + 1, mid, jnp.where(lane == term0 + 2, lo, 0.0)))
    return jnp.where((lane >= one0) & (lane < one0 + 3), 1.0, terms).astype(BF16)


def _rmsnorm_rows(x_ref, g_ref, h_ref, rows_per_chunk=NORM_ROWS):
    tm = x_ref.shape[0]
    g = g_ref[...]

    def body(r, carry):
        rows = pl.ds(pl.multiple_of(r * rows_per_chunk, rows_per_chunk), rows_per_chunk)
        x = x_ref[rows, :]
        ms = jnp.mean(x * x, axis=-1, keepdims=True)
        h_ref[rows, :] = (x * lax.rsqrt(ms + EPS) * g).astype(h_ref.dtype)
        return carry

    lax.fori_loop(0, tm // rows_per_chunk, body, 0)


def _norm_proj_kernel(x_ref, g_ref, w_ref, o_ref, h_ref):
    @pl.when(pl.program_id(1) == 0)
    def _():
        _rmsnorm_rows(x_ref, g_ref, h_ref)

    o_ref[...] = jnp.dot(h_ref[...], w_ref[...].astype(BF16),
                         preferred_element_type=F32).astype(o_ref.dtype)


def _norm_proj(x, g, w, *, tm, tn, out_dtype=F32):
    m, d = x.shape
    n = w.shape[1]
    tm = min(tm, m)
    assert m % tm == 0 and n % tn == 0
    return pl.pallas_call(
        _norm_proj_kernel,
        grid=(m // tm, n // tn),
        in_specs=[
            pl.BlockSpec((tm, d), lambda i, j: (i, 0)),
            pl.BlockSpec((1, d), lambda i, j: (0, 0)),
            pl.BlockSpec((d, tn), lambda i, j: (0, j)),
        ],
        out_specs=pl.BlockSpec((tm, tn), lambda i, j: (i, j)),
        out_shape=jax.ShapeDtypeStruct((m, n), out_dtype),
        scratch_shapes=[pltpu.VMEM((tm, d), BF16)],
        compiler_params=_params("parallel", "arbitrary"),
        name="norm_proj",
    )(x, g.reshape(1, d), w)


K_CHUNK = 512


def _memory_attention_into(qm_ref, zm_ref, mk_ref, mv_ref, mixed_ref, n_seq, seq_rows, heads):
    for s in range(n_seq):
        rows = slice(s * seq_rows, (s + 1) * seq_rows)
        for h in heads:
            cols = slice(h * HEAD_DIM, (h + 1) * HEAD_DIM)
            q = (qm_ref[rows, cols].astype(F32) * SCALE).astype(BF16)
            k = mk_ref[s, :, cols].astype(BF16)
            v = mv_ref[s, :, cols].astype(BF16)
            sc = lax.dot_general(q, k, (((1,), (1,)), ((), ())), preferred_element_type=F32)
            p = jnp.exp(sc - jnp.max(sc, axis=-1, keepdims=True))
            l = jnp.sum(p, axis=-1, keepdims=True)
            o = jnp.dot(p.astype(BF16), v, preferred_element_type=F32) / l
            out_cols = slice(CONV_DIM + h * HEAD_DIM, CONV_DIM + (h + 1) * HEAD_DIM)
            mixed_ref[rows, out_cols] = (o * _silu(zm_ref[rows, cols].astype(F32))).astype(
                mixed_ref.dtype)


def _memory_branch(qm_ref, zm_ref, mk_ref, mv_ref, mixed_ref, w_ref, x_ref, o_ref, n_seq, seq_rows):
    heads_per_chunk = K_CHUNK // HEAD_DIM
    for kc in range(CONV_DIM // K_CHUNK, MIX_DIM // K_CHUNK):
        first = (kc * K_CHUNK - CONV_DIM) // HEAD_DIM
        _memory_attention_into(qm_ref, zm_ref, mk_ref, mv_ref, mixed_ref, n_seq, seq_rows,
                               range(first, first + heads_per_chunk))
        _project_chunk(mixed_ref, w_ref, x_ref, o_ref, kc)


def _project_chunk(mixed_ref, w_ref, x_ref, o_ref, kc, tn=512):
    krows = slice(kc * K_CHUNK, (kc + 1) * K_CHUNK)
    a = mixed_ref[:, krows]
    for c in range(o_ref.shape[1] // tn):
        cols = slice(c * tn, (c + 1) * tn)
        base = x_ref[:, cols] if kc == 0 else o_ref[:, cols]
        o_ref[:, cols] = base + jnp.dot(a, w_ref[krows, cols], preferred_element_type=F32)


def _finish_rows(o_ref, g_ref, h_refs, inv_ref, final_norm, rows_per_chunk=NORM_ROWS):
    n_chunks = o_ref.shape[0] // rows_per_chunk
    chunk = lambda r: pl.ds(pl.multiple_of(r * rows_per_chunk, rows_per_chunk), rows_per_chunk)

    def stats(r, carry):
        x = o_ref[chunk(r), :]
        inv = lax.rsqrt(jnp.mean(x * x, axis=-1, keepdims=True) + EPS)
        inv_ref[chunk(r), :] = jnp.broadcast_to(inv, (rows_per_chunk, LANES))
        return carry

    def scale(r, carry):
        xn = o_ref[chunk(r), :] * _tile_lanes(inv_ref[chunk(r), :], o_ref.shape[1] // LANES)
        if final_norm:
            o_ref[chunk(r), :] = xn * g_ref[0:1, :]
        for k, h_ref in enumerate(h_refs):
            h_ref[chunk(r), :] = (xn * g_ref[k:k + 1, :]).astype(h_ref.dtype)
        return carry

    lax.fori_loop(0, n_chunks, stats, 0, unroll=True)
    lax.fori_loop(0, n_chunks, scale, 0, unroll=True)


def _layer_memory(layer, n_seq):
    return pl.BlockSpec((None, n_seq, N_MEM, M_DIM), lambda b, t: (layer, b, 0, 0))


def _tail_a_kernel(bg_ref, cg_ref, u_ref, zc_ref, cgh_ref, uh_ref, st_ref, qm_ref, zm_ref,
                   mk_ref, mv_ref, cw_ref, w_ref, x_ref, g_ref,
                   o_ref, h1_ref, h2_ref, nst_ref, mixed_ref, inv_ref):
    t = pl.program_id(1)
    n_seq, seq_rows = st_ref.shape[0], bg_ref.shape[0] // st_ref.shape[0]
    first = t == 0
    row = lax.broadcasted_iota(jnp.int32, (seq_rows, LANES), 0)
    hr = cgh_ref.shape[0]
    for c in range(CONV_DIM // LANES):
        cols = slice(c * LANES, (c + 1) * LANES)
        w = cw_ref[:, cols]
        halo = cgh_ref[:, cols].astype(F32) * uh_ref[:, cols].astype(F32)
        for s in range(n_seq):
            rows = slice(s * seq_rows, (s + 1) * seq_rows)
            ci = cg_ref[rows, cols].astype(F32) * u_ref[rows, cols].astype(F32)
            st = st_ref[s, :, cols]
            prev1 = jnp.where(first, st[1:2, :], halo[hr - 1:hr, :])
            prev2 = jnp.where(first, st[0:1, :], halo[hr - 2:hr - 1, :])
            s1 = jnp.where(row == 0, prev1, pltpu.roll(ci, 1, axis=0))
            s2 = jnp.where(row == 0, prev2, jnp.where(row == 1, prev1, pltpu.roll(ci, 2, axis=0)))
            conv = w[0:1, :] * s2 + w[1:2, :] * s1 + w[2:3, :] * ci
            branch = bg_ref[rows, cols].astype(F32) * conv * _silu(zc_ref[rows, cols].astype(F32))
            mixed_ref[rows, cols] = branch.astype(mixed_ref.dtype)
            nst_ref[s, :, cols] = ci[seq_rows - 2:seq_rows, :]

        if (c + 1) * LANES % K_CHUNK == 0:
            _project_chunk(mixed_ref, w_ref, x_ref, o_ref, (c + 1) * LANES // K_CHUNK - 1)

    _memory_branch(qm_ref, zm_ref, mk_ref, mv_ref, mixed_ref, w_ref, x_ref, o_ref, n_seq, seq_rows)
    _finish_rows(o_ref, g_ref, (h1_ref, h2_ref), inv_ref, False)


def _tail_a(proj, state, mem_k, mem_v, conv_w, w_out, x, gains, *, layer, bsz, t_len, tt):
    m = bsz * t_len
    seq_rows = min(tt, t_len)
    n_seq = tt // seq_rows
    nt = t_len // seq_rows
    hr = 16
    assert t_len % seq_rows == 0 and bsz % n_seq == 0 and seq_rows % hr == 0
    tile = lambda b, t: b * nt + t
    wide = lambda k: pl.BlockSpec((tt, CONV_DIM), lambda b, t: (tile(b, t), k))
    halo = lambda k: pl.BlockSpec(
        (hr, CONV_DIM), lambda b, t: (jnp.maximum(tile(b, t) * (tt // hr) - 1, 0), k))
    narrow = lambda k: pl.BlockSpec((tt, M_DIM), lambda b, t: (tile(b, t), k))
    per_seq = lambda *shape: pl.BlockSpec((n_seq,) + shape, lambda b, t: (b,) + (0,) * len(shape))
    row = pl.BlockSpec((tt, D_MODEL), lambda b, t: (tile(b, t), 0))
    q_col = 4 * CONV_DIM // M_DIM
    return pl.pallas_call(
        _tail_a_kernel,
        grid=(bsz // n_seq, nt),
        in_specs=[wide(0), wide(1), wide(2), wide(3), halo(1), halo(2),
                  per_seq(CONV_W - 1, CONV_DIM), narrow(q_col), narrow(q_col + 1),
                  _layer_memory(layer, n_seq), _layer_memory(layer, n_seq),
                  _resident(CONV_W, CONV_DIM),
                  _resident(MIX_DIM, D_MODEL), row, _resident(2, D_MODEL)],
        out_specs=[row, row, row, per_seq(CONV_W - 1, CONV_DIM)],
        out_shape=[jax.ShapeDtypeStruct((m, D_MODEL), F32),
                   jax.ShapeDtypeStruct((m, D_MODEL), BF16),
                   jax.ShapeDtypeStruct((m, D_MODEL), BF16),
                   jax.ShapeDtypeStruct((bsz, CONV_W - 1, CONV_DIM), F32)],
        scratch_shapes=[pltpu.VMEM((tt, MIX_DIM), BF16), pltpu.VMEM((tt, LANES), F32)],
        compiler_params=_params("parallel", "arbitrary"),
        name="tail_a",
    )(proj, proj, proj, proj, proj, proj, state, proj, proj, mem_k, mem_v, conv_w, w_out, x, gains)


def _tail_b_kernel(a_ref, zf_ref, qm_ref, zm_ref, mk_ref, mv_ref, w_ref, x_ref, g_ref,
                   o_ref, mixed_ref, inv_ref):
    n_seq = mk_ref.shape[0]
    seq_rows = a_ref.shape[0] // n_seq
    for c in range(F_DIM // LANES):
        cols = slice(c * LANES, (c + 1) * LANES)
        mixed_ref[:, cols] = (a_ref[:, cols].astype(F32)
                              * _silu(zf_ref[:, cols].astype(F32))).astype(mixed_ref.dtype)
        if (c + 1) * LANES % K_CHUNK == 0:
            _project_chunk(mixed_ref, w_ref, x_ref, o_ref, (c + 1) * LANES // K_CHUNK - 1)
    _memory_branch(qm_ref, zm_ref, mk_ref, mv_ref, mixed_ref, w_ref, x_ref, o_ref, n_seq, seq_rows)
    _finish_rows(o_ref, g_ref, (), inv_ref, True)


def _tail_b(a, rest, mem_k, mem_v, w_out, x, gain, *, layer, bsz, t_len, tt):
    m = bsz * t_len
    seq_rows = min(tt, t_len)
    n_seq = tt // seq_rows
    nt = t_len // seq_rows
    assert t_len % seq_rows == 0 and bsz % n_seq == 0
    tile = lambda b, t: b * nt + t
    narrow = lambda k: pl.BlockSpec((tt, M_DIM), lambda b, t: (tile(b, t), k))
    mem = _layer_memory(layer, n_seq)
    wide = pl.BlockSpec((tt, F_DIM), lambda b, t: (tile(b, t), 0))
    row = pl.BlockSpec((tt, D_MODEL), lambda b, t: (tile(b, t), 0))
    q_col = F_DIM // M_DIM
    return pl.pallas_call(
        _tail_b_kernel,
        grid=(bsz // n_seq, nt),
        in_specs=[wide, wide, narrow(q_col), narrow(q_col + 1), mem, mem,
                  _resident(MIX_DIM, D_MODEL), row, _resident(1, D_MODEL)],
        out_specs=row,
        out_shape=jax.ShapeDtypeStruct((m, D_MODEL), F32),
        scratch_shapes=[pltpu.VMEM((tt, MIX_DIM), BF16), pltpu.VMEM((tt, LANES), F32)],
        compiler_params=_params("parallel", "parallel"),
        name="tail_b",
    )(a, rest, rest, rest, mem_k, mem_v, w_out, x, gain)


def _proj_kernel(h_ref, w_ref, o_ref):
    o_ref[...] = jnp.dot(h_ref[...], w_ref[...].astype(BF16),
                         preferred_element_type=F32).astype(o_ref.dtype)


def _proj(h, w, *, first_col, tm, tn, out_dtype):
    m, d = h.shape
    n = w.shape[1] - first_col
    tm = min(tm, m)
    assert m % tm == 0 and n % tn == 0 and first_col % tn == 0
    return pl.pallas_call(
        _proj_kernel,
        grid=(m // tm, n // tn),
        in_specs=[pl.BlockSpec((tm, d), lambda i, j: (i, 0)),
                  pl.BlockSpec((d, tn), lambda i, j: (0, j + first_col // tn))],
        out_specs=pl.BlockSpec((tm, tn), lambda i, j: (i, j)),
        out_shape=jax.ShapeDtypeStruct((m, n), out_dtype),
        compiler_params=_params("parallel", "arbitrary"),
        name="proj",
    )(h, w)


def _kvf_kernel(h_ref, wkv_ref, wf_ref, bf_ref, c0_ref, *refs,
                seg, tiles_per_seq, tn, emit_vt):
    if emit_vt:
        k_ref, v_ref, logf_ref, c_ref, kx_ref, vt_ref, carry_ref = refs
    else:
        k_ref, v_ref, logf_ref, c_ref, kx_ref, carry_ref = refs
    n_seq, seq_rows = k_ref.shape[0], k_ref.shape[2]
    i = pl.program_id(0)
    tm = h_ref.shape[0]
    h = h_ref[...]

    heads_per_chunk = tn // HEAD_DIM
    n_chunks = F_DIM // tn
    nt_dims = (((1,), (1,)), ((), ()))

    def project(n):
        return lax.dot_general(h, wkv_ref[n * tn:(n + 1) * tn, :], nt_dims,
                               preferred_element_type=F32)

    logit = lax.dot_general(h, wf_ref[...], nt_dims, preferred_element_type=F32) + bf_ref[...]

    for n in range(n_chunks, 2 * n_chunks):
        y = project(n)
        for hh in range(heads_per_chunk):
            head = (n - n_chunks) * heads_per_chunk + hh
            yh = y[:, hh * HEAD_DIM:(hh + 1) * HEAD_DIM]
            if emit_vt:
                vt_ref[0, head, 0, 0:HEAD_DIM, :] = yh.T.astype(BF16)
                one_row = lax.broadcasted_iota(jnp.int32, (VT_ROWS - HEAD_DIM, tm), 0) == 0
                vt_ref[0, head, 0, HEAD_DIM:VT_ROWS, :] = jnp.where(one_row, 1.0, 0.0).astype(BF16)
            for s in range(n_seq):
                v_ref[s, head] = yh[s * seq_rows:(s + 1) * seq_rows]

    logf = jnp.minimum(logit, 0.0) - jnp.log1p(jnp.exp(-jnp.abs(logit)))
    logf_ref[...] = logf[:, :N_HEADS_F]
    pos = lax.broadcasted_iota(jnp.int32, (tm, LANES), 0) & (seg - 1)
    c = logf
    step = 1
    while step < seg:
        c = c + jnp.where(pos >= step, pltpu.roll(c, step, axis=0), 0.0)
        step *= 2
    if tiles_per_seq > 1:
        @pl.when(i % tiles_per_seq == 0)
        def _():
            carry_ref[...] = c0_ref[0:1, :]

        c = c + carry_ref[...]
        carry_ref[...] = c[tm - 1:tm, :]
    else:
        c = c + c0_ref[...]
    c_ref[...] = c

    c2 = c * LOG2E
    for n in range(n_chunks):
        y = project(n)
        for hh in range(heads_per_chunk):
            head = n * heads_per_chunk + hh
            yh = y[:, hh * HEAD_DIM:(hh + 1) * HEAD_DIM]
            yb = yh.astype(BF16)
            aug = _aug_block(c2[:, head:head + 1], True)
            for s in range(n_seq):
                rows = slice(s * seq_rows, (s + 1) * seq_rows)
                k_ref[s, head] = yh[rows]
                kx_ref[s, head, :, 0:HEAD_DIM] = yb[rows]
                kx_ref[s, head, :, HEAD_DIM:AUG_DIM] = aug[rows]


def _kvf(h, w_kv, w_f, b_f, c0, *, bsz, t_len, tm, emit_vt):
    m, d = h.shape
    tm = min(tm, m)
    seg = min(t_len, tm)
    assert m % tm == 0 and seg & (seg - 1) == 0 and (t_len % tm == 0 or tm % t_len == 0)
    tiles_per_seq = max(t_len // tm, 1)
    n_seq = tm // seg
    row = lambda w: pl.BlockSpec((tm, w), lambda i: (i, 0))
    heads = lambda w: pl.BlockSpec((n_seq, N_HEADS_F, seg, w),
                                   lambda i: (i // tiles_per_seq, 0, i % tiles_per_seq, 0))
    out_specs = [heads(HEAD_DIM), heads(HEAD_DIM), row(N_HEADS_F), row(LANES), heads(AUG_DIM)]
    out_shape = [jax.ShapeDtypeStruct((bsz, N_HEADS_F, t_len, HEAD_DIM), F32),
                 jax.ShapeDtypeStruct((bsz, N_HEADS_F, t_len, HEAD_DIM), F32),
                 jax.ShapeDtypeStruct((m, N_HEADS_F), F32),
                 jax.ShapeDtypeStruct((m, LANES), F32),
                 jax.ShapeDtypeStruct((bsz, N_HEADS_F, t_len, AUG_DIM), BF16)]
    if emit_vt:
        assert tiles_per_seq * tm == t_len
        out_specs.append(pl.BlockSpec(
            (1, N_HEADS_F, 1, VT_ROWS, tm),
            lambda i: (i // tiles_per_seq, 0, i % tiles_per_seq, 0, 0)))
        out_shape.append(
            jax.ShapeDtypeStruct((bsz, N_HEADS_F, tiles_per_seq, VT_ROWS, tm), BF16))
    return pl.pallas_call(
        functools.partial(_kvf_kernel, seg=seg, tiles_per_seq=tiles_per_seq, tn=512,
                          emit_vt=emit_vt),
        grid=(m // tm,),
        in_specs=[row(d), _resident(2 * F_DIM, d), _resident(LANES, d), _resident(1, LANES),
                  row(LANES)],
        out_specs=out_specs,
        out_shape=out_shape,
        scratch_shapes=[pltpu.VMEM((1, LANES), F32)],
        compiler_params=_params("arbitrary"),
        name="kvf",
    )(h, w_kv, w_f, b_f, c0)


def _qproj_kernel(h_ref, wq_ref, c_ref, qx_ref, *, tn):
    h = h_ref[...]
    c = c_ref[...] * LOG2E
    heads_per_chunk = tn // HEAD_DIM
    for n in range(F_DIM // tn):
        y = jnp.dot(h, wq_ref[:, n * tn:(n + 1) * tn],
                    preferred_element_type=F32) * (SCALE * LOG2E)
        for hh in range(heads_per_chunk):
            head = n * heads_per_chunk + hh
            base = head * AUG_DIM
            qx_ref[:, base:base + HEAD_DIM] = y[:, hh * HEAD_DIM:(hh + 1) * HEAD_DIM].astype(BF16)
            qx_ref[:, base + HEAD_DIM:base + AUG_DIM] = _aug_block(c[:, head:head + 1], False)


def _qproj(h, w_q, c, *, tm):
    m, d = h.shape
    tm = min(tm, m)
    assert m % tm == 0
    return pl.pallas_call(
        functools.partial(_qproj_kernel, tn=512),
        grid=(m // tm,),
        in_specs=[pl.BlockSpec((tm, d), lambda i: (i, 0)),
                  _resident(d, F_DIM),
                  pl.BlockSpec((tm, LANES), lambda i: (i, 0))],
        out_specs=pl.BlockSpec((tm, N_HEADS_F * AUG_DIM), lambda i: (i, 0)),
        out_shape=jax.ShapeDtypeStruct((m, N_HEADS_F * AUG_DIM), BF16),
        compiler_params=_params("parallel"),
        name="qproj",
    )(h, w_q, c)


def _fox_prompt_kernel(qx_ref, kx_ref, vt_ref, o_ref, m_ref, acc_ref, sa_ref, sb_ref, ma_ref, mb_ref,
                       *, cw):
    qi = pl.program_id(2)
    tq = qx_ref.shape[0]
    tk = sa_ref.shape[1]
    tkv = vt_ref.shape[4]
    n_chunks = tq // cw
    buf_a, buf_b = (sa_ref, ma_ref), (sb_ref, mb_ref)
    m_ref[...] = jnp.full_like(m_ref, -jnp.inf)
    acc_ref[...] = jnp.zeros_like(acc_ref)

    def scores(kb, visible, buf):
        s_ref, smax_ref = buf
        for c, (keys, key_minus_query) in visible.items():
            k = kx_ref[0, 0, pl.ds(pl.multiple_of(kb * tk, tk), keys), :]
            q = qx_ref[c * cw:(c + 1) * cw, :]
            s = lax.dot_general(k, q, (((1,), (1,)), ((), ())), preferred_element_type=F32)
            if key_minus_query is not None:
                key = lax.broadcasted_iota(jnp.int32, (keys, cw), 0) + key_minus_query
                query = lax.broadcasted_iota(jnp.int32, (keys, cw), 1)
                s = jnp.where(key <= query, s, -jnp.inf)
            s_ref[c, 0:keys, :] = s
            smax_ref[c] = jnp.max(s, axis=0, keepdims=True)

    def update(kb, visible, buf):
        s_ref, smax_ref = buf
        for c, (keys, _) in visible.items():
            m_old = m_ref[c]
            m_new = jnp.maximum(m_old, smax_ref[c])
            alpha = jnp.exp2(m_old - m_new)
            acc = alpha * acc_ref[c]
            for lo in range(0, keys, tkv):
                n = min(tkv, keys - lo)
                p = jnp.exp2(s_ref[c, lo:lo + n, :] - m_new).astype(BF16)
                acc = acc + jnp.dot(vt_ref[0, 0, kb * (tk // tkv) + lo // tkv, :, 0:n], p,
                                    preferred_element_type=F32)
            acc_ref[c] = acc
            m_ref[c] = m_new

    every = {c: (tk, None) for c in range(n_chunks)}
    own = []
    for jj in range(2):
        visible = {}
        for c in range(n_chunks):
            k_lo, q_lo = jj * tk, c * cw
            keys = min(tk, q_lo + cw - k_lo)
            if keys > 0:
                visible[c] = (keys, k_lo - q_lo if k_lo + keys - 1 > q_lo else None)
        own.append(visible)

    @pl.when(qi == 0)
    def _():
        scores(0, own[0], buf_a)
        scores(1, own[1], buf_b)
        update(0, own[0], buf_a)
        update(1, own[1], buf_b)

    @pl.when(qi > 0)
    def _():
        n = 2 * qi
        scores(0, every, buf_a)

        def pair(j):
            scores(j, every, buf_b)
            update(j - 1, every, buf_a)
            scores(j + 1, every, buf_a)
            update(j, every, buf_b)

        def two_pairs(t, carry):
            pair(4 * t + 1)
            pair(4 * t + 3)
            return carry

        n_pairs = qi - 1
        lax.fori_loop(0, n_pairs // 2, two_pairs, 0)

        @pl.when(n_pairs % 2 == 1)
        def _():
            pair(n - 3)

        scores(n - 1, every, buf_b)
        update(n - 2, every, buf_a)
        scores(n, own[0], buf_a)
        update(n - 1, every, buf_b)
        scores(n + 1, own[1], buf_b)
        update(n, own[0], buf_a)
        update(n + 1, own[1], buf_b)

    for c in range(n_chunks):
        acc = acc_ref[c]
        o = acc[0:HEAD_DIM, :] / acc[HEAD_DIM:HEAD_DIM + 1, :]
        o_ref[c * cw:(c + 1) * cw, :] = o.T.astype(o_ref.dtype)


def _fox_prompt(qx, kx, vt, *, bsz, t_len, tq, cw):
    m = bsz * t_len
    nq = t_len // tq
    tk = tq // 2
    tkv = vt.shape[4]
    n_chunks = tq // cw
    assert t_len % tq == 0 and tk % tkv == 0 and tq % cw == 0
    return pl.pallas_call(
        functools.partial(_fox_prompt_kernel, cw=cw),
        grid=(bsz, N_HEADS_F, nq),
        in_specs=[pl.BlockSpec((tq, AUG_DIM), lambda b, h, qi: (b * nq + qi, h)),
                  pl.BlockSpec((1, 1, t_len, AUG_DIM), lambda b, h, qi: (b, h, 0, 0)),
                  pl.BlockSpec((1, 1, t_len // tkv, VT_ROWS, tkv),
                               lambda b, h, qi: (b, h, 0, 0, 0))],
        out_specs=pl.BlockSpec((tq, HEAD_DIM), lambda b, h, qi: (b * nq + qi, h)),
        out_shape=jax.ShapeDtypeStruct((m, F_DIM), BF16),
        scratch_shapes=[pltpu.VMEM((n_chunks, 1, cw), F32),
                        pltpu.VMEM((n_chunks, VT_ROWS, cw), F32),
                        pltpu.VMEM((n_chunks, tk, cw), F32), pltpu.VMEM((n_chunks, tk, cw), F32),
                        pltpu.VMEM((n_chunks, 1, cw), F32), pltpu.VMEM((n_chunks, 1, cw), F32)],
        compiler_params=_params("parallel", "parallel", "arbitrary"),
        name="fox_prompt",
    )(qx, kx, vt)


def _fox_cached_kernel(qx_ref, kxn_ref, vn_ref, cn_ref, ck_ref, cv_ref, cp_ref, o_ref,
                       m_ref, l_ref, acc_ref, cq_ref, smax_ref, s_ref):
    b = pl.program_id(0)
    ki = pl.program_id(1)
    t_new = qx_ref.shape[0]

    @pl.when(ki == 0)
    def _():
        m_ref[...] = jnp.full_like(m_ref, -jnp.inf)
        l_ref[...] = jnp.zeros_like(l_ref)
        acc_ref[...] = jnp.zeros_like(acc_ref)
        for h in range(N_HEADS_F):
            cq_ref[h] = jnp.broadcast_to(cn_ref[:, h:h + 1] * LOG2E, (t_new, LANES))

    def update(h, s, s_max, v):
        n = s.shape[1]
        across = (lambda x: _tile_lanes(x, n // LANES)) if n >= LANES else (lambda x: x[:, :n])
        m_old = m_ref[h]
        m_new = jnp.maximum(m_old, s_max)
        alpha = jnp.exp2(m_old - m_new)
        p = jnp.exp2(s - across(m_new))
        l_ref[h] = alpha * l_ref[h] + jnp.sum(p, axis=-1, keepdims=True)
        acc_ref[h] = alpha * acc_ref[h] + jnp.dot(p.astype(BF16), v, preferred_element_type=F32)
        m_ref[h] = m_new

    tk = ck_ref.shape[2]
    for h in range(N_HEADS_F):
        q = qx_ref[:, h * AUG_DIM:h * AUG_DIM + HEAD_DIM]
        k = ck_ref[0, h].astype(BF16)
        s = lax.dot_general(q, k, (((1,), (1,)), ((), ())), preferred_element_type=F32)
        decay = _tile_lanes(cq_ref[h], tk // LANES) - cp_ref[h, pl.ds(b, 1), :] * LOG2E
        s = s + decay
        s_ref[h] = s
        smax_ref[h] = jnp.broadcast_to(jnp.max(s, axis=-1, keepdims=True), (t_new, LANES))
    for h in range(N_HEADS_F):
        update(h, s_ref[h], smax_ref[h], cv_ref[0, h].astype(BF16))

    @pl.when(ki == pl.num_programs(1) - 1)
    def _():
        row = lax.broadcasted_iota(jnp.int32, (t_new, t_new), 0)
        col = lax.broadcasted_iota(jnp.int32, (t_new, t_new), 1)
        for h in range(N_HEADS_F):
            xcols = slice(h * AUG_DIM, (h + 1) * AUG_DIM)
            s = lax.dot_general(qx_ref[:, xcols], kxn_ref[0, h], (((1,), (1,)), ((), ())),
                                preferred_element_type=F32)
            s = jnp.where(col <= row, s, -jnp.inf)
            s_ref[h, :, 0:t_new] = s
            smax_ref[h] = jnp.broadcast_to(jnp.max(s, axis=-1, keepdims=True), (t_new, LANES))
        for h in range(N_HEADS_F):
            cols = slice(h * HEAD_DIM, (h + 1) * HEAD_DIM)
            update(h, s_ref[h, :, 0:t_new], smax_ref[h], vn_ref[0, h].astype(BF16))
            o_ref[:, cols] = (acc_ref[h] / l_ref[h]).astype(o_ref.dtype)


def _fox_cached(qx, kx_new, v_new, c_new, cache_k, cache_v, c_past, *, bsz, t_len, tk):
    m = bsz * t_len
    past = cache_k.shape[2]
    assert past % tk == 0 and past > 0
    seq = lambda w: pl.BlockSpec((t_len, w), lambda b, ki: (b, 0))
    new = lambda w: pl.BlockSpec((1, N_HEADS_F, t_len, w), lambda b, ki: (b, 0, 0, 0))
    cache = pl.BlockSpec((1, N_HEADS_F, tk, HEAD_DIM), lambda b, ki: (b, 0, ki, 0))
    return pl.pallas_call(
        _fox_cached_kernel,
        grid=(bsz, past // tk),
        in_specs=[seq(N_HEADS_F * AUG_DIM), new(AUG_DIM), new(HEAD_DIM), seq(LANES),
                  cache, cache,
                  pl.BlockSpec((N_HEADS_F, bsz, tk), lambda b, ki: (0, 0, ki))],
        out_specs=seq(F_DIM),
        out_shape=jax.ShapeDtypeStruct((m, F_DIM), BF16),
        scratch_shapes=[pltpu.VMEM((N_HEADS_F, t_len, LANES), F32) for _ in range(5)]
        + [pltpu.VMEM((N_HEADS_F, t_len, tk), F32)],
        compiler_params=_params("parallel", "arbitrary"),
        name="fox_cached",
    )(qx, kx_new, v_new, c_new, cache_k, cache_v, c_past)


def _cumsum_lanes_kernel(x_ref, o_ref):
    rows, n = x_ref.shape
    r = lax.broadcasted_iota(jnp.int32, (LANES, LANES), 0)
    c = lax.broadcasted_iota(jnp.int32, (LANES, LANES), 1)
    upper = jnp.where(r <= c, 1.0, 0.0).astype(BF16)
    carry = jnp.zeros((rows, 1), F32)
    for j in range(n // LANES):
        cols = slice(j * LANES, (j + 1) * LANES)
        hi, mid, lo = _split3(x_ref[:, cols])
        local = (jnp.dot(hi.astype(BF16), upper, preferred_element_type=F32)
                 + jnp.dot(mid.astype(BF16), upper, preferred_element_type=F32)
                 + jnp.dot(lo.astype(BF16), upper, preferred_element_type=F32))
        o_ref[:, cols] = local + carry
        carry = carry + local[:, LANES - 1:LANES]


def _cumsum_lanes(x):
    rows, n = x.shape
    return pl.pallas_call(
        _cumsum_lanes_kernel,
        grid=(1,),
        in_specs=[pl.BlockSpec((rows, n), lambda i: (0, 0))],
        out_specs=pl.BlockSpec((rows, n), lambda i: (0, 0)),
        out_shape=jax.ShapeDtypeStruct((rows, n), F32),
        compiler_params=_params("arbitrary"),
        name="cumsum_lanes",
    )(x)


class _Tiles(NamedTuple):
    proj_rows: int
    proj_cols: int
    rest_cols: int
    head_rows: int
    tail_rows: int
    attn_queries: int
    attn_chunk: int
    cache_keys: int


def _tiles(t_len):
    tail_rows = 512 if t_len >= 512 else 256
    return _Tiles(proj_rows=1024, proj_cols=1024, rest_cols=512, head_rows=512,
                  tail_rows=tail_rows, attn_queries=1024, attn_chunk=256, cache_keys=1024)


def _trunk(x, conv_state, mem_k, mem_v, past, w):
    bsz, t_len, d = x.shape
    m = bsz * t_len
    x0 = x.reshape(m, d)
    tiles = _tiles(t_len)

    proj = _norm_proj(x0, w["g_norm"][0], w["w_in_a"], tm=tiles.proj_rows, tn=tiles.proj_cols,
                      out_dtype=BF16)
    x1, h1, h_kv, new_state = _tail_a(
        proj, conv_state, mem_k, mem_v, w["conv_w"], w["w_out"][0], x0,
        jnp.stack([w["g_norm"][1], w["g_kv"]]), layer=0, bsz=bsz, t_len=t_len,
        tt=tiles.tail_rows)

    if past is None:
        c0 = jnp.zeros((m, LANES), F32)
    else:
        cache_k, cache_v, cache_logf = past
        past_len = cache_k.shape[1]
        cache_k = jnp.transpose(cache_k, (0, 2, 1, 3))
        cache_v = jnp.transpose(cache_v, (0, 2, 1, 3))
        logf_t = jnp.transpose(cache_logf, (2, 0, 1)).reshape(N_HEADS_F * bsz, past_len)
        c_past = _cumsum_lanes(logf_t).reshape(N_HEADS_F, bsz, past_len)
        c_end = jnp.pad(c_past[:, :, past_len - 1].T, ((0, 0), (0, LANES - N_HEADS_F)))
        c0 = jnp.repeat(c_end, t_len, axis=0)
    kvf_out = _kvf(h_kv, w["w_kv"], w["w_f"], w["b_f"], c0,
                   bsz=bsz, t_len=t_len, tm=tiles.head_rows, emit_vt=past is None)
    k_new, v_new, logf, c, kx = kvf_out[:5]
    qx = _qproj(h1, w["w_q"], c, tm=tiles.head_rows)
    rest = _proj(h1, w["w_in_b"], first_col=F_DIM, tm=tiles.proj_rows, tn=tiles.rest_cols,
                 out_dtype=BF16)

    if past is None:
        o = _fox_prompt(qx, kx, kvf_out[5], bsz=bsz, t_len=t_len, tq=tiles.attn_queries,
                        cw=tiles.attn_chunk)
    else:
        o = _fox_cached(qx, kx, v_new, c, cache_k, cache_v, c_past,
                        bsz=bsz, t_len=t_len, tk=tiles.cache_keys)
    y = _tail_b(o, rest, mem_k, mem_v, w["w_out"][1], x1, w["g_final"][None],
                layer=1, bsz=bsz, t_len=t_len, tt=tiles.tail_rows)
    return (y.reshape(bsz, t_len, d), new_state[None],
            jnp.transpose(k_new, (0, 2, 1, 3)), jnp.transpose(v_new, (0, 2, 1, 3)),
            logf.reshape(bsz, t_len, N_HEADS_F))


def kernel(x_prompt, x_sample, state_conv, cache_k, cache_v, cache_logf, cache_mem_k, cache_mem_v,
           mem_prompt, g_norm, w_in_a, conv_w, w_in_b, w_out, g_mem, w_mem_kv, g_kv, w_kvf, b_f,
           g_final):
    depth = g_norm.shape[0]
    bp = x_prompt.shape[0]
    bs = x_sample.shape[0]
    w = {
        "g_norm": g_norm, "g_kv": g_kv, "g_final": g_final, "conv_w": conv_w[0],
        "w_in_a": w_in_a[0],
        "w_q": w_in_b[0][:, :F_DIM].astype(BF16),
        "w_in_b": w_in_b[0],
        "w_out": w_out.astype(BF16),
        "w_kv": w_kvf.T[:2 * F_DIM].astype(BF16),
        "w_f": jnp.pad(w_kvf.T[2 * F_DIM:], ((0, LANES - N_HEADS_F), (0, 0))).astype(BF16),
        "b_f": jnp.pad(b_f, (0, LANES - N_HEADS_F)).reshape(1, LANES),
    }

    mem_rows = mem_prompt.reshape(bp * N_MEM, D_MODEL)
    mem_kv = [_norm_proj(mem_rows, g_mem[i], w_mem_kv[i], tm=512, tn=512)
              for i in range(depth)]
    p_mem_k = jnp.stack([a[:, :M_DIM] for a in mem_kv]).reshape(depth, bp, N_MEM, N_HEADS_M, HEAD_DIM)
    p_mem_v = jnp.stack([a[:, M_DIM:] for a in mem_kv]).reshape(depth, bp, N_MEM, N_HEADS_M, HEAD_DIM)

    zero_conv = jnp.zeros((bp, CONV_W - 1, CONV_DIM), F32)
    y_p, p_state, p_k, p_v, p_logf = _trunk(
        x_prompt, zero_conv, p_mem_k.reshape(depth, bp, N_MEM, M_DIM),
        p_mem_v.reshape(depth, bp, N_MEM, M_DIM), None, w)
    y_s, s_state, s_k, s_v, s_logf = _trunk(
        x_sample, state_conv[0], cache_mem_k.reshape(depth, bs, N_MEM, M_DIM),
        cache_mem_v.reshape(depth, bs, N_MEM, M_DIM), (cache_k, cache_v, cache_logf), w)
    return (y_p, y_s, p_state, p_k, p_v, p_logf, p_mem_k, p_mem_v, s_state, s_k, s_v, s_logf)
```

```python
import functools
import math
from typing import NamedTuple

import jax
import jax.numpy as jnp
from jax import lax
from jax.experimental import pallas as pl
from jax.experimental.pallas import tpu as pltpu

F32 = jnp.float32
BF16 = jnp.bfloat16

D_MODEL = 2048
CONV_W = 3
CONV_DIM = 1536
HEAD_DIM = 128
N_HEADS_F = 12
F_DIM = N_HEADS_F * HEAD_DIM
N_MEM = 256
N_HEADS_M = 4
M_DIM = N_HEADS_M * HEAD_DIM
MIX_DIM = CONV_DIM + M_DIM
EPS = 1e-6
SCALE = 1.0 / math.sqrt(HEAD_DIM)
LOG2E = math.log2(math.e)

LANES = 128
AUG_DIM = 2 * HEAD_DIM
VT_ROWS = HEAD_DIM + 16
VMEM_LIMIT = 56 * 1024 * 1024
NORM_ROWS = 64


def _params(*sem):
    return pltpu.CompilerParams(dimension_semantics=sem, vmem_limit_bytes=VMEM_LIMIT)


def _resident(*shape):
    return pl.BlockSpec(shape, lambda *_: (0,) * len(shape), pipeline_mode=pl.Buffered(1))


def _tile_lanes(x, n):
    return jnp.concatenate([x] * n, axis=1)


def _silu(z):
    return z * (1.0 / (1.0 + jnp.exp(-z)))


def _split3(c):
    hi = c.astype(BF16).astype(F32)
    r = c - hi
    mid = r.astype(BF16).astype(F32)
    lo = (r - mid).astype(BF16).astype(F32)
    return hi, mid, lo


def _aug_block(c_col, key_side):
    rows = c_col.shape[0]
    hi, mid, lo = _split3(-c_col if key_side else c_col)
    lane = lax.broadcasted_iota(jnp.int32, (rows, LANES), 1)
    term0 = 3 if key_side else 0
    one0 = 0 if key_side else 3
    terms = jnp.where(lane == term0, hi,
                      jnp.where(lane == term0 + 1, mid, jnp.where(lane == term0 + 2, lo, 0.0)))
    return jnp.where((lane >= one0) & (lane < one0 + 3), 1.0, terms).astype(BF16)


def _rmsnorm_rows(x_ref, g_ref, h_ref, rows_per_chunk=NORM_ROWS):
    tm = x_ref.shape[0]
    g = g_ref[...]

    def body(r, carry):
        rows = pl.ds(pl.multiple_of(r * rows_per_chunk, rows_per_chunk), rows_per_chunk)
        x = x_ref[rows, :]
        ms = jnp.mean(x * x, axis=-1, keepdims=True)
        h_ref[rows, :] = (x * lax.rsqrt(ms + EPS) * g).astype(h_ref.dtype)
        return carry

    lax.fori_loop(0, tm // rows_per_chunk, body, 0)


def _norm_proj_kernel(x_ref, g_ref, w_ref, o_ref, h_ref):
    @pl.when(pl.program_id(1) == 0)
    def _():
        _rmsnorm_rows(x_ref, g_ref, h_ref)

    o_ref[...] = jnp.dot(h_ref[...], w_ref[...].astype(BF16),
                         preferred_element_type=F32).astype(o_ref.dtype)


def _norm_proj(x, g, w, *, tm, tn, out_dtype=F32):
    m, d = x.shape
    n = w.shape[1]
    tm = min(tm, m)
    assert m % tm == 0 and n % tn == 0
    return pl.pallas_call(
        _norm_proj_kernel,
        grid=(m // tm, n // tn),
        in_specs=[
            pl.BlockSpec((tm, d), lambda i, j: (i, 0)),
            pl.BlockSpec((1, d), lambda i, j: (0, 0)),
            pl.BlockSpec((d, tn), lambda i, j: (0, j)),
        ],
        out_specs=pl.BlockSpec((tm, tn), lambda i, j: (i, j)),
        out_shape=jax.ShapeDtypeStruct((m, n), out_dtype),
        scratch_shapes=[pltpu.VMEM((tm, d), BF16)],
        compiler_params=_params("parallel", "arbitrary"),
        name="norm_proj",
    )(x, g.reshape(1, d), w)


K_CHUNK = 512


def _memory_attention_into(qm_ref, zm_ref, mk_ref, mv_ref, mixed_ref, n_seq, seq_rows, heads):
    for s in range(n_seq):
        rows = slice(s * seq_rows, (s + 1) * seq_rows)
        for h in heads:
            cols = slice(h * HEAD_DIM, (h + 1) * HEAD_DIM)
            q = (qm_ref[rows, cols].astype(F32) * SCALE).astype(BF16)
            k = mk_ref.at[s][pl.ds(h, N_MEM, stride=N_HEADS_M), :].astype(BF16)
            v = mv_ref.at[s][pl.ds(h, N_MEM, stride=N_HEADS_M), :].astype(BF16)
            sc = lax.dot_general(q, k, (((1,), (1,)), ((), ())), preferred_element_type=F32)
            p = jnp.exp(sc - jnp.max(sc, axis=-1, keepdims=True))
            l = jnp.sum(p, axis=-1, keepdims=True)
            o = jnp.dot(p.astype(BF16), v, preferred_element_type=F32) / l
            out_cols = slice(CONV_DIM + h * HEAD_DIM, CONV_DIM + (h + 1) * HEAD_DIM)
            mixed_ref[rows, out_cols] = (o * _silu(zm_ref[rows, cols].astype(F32))).astype(
                mixed_ref.dtype)


def _memory_branch(qm_ref, zm_ref, mk_ref, mv_ref, mixed_ref, w_ref, x_ref, o_ref, n_seq, seq_rows):
    heads_per_chunk = K_CHUNK // HEAD_DIM
    for kc in range(CONV_DIM // K_CHUNK, MIX_DIM // K_CHUNK):
        first = (kc * K_CHUNK - CONV_DIM) // HEAD_DIM
        _memory_attention_into(qm_ref, zm_ref, mk_ref, mv_ref, mixed_ref, n_seq, seq_rows,
                               range(first, first + heads_per_chunk))
        _project_chunk(mixed_ref, w_ref, x_ref, o_ref, kc)


def _project_chunk(mixed_ref, w_ref, x_ref, o_ref, kc, tn=512):
    krows = slice(kc * K_CHUNK, (kc + 1) * K_CHUNK)
    a = mixed_ref[:, krows]
    for c in range(o_ref.shape[1] // tn):
        cols = slice(c * tn, (c + 1) * tn)
        base = x_ref[:, cols] if kc == 0 else o_ref[:, cols]
        o_ref[:, cols] = base + jnp.dot(a, w_ref[krows, cols], preferred_element_type=F32)


def _finish_rows(o_ref, g_ref, h_refs, inv_ref, final_norm, rows_per_chunk=NORM_ROWS):
    n_chunks = o_ref.shape[0] // rows_per_chunk
    chunk = lambda r: pl.ds(pl.multiple_of(r * rows_per_chunk, rows_per_chunk), rows_per_chunk)

    def stats(r, carry):
        x = o_ref[chunk(r), :]
        inv = lax.rsqrt(jnp.mean(x * x, axis=-1, keepdims=True) + EPS)
        inv_ref[chunk(r), :] = jnp.broadcast_to(inv, (rows_per_chunk, LANES))
        return carry

    def scale(r, carry):
        xn = o_ref[chunk(r), :] * _tile_lanes(inv_ref[chunk(r), :], o_ref.shape[1] // LANES)
        if final_norm:
            o_ref[chunk(r), :] = xn * g_ref[0:1, :]
        for k, h_ref in enumerate(h_refs):
            h_ref[chunk(r), :] = (xn * g_ref[k:k + 1, :]).astype(h_ref.dtype)
        return carry

    lax.fori_loop(0, n_chunks, stats, 0, unroll=True)
    lax.fori_loop(0, n_chunks, scale, 0, unroll=True)


def _layer_memory(layer, n_seq):
    return pl.BlockSpec((None, n_seq, N_MEM * N_HEADS_M, HEAD_DIM), lambda b, t: (layer, b, 0, 0))


def _tail_a_kernel(bg_ref, cg_ref, u_ref, zc_ref, cgh_ref, uh_ref, st_ref, qm_ref, zm_ref,
                   mk_ref, mv_ref, cw_ref, w_ref, x_ref, g_ref,
                   o_ref, h1_ref, h2_ref, nst_ref, mixed_ref, inv_ref):
    t = pl.program_id(1)
    n_seq, seq_rows = st_ref.shape[0], bg_ref.shape[0] // st_ref.shape[0]
    first = t == 0
    row = lax.broadcasted_iota(jnp.int32, (seq_rows, LANES), 0)
    hr = cgh_ref.shape[0]
    for c in range(CONV_DIM // LANES):
        cols = slice(c * LANES, (c + 1) * LANES)
        w = cw_ref[:, cols]
        halo = cgh_ref[:, cols].astype(F32) * uh_ref[:, cols].astype(F32)
        for s in range(n_seq):
            rows = slice(s * seq_rows, (s + 1) * seq_rows)
            ci = cg_ref[rows, cols].astype(F32) * u_ref[rows, cols].astype(F32)
            st = st_ref[s, :, cols]
            prev1 = jnp.where(first, st[1:2, :], halo[hr - 1:hr, :])
            prev2 = jnp.where(first, st[0:1, :], halo[hr - 2:hr - 1, :])
            s1 = jnp.where(row == 0, prev1, pltpu.roll(ci, 1, axis=0))
            s2 = jnp.where(row == 0, prev2, jnp.where(row == 1, prev1, pltpu.roll(ci, 2, axis=0)))
            conv = w[0:1, :] * s2 + w[1:2, :] * s1 + w[2:3, :] * ci
            branch = bg_ref[rows, cols].astype(F32) * conv * _silu(zc_ref[rows, cols].astype(F32))
            mixed_ref[rows, cols] = branch.astype(mixed_ref.dtype)
            nst_ref[s, :, cols] = ci[seq_rows - 2:seq_rows, :]

        if (c + 1) * LANES % K_CHUNK == 0:
            _project_chunk(mixed_ref, w_ref, x_ref, o_ref, (c + 1) * LANES // K_CHUNK - 1)

    _memory_branch(qm_ref, zm_ref, mk_ref, mv_ref, mixed_ref, w_ref, x_ref, o_ref, n_seq, seq_rows)
    _finish_rows(o_ref, g_ref, (h1_ref, h2_ref), inv_ref, False)


def _tail_a(proj, state, mem_k, mem_v, conv_w, w_out, x, gains, *, layer, bsz, t_len, tt):
    m = bsz * t_len
    seq_rows = min(tt, t_len)
    n_seq = tt // seq_rows
    nt = t_len // seq_rows
    hr = 16
    assert t_len % seq_rows == 0 and bsz % n_seq == 0 and seq_rows % hr == 0
    tile = lambda b, t: b * nt + t
    wide = lambda k: pl.BlockSpec((tt, CONV_DIM), lambda b, t: (tile(b, t), k))
    halo = lambda k: pl.BlockSpec(
        (hr, CONV_DIM), lambda b, t: (jnp.maximum(tile(b, t) * (tt // hr) - 1, 0), k))
    narrow = lambda k: pl.BlockSpec((tt, M_DIM), lambda b, t: (tile(b, t), k))
    per_seq = lambda *shape: pl.BlockSpec((n_seq,) + shape, lambda b, t: (b,) + (0,) * len(shape))
    row = pl.BlockSpec((tt, D_MODEL), lambda b, t: (tile(b, t), 0))
    q_col = 4 * CONV_DIM // M_DIM
    return pl.pallas_call(
        _tail_a_kernel,
        grid=(bsz // n_seq, nt),
        in_specs=[wide(0), wide(1), wide(2), wide(3), halo(1), halo(2),
                  per_seq(CONV_W - 1, CONV_DIM), narrow(q_col), narrow(q_col + 1),
                  _layer_memory(layer, n_seq), _layer_memory(layer, n_seq),
                  _resident(CONV_W, CONV_DIM),
                  _resident(MIX_DIM, D_MODEL), row, _resident(2, D_MODEL)],
        out_specs=[row, row, row, per_seq(CONV_W - 1, CONV_DIM)],
        out_shape=[jax.ShapeDtypeStruct((m, D_MODEL), F32),
                   jax.ShapeDtypeStruct((m, D_MODEL), BF16),
                   jax.ShapeDtypeStruct((m, D_MODEL), BF16),
                   jax.ShapeDtypeStruct((bsz, CONV_W - 1, CONV_DIM), F32)],
        scratch_shapes=[pltpu.VMEM((tt, MIX_DIM), BF16), pltpu.VMEM((tt, LANES), F32)],
        compiler_params=_params("parallel", "arbitrary"),
        name="tail_a",
    )(proj, proj, proj, proj, proj, proj, state, proj, proj, mem_k, mem_v, conv_w, w_out, x, gains)


def _tail_b_kernel(a_ref, zf_ref, qm_ref, zm_ref, mk_ref, mv_ref, w_ref, x_ref, g_ref,
                   o_ref, mixed_ref, inv_ref):
    n_seq = mk_ref.shape[0]
    seq_rows = a_ref.shape[0] // n_seq
    for c in range(F_DIM // LANES):
        cols = slice(c * LANES, (c + 1) * LANES)
        mixed_ref[:, cols] = (a_ref[:, cols].astype(F32)
                              * _silu(zf_ref[:, cols].astype(F32))).astype(mixed_ref.dtype)
        if (c + 1) * LANES % K_CHUNK == 0:
            _project_chunk(mixed_ref, w_ref, x_ref, o_ref, (c + 1) * LANES // K_CHUNK - 1)
    _memory_branch(qm_ref, zm_ref, mk_ref, mv_ref, mixed_ref, w_ref, x_ref, o_ref, n_seq, seq_rows)
    _finish_rows(o_ref, g_ref, (), inv_ref, True)


def _tail_b(a, rest, mem_k, mem_v, w_out, x, gain, *, layer, bsz, t_len, tt):
    m = bsz * t_len
    seq_rows = min(tt, t_len)
    n_seq = tt // seq_rows
    nt = t_len // seq_rows
    assert t_len % seq_rows == 0 and bsz % n_seq == 0
    tile = lambda b, t: b * nt + t
    narrow = lambda k: pl.BlockSpec((tt, M_DIM), lambda b, t: (tile(b, t), k))
    mem = _layer_memory(layer, n_seq)
    wide = pl.BlockSpec((tt, F_DIM), lambda b, t: (tile(b, t), 0))
    row = pl.BlockSpec((tt, D_MODEL), lambda b, t: (tile(b, t), 0))
    q_col = F_DIM // M_DIM
    return pl.pallas_call(
        _tail_b_kernel,
        grid=(bsz // n_seq, nt),
        in_specs=[wide, wide, narrow(q_col), narrow(q_col + 1), mem, mem,
                  _resident(MIX_DIM, D_MODEL), row, _resident(1, D_MODEL)],
        out_specs=row,
        out_shape=jax.ShapeDtypeStruct((m, D_MODEL), F32),
        scratch_shapes=[pltpu.VMEM((tt, MIX_DIM), BF16), pltpu.VMEM((tt, LANES), F32)],
        compiler_params=_params("parallel", "parallel"),
        name="tail_b",
    )(a, rest, rest, rest, mem_k, mem_v, w_out, x, gain)


def _proj_kernel(h_ref, w_ref, o_ref):
    o_ref[...] = jnp.dot(h_ref[...], w_ref[...].astype(BF16),
                         preferred_element_type=F32).astype(o_ref.dtype)


def _proj(h, w, *, first_col, tm, tn, out_dtype):
    m, d = h.shape
    n = w.shape[1] - first_col
    tm = min(tm, m)
    assert m % tm == 0 and n % tn == 0 and first_col % tn == 0
    return pl.pallas_call(
        _proj_kernel,
        grid=(m // tm, n // tn),
        in_specs=[pl.BlockSpec((tm, d), lambda i, j: (i, 0)),
                  pl.BlockSpec((d, tn), lambda i, j: (0, j + first_col // tn))],
        out_specs=pl.BlockSpec((tm, tn), lambda i, j: (i, j)),
        out_shape=jax.ShapeDtypeStruct((m, n), out_dtype),
        compiler_params=_params("parallel", "arbitrary"),
        name="proj",
    )(h, w)


def _kvf_kernel(h_ref, wkv_ref, wf_ref, bf_ref, c0_ref, *refs,
                seg, tiles_per_seq, tn, emit_vt):
    if emit_vt:
        k_ref, v_ref, logf_ref, c_ref, kx_ref, vt_ref, carry_ref = refs
    else:
        k_ref, v_ref, logf_ref, c_ref, kx_ref, carry_ref = refs
    n_seq, seq_rows = k_ref.shape[0], k_ref.shape[2]
    i = pl.program_id(0)
    tm = h_ref.shape[0]
    h = h_ref[...]

    heads_per_chunk = tn // HEAD_DIM
    n_chunks = F_DIM // tn
    nt_dims = (((1,), (1,)), ((), ()))

    def project(n):
        return lax.dot_general(h, wkv_ref[n * tn:(n + 1) * tn, :], nt_dims,
                               preferred_element_type=F32)

    logit = lax.dot_general(h, wf_ref[...], nt_dims, preferred_element_type=F32) + bf_ref[...]

    for n in range(n_chunks, 2 * n_chunks):
        y = project(n)
        for hh in range(heads_per_chunk):
            head = (n - n_chunks) * heads_per_chunk + hh
            yh = y[:, hh * HEAD_DIM:(hh + 1) * HEAD_DIM]
            if emit_vt:
                vt_ref[0, head, 0, 0:HEAD_DIM, :] = yh.T.astype(BF16)
                one_row = lax.broadcasted_iota(jnp.int32, (VT_ROWS - HEAD_DIM, tm), 0) == 0
                vt_ref[0, head, 0, HEAD_DIM:VT_ROWS, :] = jnp.where(one_row, 1.0, 0.0).astype(BF16)
            for s in range(n_seq):
                v_ref[s, head] = yh[s * seq_rows:(s + 1) * seq_rows]

    logf = jnp.minimum(logit, 0.0) - jnp.log1p(jnp.exp(-jnp.abs(logit)))
    logf_ref[...] = logf[:, :N_HEADS_F]
    pos = lax.broadcasted_iota(jnp.int32, (tm, LANES), 0) & (seg - 1)
    c = logf
    step = 1
    while step < seg:
        c = c + jnp.where(pos >= step, pltpu.roll(c, step, axis=0), 0.0)
        step *= 2
    if tiles_per_seq > 1:
        @pl.when(i % tiles_per_seq == 0)
        def _():
            carry_ref[...] = c0_ref[0:1, :]

        c = c + carry_ref[...]
        carry_ref[...] = c[tm - 1:tm, :]
    else:
        c = c + c0_ref[...]
    c_ref[...] = c

    c2 = c * LOG2E
    for n in range(n_chunks):
        y = project(n)
        for hh in range(heads_per_chunk):
            head = n * heads_per_chunk + hh
            yh = y[:, hh * HEAD_DIM:(hh + 1) * HEAD_DIM]
            yb = yh.astype(BF16)
            aug = _aug_block(c2[:, head:head + 1], True)
            for s in range(n_seq):
                rows = slice(s * seq_rows, (s + 1) * seq_rows)
                k_ref[s, head] = yh[rows]
                kx_ref[s, head, :, 0:HEAD_DIM] = yb[rows]
                kx_ref[s, head, :, HEAD_DIM:AUG_DIM] = aug[rows]


def _kvf(h, w_kv, w_f, b_f, c0, *, bsz, t_len, tm, emit_vt):
    m, d = h.shape
    tm = min(tm, m)
    seg = min(t_len, tm)
    assert m % tm == 0 and seg & (seg - 1) == 0 and (t_len % tm == 0 or tm % t_len == 0)
    tiles_per_seq = max(t_len // tm, 1)
    n_seq = tm // seg
    row = lambda w: pl.BlockSpec((tm, w), lambda i: (i, 0))
    heads = lambda w: pl.BlockSpec((n_seq, N_HEADS_F, seg, w),
                                   lambda i: (i // tiles_per_seq, 0, i % tiles_per_seq, 0))
    out_specs = [heads(HEAD_DIM), heads(HEAD_DIM), row(N_HEADS_F), row(LANES), heads(AUG_DIM)]
    out_shape = [jax.ShapeDtypeStruct((bsz, N_HEADS_F, t_len, HEAD_DIM), F32),
                 jax.ShapeDtypeStruct((bsz, N_HEADS_F, t_len, HEAD_DIM), F32),
                 jax.ShapeDtypeStruct((m, N_HEADS_F), F32),
                 jax.ShapeDtypeStruct((m, LANES), F32),
                 jax.ShapeDtypeStruct((bsz, N_HEADS_F, t_len, AUG_DIM), BF16)]
    if emit_vt:
        assert tiles_per_seq * tm == t_len
        out_specs.append(pl.BlockSpec(
            (1, N_HEADS_F, 1, VT_ROWS, tm),
            lambda i: (i // tiles_per_seq, 0, i % tiles_per_seq, 0, 0)))
        out_shape.append(
            jax.ShapeDtypeStruct((bsz, N_HEADS_F, tiles_per_seq, VT_ROWS, tm), BF16))
    return pl.pallas_call(
        functools.partial(_kvf_kernel, seg=seg, tiles_per_seq=tiles_per_seq, tn=512,
                          emit_vt=emit_vt),
        grid=(m // tm,),
        in_specs=[row(d), _resident(2 * F_DIM, d), _resident(LANES, d), _resident(1, LANES),
                  row(LANES)],
        out_specs=out_specs,
        out_shape=out_shape,
        scratch_shapes=[pltpu.VMEM((1, LANES), F32)],
        compiler_params=_params("arbitrary"),
        name="kvf",
    )(h, w_kv, w_f, b_f, c0)


def _qproj_kernel(h_ref, wq_ref, c_ref, qx_ref, *, tn):
    h = h_ref[...]
    c = c_ref[...] * LOG2E
    heads_per_chunk = tn // HEAD_DIM
    for n in range(F_DIM // tn):
        y = jnp.dot(h, wq_ref[:, n * tn:(n + 1) * tn],
                    preferred_element_type=F32) * (SCALE * LOG2E)
        for hh in range(heads_per_chunk):
            head = n * heads_per_chunk + hh
            base = head * AUG_DIM
            qx_ref[:, base:base + HEAD_DIM] = y[:, hh * HEAD_DIM:(hh + 1) * HEAD_DIM].astype(BF16)
            qx_ref[:, base + HEAD_DIM:base + AUG_DIM] = _aug_block(c[:, head:head + 1], False)


def _qproj(h, w_q, c, *, tm):
    m, d = h.shape
    tm = min(tm, m)
    assert m % tm == 0
    return pl.pallas_call(
        functools.partial(_qproj_kernel, tn=512),
        grid=(m // tm,),
        in_specs=[pl.BlockSpec((tm, d), lambda i: (i, 0)),
                  _resident(d, F_DIM),
                  pl.BlockSpec((tm, LANES), lambda i: (i, 0))],
        out_specs=pl.BlockSpec((tm, N_HEADS_F * AUG_DIM), lambda i: (i, 0)),
        out_shape=jax.ShapeDtypeStruct((m, N_HEADS_F * AUG_DIM), BF16),
        compiler_params=_params("parallel"),
        name="qproj",
    )(h, w_q, c)


def _fox_prompt_kernel(qx_ref, kx_ref, vt_ref, o_ref, m_ref, acc_ref, sa_ref, sb_ref, ma_ref, mb_ref,
                       *, cw):
    qi = pl.program_id(2)
    tq = qx_ref.shape[0]
    tk = sa_ref.shape[1]
    tkv = vt_ref.shape[4]
    n_chunks = tq // cw
    buf_a, buf_b = (sa_ref, ma_ref), (sb_ref, mb_ref)
    m_ref[...] = jnp.full_like(m_ref, -jnp.inf)
    acc_ref[...] = jnp.zeros_like(acc_ref)

    def scores(kb, visible, buf):
        s_ref, smax_ref = buf
        for c, (keys, key_minus_query) in visible.items():
            k = kx_ref[0, 0, pl.ds(pl.multiple_of(kb * tk, tk), keys), :]
            q = qx_ref[c * cw:(c + 1) * cw, :]
            s = lax.dot_general(k, q, (((1,), (1,)), ((), ())), preferred_element_type=F32)
            if key_minus_query is not None:
                key = lax.broadcasted_iota(jnp.int32, (keys, cw), 0) + key_minus_query
                query = lax.broadcasted_iota(jnp.int32, (keys, cw), 1)
                s = jnp.where(key <= query, s, -jnp.inf)
            s_ref[c, 0:keys, :] = s
            smax_ref[c] = jnp.max(s, axis=0, keepdims=True)

    def update(kb, visible, buf):
        s_ref, smax_ref = buf
        for c, (keys, _) in visible.items():
            m_old = m_ref[c]
            m_new = jnp.maximum(m_old, smax_ref[c])
            alpha = jnp.exp2(m_old - m_new)
            acc = alpha * acc_ref[c]
            for lo in range(0, keys, tkv):
                n = min(tkv, keys - lo)
                p = jnp.exp2(s_ref[c, lo:lo + n, :] - m_new).astype(BF16)
                acc = acc + jnp.dot(vt_ref[0, 0, kb * (tk // tkv) + lo // tkv, :, 0:n], p,
                                    preferred_element_type=F32)
            acc_ref[c] = acc
            m_ref[c] = m_new

    every = {c: (tk, None) for c in range(n_chunks)}
    own = []
    for jj in range(2):
        visible = {}
        for c in range(n_chunks):
            k_lo, q_lo = jj * tk, c * cw
            keys = min(tk, q_lo + cw - k_lo)
            if keys > 0:
                visible[c] = (keys, k_lo - q_lo if k_lo + keys - 1 > q_lo else None)
        own.append(visible)

    @pl.when(qi == 0)
    def _():
        scores(0, own[0], buf_a)
        scores(1, own[1], buf_b)
        update(0, own[0], buf_a)
        update(1, own[1], buf_b)

    @pl.when(qi > 0)
    def _():
        n = 2 * qi
        scores(0, every, buf_a)

        def pair(j):
            scores(j, every, buf_b)
            update(j - 1, every, buf_a)
            scores(j + 1, every, buf_a)
            update(j, every, buf_b)

        def two_pairs(t, carry):
            pair(4 * t + 1)
            pair(4 * t + 3)
            return carry

        n_pairs = qi - 1
        lax.fori_loop(0, n_pairs // 2, two_pairs, 0)

        @pl.when(n_pairs % 2 == 1)
        def _():
            pair(n - 3)

        scores(n - 1, every, buf_b)
        update(n - 2, every, buf_a)
        scores(n, own[0], buf_a)
        update(n - 1, every, buf_b)
        scores(n + 1, own[1], buf_b)
        update(n, own[0], buf_a)
        update(n + 1, own[1], buf_b)

    for c in range(n_chunks):
        acc = acc_ref[c]
        o = acc[0:HEAD_DIM, :] / acc[HEAD_DIM:HEAD_DIM + 1, :]
        o_ref[c * cw:(c + 1) * cw, :] = o.T.astype(o_ref.dtype)


def _fox_prompt(qx, kx, vt, *, bsz, t_len, tq, cw):
    m = bsz * t_len
    nq = t_len // tq
    tk = tq // 2
    tkv = vt.shape[4]
    n_chunks = tq // cw
    assert t_len % tq == 0 and tk % tkv == 0 and tq % cw == 0
    return pl.pallas_call(
        functools.partial(_fox_prompt_kernel, cw=cw),
        grid=(bsz, N_HEADS_F, nq),
        in_specs=[pl.BlockSpec((tq, AUG_DIM), lambda b, h, qi: (b * nq + qi, h)),
                  pl.BlockSpec((1, 1, t_len, AUG_DIM), lambda b, h, qi: (b, h, 0, 0)),
                  pl.BlockSpec((1, 1, t_len // tkv, VT_ROWS, tkv),
                               lambda b, h, qi: (b, h, 0, 0, 0))],
        out_specs=pl.BlockSpec((tq, HEAD_DIM), lambda b, h, qi: (b * nq + qi, h)),
        out_shape=jax.ShapeDtypeStruct((m, F_DIM), BF16),
        scratch_shapes=[pltpu.VMEM((n_chunks, 1, cw), F32),
                        pltpu.VMEM((n_chunks, VT_ROWS, cw), F32),
                        pltpu.VMEM((n_chunks, tk, cw), F32), pltpu.VMEM((n_chunks, tk, cw), F32),
                        pltpu.VMEM((n_chunks, 1, cw), F32), pltpu.VMEM((n_chunks, 1, cw), F32)],
        compiler_params=_params("parallel", "parallel", "arbitrary"),
        name="fox_prompt",
    )(qx, kx, vt)


def _fox_cached_kernel(qx_ref, kxn_ref, vn_ref, cn_ref, ck_ref, cv_ref, cp_ref, o_ref,
                       m_ref, l_ref, acc_ref, cq_ref, smax_ref, s_ref):
    b = pl.program_id(0)
    ki = pl.program_id(1)
    t_new = qx_ref.shape[0]

    @pl.when(ki == 0)
    def _():
        m_ref[...] = jnp.full_like(m_ref, -jnp.inf)
        l_ref[...] = jnp.zeros_like(l_ref)
        acc_ref[...] = jnp.zeros_like(acc_ref)
        for h in range(N_HEADS_F):
            cq_ref[h] = jnp.broadcast_to(cn_ref[:, h:h + 1] * LOG2E, (t_new, LANES))

    def update(h, s, s_max, v):
        n = s.shape[1]
        across = (lambda x: _tile_lanes(x, n // LANES)) if n >= LANES else (lambda x: x[:, :n])
        m_old = m_ref[h]
        m_new = jnp.maximum(m_old, s_max)
        alpha = jnp.exp2(m_old - m_new)
        p = jnp.exp2(s - across(m_new))
        l_ref[h] = alpha * l_ref[h] + jnp.sum(p, axis=-1, keepdims=True)
        acc_ref[h] = alpha * acc_ref[h] + jnp.dot(p.astype(BF16), v, preferred_element_type=F32)
        m_ref[h] = m_new

    tk = ck_ref.shape[2]
    for h in range(N_HEADS_F):
        q = qx_ref[:, h * AUG_DIM:h * AUG_DIM + HEAD_DIM]
        k = ck_ref[0, h].astype(BF16)
        s = lax.dot_general(q, k, (((1,), (1,)), ((), ())), preferred_element_type=F32)
        decay = _tile_lanes(cq_ref[h], tk // LANES) - cp_ref[h, pl.ds(b, 1), :] * LOG2E
        s = s + decay
        s_ref[h] = s
        smax_ref[h] = jnp.broadcast_to(jnp.max(s, axis=-1, keepdims=True), (t_new, LANES))
    for h in range(N_HEADS_F):
        update(h, s_ref[h], smax_ref[h], cv_ref[0, h].astype(BF16))

    @pl.when(ki == pl.num_programs(1) - 1)
    def _():
        row = lax.broadcasted_iota(jnp.int32, (t_new, t_new), 0)
        col = lax.broadcasted_iota(jnp.int32, (t_new, t_new), 1)
        for h in range(N_HEADS_F):
            xcols = slice(h * AUG_DIM, (h + 1) * AUG_DIM)
            s = lax.dot_general(qx_ref[:, xcols], kxn_ref[0, h], (((1,), (1,)), ((), ())),
                                preferred_element_type=F32)
            s = jnp.where(col <= row, s, -jnp.inf)
            s_ref[h, :, 0:t_new] = s
            smax_ref[h] = jnp.broadcast_to(jnp.max(s, axis=-1, keepdims=True), (t_new, LANES))
        for h in range(N_HEADS_F):
            cols = slice(h * HEAD_DIM, (h + 1) * HEAD_DIM)
            update(h, s_ref[h, :, 0:t_new], smax_ref[h], vn_ref[0, h].astype(BF16))
            o_ref[:, cols] = (acc_ref[h] / l_ref[h]).astype(o_ref.dtype)


def _fox_cached(qx, kx_new, v_new, c_new, cache_k, cache_v, c_past, *, bsz, t_len, tk):
    m = bsz * t_len
    past = cache_k.shape[2]
    assert past % tk == 0 and past > 0
    seq = lambda w: pl.BlockSpec((t_len, w), lambda b, ki: (b, 0))
    new = lambda w: pl.BlockSpec((1, N_HEADS_F, t_len, w), lambda b, ki: (b, 0, 0, 0))
    cache = pl.BlockSpec((1, N_HEADS_F, tk, HEAD_DIM), lambda b, ki: (b, 0, ki, 0))
    return pl.pallas_call(
        _fox_cached_kernel,
        grid=(bsz, past // tk),
        in_specs=[seq(N_HEADS_F * AUG_DIM), new(AUG_DIM), new(HEAD_DIM), seq(LANES),
                  cache, cache,
                  pl.BlockSpec((N_HEADS_F, bsz, tk), lambda b, ki: (0, 0, ki))],
        out_specs=seq(F_DIM),
        out_shape=jax.ShapeDtypeStruct((m, F_DIM), BF16),
        scratch_shapes=[pltpu.VMEM((N_HEADS_F, t_len, LANES), F32) for _ in range(5)]
        + [pltpu.VMEM((N_HEADS_F, t_len, tk), F32)],
        compiler_params=_params("parallel", "arbitrary"),
        name="fox_cached",
    )(qx, kx_new, v_new, c_new, cache_k, cache_v, c_past)


def _cumsum_lanes_kernel(x_ref, o_ref):
    rows, n = x_ref.shape
    r = lax.broadcasted_iota(jnp.int32, (LANES, LANES), 0)
    c = lax.broadcasted_iota(jnp.int32, (LANES, LANES), 1)
    upper = jnp.where(r <= c, 1.0, 0.0).astype(BF16)
    carry = jnp.zeros((rows, 1), F32)
    for j in range(n // LANES):
        cols = slice(j * LANES, (j + 1) * LANES)
        hi, mid, lo = _split3(x_ref[:, cols])
        local = (jnp.dot(hi.astype(BF16), upper, preferred_element_type=F32)
                 + jnp.dot(mid.astype(BF16), upper, preferred_element_type=F32)
                 + jnp.dot(lo.astype(BF16), upper, preferred_element_type=F32))
        o_ref[:, cols] = local + carry
        carry = carry + local[:, LANES - 1:LANES]


def _cumsum_lanes(x):
    rows, n = x.shape
    return pl.pallas_call(
        _cumsum_lanes_kernel,
        grid=(1,),
        in_specs=[pl.BlockSpec((rows, n), lambda i: (0, 0))],
        out_specs=pl.BlockSpec((rows, n), lambda i: (0, 0)),
        out_shape=jax.ShapeDtypeStruct((rows, n), F32),
        compiler_params=_params("arbitrary"),
        name="cumsum_lanes",
    )(x)


class _Tiles(NamedTuple):
    proj_rows: int
    proj_cols: int
    rest_cols: int
    head_rows: int
    tail_rows: int
    attn_queries: int
    attn_chunk: int
    cache_keys: int


def _tiles(t_len):
    tail_rows = 512 if t_len >= 512 else 256
    return _Tiles(proj_rows=1024, proj_cols=1024, rest_cols=512, head_rows=512,
                  tail_rows=tail_rows, attn_queries=1024, attn_chunk=256, cache_keys=1024)


def _trunk(x, conv_state, mem_k, mem_v, past, w):
    bsz, t_len, d = x.shape
    m = bsz * t_len
    x0 = x.reshape(m, d)
    tiles = _tiles(t_len)

    proj = _norm_proj(x0, w["g_norm"][0], w["w_in_a"], tm=tiles.proj_rows, tn=tiles.proj_cols,
                      out_dtype=BF16)
    x1, h1, h_kv, new_state = _tail_a(
        proj, conv_state, mem_k, mem_v, w["conv_w"], w["w_out"][0], x0,
        jnp.stack([w["g_norm"][1], w["g_kv"]]), layer=0, bsz=bsz, t_len=t_len,
        tt=tiles.tail_rows)

    if past is None:
        c0 = jnp.zeros((m, LANES), F32)
    else:
        cache_k, cache_v, cache_logf = past
        past_len = cache_k.shape[1]
        cache_k = jnp.transpose(cache_k, (0, 2, 1, 3))
        cache_v = jnp.transpose(cache_v, (0, 2, 1, 3))
        logf_t = jnp.transpose(cache_logf, (2, 0, 1)).reshape(N_HEADS_F * bsz, past_len)
        c_past = _cumsum_lanes(logf_t).reshape(N_HEADS_F, bsz, past_len)
        c_end = jnp.pad(c_past[:, :, past_len - 1].T, ((0, 0), (0, LANES - N_HEADS_F)))
        c0 = jnp.repeat(c_end, t_len, axis=0)
    kvf_out = _kvf(h_kv, w["w_kv"], w["w_f"], w["b_f"], c0,
                   bsz=bsz, t_len=t_len, tm=tiles.head_rows, emit_vt=past is None)
    k_new, v_new, logf, c, kx = kvf_out[:5]
    qx = _qproj(h1, w["w_q"], c, tm=tiles.head_rows)
    rest = _proj(h1, w["w_in_b"], first_col=F_DIM, tm=tiles.proj_rows, tn=tiles.rest_cols,
                 out_dtype=BF16)

    if past is None:
        o = _fox_prompt(qx, kx, kvf_out[5], bsz=bsz, t_len=t_len, tq=tiles.attn_queries,
                        cw=tiles.attn_chunk)
    else:
        o = _fox_cached(qx, kx, v_new, c, cache_k, cache_v, c_past,
                        bsz=bsz, t_len=t_len, tk=tiles.cache_keys)
    y = _tail_b(o, rest, mem_k, mem_v, w["w_out"][1], x1, w["g_final"][None],
                layer=1, bsz=bsz, t_len=t_len, tt=tiles.tail_rows)
    return (y.reshape(bsz, t_len, d), new_state[None],
            jnp.transpose(k_new, (0, 2, 1, 3)), jnp.transpose(v_new, (0, 2, 1, 3)),
            logf.reshape(bsz, t_len, N_HEADS_F))


def kernel(x_prompt, x_sample, state_conv, cache_k, cache_v, cache_logf, cache_mem_k, cache_mem_v,
           mem_prompt, g_norm, w_in_a, conv_w, w_in_b, w_out, g_mem, w_mem_kv, g_kv, w_kvf, b_f,
           g_final):
    depth = g_norm.shape[0]
    bp = x_prompt.shape[0]
    bs = x_sample.shape[0]
    w = {
        "g_norm": g_norm, "g_kv": g_kv, "g_final": g_final, "conv_w": conv_w[0],
        "w_in_a": w_in_a[0],
        "w_q": w_in_b[0][:, :F_DIM].astype(BF16),
        "w_in_b": w_in_b[0],
        "w_out": w_out.astype(BF16),
        "w_kv": w_kvf.T[:2 * F_DIM].astype(BF16),
        "w_f": jnp.pad(w_kvf.T[2 * F_DIM:], ((0, LANES - N_HEADS_F), (0, 0))).astype(BF16),
        "b_f": jnp.pad(b_f, (0, LANES - N_HEADS_F)).reshape(1, LANES),
    }

    mem_rows = mem_prompt.reshape(bp * N_MEM, D_MODEL)
    mem_kv = [_norm_proj(mem_rows, g_mem[i], w_mem_kv[i], tm=512, tn=512)
              for i in range(depth)]
    p_mem_k = jnp.stack([a[:, :M_DIM] for a in mem_kv]).reshape(depth, bp, N_MEM, N_HEADS_M, HEAD_DIM)
    p_mem_v = jnp.stack([a[:, M_DIM:] for a in mem_kv]).reshape(depth, bp, N_MEM, N_HEADS_M, HEAD_DIM)

    zero_conv = jnp.zeros((bp, CONV_W - 1, CONV_DIM), F32)
    y_p, p_state, p_k, p_v, p_logf = _trunk(
        x_prompt, zero_conv, p_mem_k.reshape(depth, bp, N_MEM * N_HEADS_M, HEAD_DIM),
        p_mem_v.reshape(depth, bp, N_MEM * N_HEADS_M, HEAD_DIM), None, w)
    y_s, s_state, s_k, s_v, s_logf = _trunk(
        x_sample, state_conv[0], cache_mem_k.reshape(depth, bs, N_MEM * N_HEADS_M, HEAD_DIM),
        cache_mem_v.reshape(depth, bs, N_MEM * N_HEADS_M, HEAD_DIM),
        (cache_k, cache_v, cache_logf), w)
    return (y_p, y_s, p_state, p_k, p_v, p_logf, p_mem_k, p_mem_v, s_state, s_k, s_v, s_logf)
```

```python
import functools
import math
from typing import NamedTuple

import jax
import jax.numpy as jnp
from jax import lax
from jax.experimental import pallas as pl
from jax.experimental.pallas import tpu as pltpu

F32 = jnp.float32
BF16 = jnp.bfloat16

D_MODEL = 2048
CONV_W = 3
CONV_DIM = 1536
HEAD_DIM = 128
N_HEADS_F = 12
F_DIM = N_HEADS_F * HEAD_DIM
N_MEM = 256
N_HEADS_M = 4
M_DIM = N_HEADS_M * HEAD_DIM
MIX_DIM = CONV_DIM + M_DIM
EPS = 1e-6
SCALE = 1.0 / math.sqrt(HEAD_DIM)
LOG2E = math.log2(math.e)

LANES = 128
AUG_DIM = 2 * HEAD_DIM
VT_ROWS = HEAD_DIM + 16
VMEM_LIMIT = 56 * 1024 * 1024
NORM_ROWS = 64


def _params(*sem):
    return pltpu.CompilerParams(dimension_semantics=sem, vmem_limit_bytes=VMEM_LIMIT)


def _resident(*shape):
    return pl.BlockSpec(shape, lambda *_: (0,) * len(shape), pipeline_mode=pl.Buffered(1))


def _tile_lanes(x, n):
    return jnp.concatenate([x] * n, axis=1)


def _silu(z):
    return z * (1.0 / (1.0 + jnp.exp(-z)))


def _split3(c):
    hi = c.astype(BF16).astype(F32)
    r = c - hi
    mid = r.astype(BF16).astype(F32)
    lo = (r - mid).astype(BF16).astype(F32)
    return hi, mid, lo


def _aug_block(c_col, key_side):
    rows = c_col.shape[0]
    hi, mid, lo = _split3(-c_col if key_side else c_col)
    lane = lax.broadcasted_iota(jnp.int32, (rows, LANES), 1)
    term0 = 3 if key_side else 0
    one0 = 0 if key_side else 3
    terms = jnp.where(lane == term0, hi,
                      jnp.where(lane == term0 + 1, mid, jnp.where(lane == term0 + 2, lo, 0.0)))
    return jnp.where((lane >= one0) & (lane < one0 + 3), 1.0, terms).astype(BF16)


def _rmsnorm_rows(x_ref, g_ref, h_ref, rows_per_chunk=NORM_ROWS):
    tm = x_ref.shape[0]
    g = g_ref[...]

    def body(r, carry):
        rows = pl.ds(pl.multiple_of(r * rows_per_chunk, rows_per_chunk), rows_per_chunk)
        x = x_ref[rows, :]
        ms = jnp.mean(x * x, axis=-1, keepdims=True)
        h_ref[rows, :] = (x * lax.rsqrt(ms + EPS) * g).astype(h_ref.dtype)
        return carry

    lax.fori_loop(0, tm // rows_per_chunk, body, 0)


def _norm_proj_kernel(x_ref, g_ref, w_ref, o_ref, h_ref):
    @pl.when(pl.program_id(1) == 0)
    def _():
        _rmsnorm_rows(x_ref, g_ref, h_ref)

    o_ref[...] = jnp.dot(h_ref[...], w_ref[...].astype(BF16),
                         preferred_element_type=F32).astype(o_ref.dtype)


def _norm_proj(x, g, w, *, tm, tn, out_dtype=F32):
    m, d = x.shape
    n = w.shape[1]
    tm = min(tm, m)
    assert m % tm == 0 and n % tn == 0
    return pl.pallas_call(
        _norm_proj_kernel,
        grid=(m // tm, n // tn),
        in_specs=[
            pl.BlockSpec((tm, d), lambda i, j: (i, 0)),
            pl.BlockSpec((1, d), lambda i, j: (0, 0)),
            pl.BlockSpec((d, tn), lambda i, j: (0, j)),
        ],
        out_specs=pl.BlockSpec((tm, tn), lambda i, j: (i, j)),
        out_shape=jax.ShapeDtypeStruct((m, n), out_dtype),
        scratch_shapes=[pltpu.VMEM((tm, d), BF16)],
        compiler_params=_params("parallel", "arbitrary"),
        name="norm_proj",
    )(x, g.reshape(1, d), w)


K_CHUNK = 512


def _memory_attention_into(qm_ref, zm_ref, mk_ref, mv_ref, mixed_ref, n_seq, seq_rows, heads):
    for s in range(n_seq):
        rows = slice(s * seq_rows, (s + 1) * seq_rows)
        for h in heads:
            cols = slice(h * HEAD_DIM, (h + 1) * HEAD_DIM)
            q = (qm_ref[rows, cols].astype(F32) * SCALE).astype(BF16)
            k = mk_ref.at[s][pl.ds(h, N_MEM, stride=N_HEADS_M), :].astype(BF16)
            v = mv_ref.at[s][pl.ds(h, N_MEM, stride=N_HEADS_M), :].astype(BF16)
            sc = lax.dot_general(q, k, (((1,), (1,)), ((), ())), preferred_element_type=F32)
            p = jnp.exp(sc - jnp.max(sc, axis=-1, keepdims=True))
            l = jnp.sum(p, axis=-1, keepdims=True)
            o = jnp.dot(p.astype(BF16), v, preferred_element_type=F32) / l
            out_cols = slice(CONV_DIM + h * HEAD_DIM, CONV_DIM + (h + 1) * HEAD_DIM)
            mixed_ref[rows, out_cols] = (o * _silu(zm_ref[rows, cols].astype(F32))).astype(
                mixed_ref.dtype)


def _memory_branch(qm_ref, zm_ref, mk_ref, mv_ref, mixed_ref, w_ref, x_ref, o_ref, n_seq, seq_rows):
    heads_per_chunk = K_CHUNK // HEAD_DIM
    for kc in range(CONV_DIM // K_CHUNK, MIX_DIM // K_CHUNK):
        first = (kc * K_CHUNK - CONV_DIM) // HEAD_DIM
        _memory_attention_into(qm_ref, zm_ref, mk_ref, mv_ref, mixed_ref, n_seq, seq_rows,
                               range(first, first + heads_per_chunk))
        _project_chunk(mixed_ref, w_ref, x_ref, o_ref, kc)


def _project_chunk(mixed_ref, w_ref, x_ref, o_ref, kc, tn=512):
    krows = slice(kc * K_CHUNK, (kc + 1) * K_CHUNK)
    a = mixed_ref[:, krows]
    for c in range(o_ref.shape[1] // tn):
        cols = slice(c * tn, (c + 1) * tn)
        base = x_ref[:, cols] if kc == 0 else o_ref[:, cols]
        o_ref[:, cols] = base + jnp.dot(a, w_ref[krows, cols], preferred_element_type=F32)


def _finish_rows(o_ref, g_ref, h_refs, inv_ref, final_norm, rows_per_chunk=NORM_ROWS):
    n_chunks = o_ref.shape[0] // rows_per_chunk
    chunk = lambda r: pl.ds(pl.multiple_of(r * rows_per_chunk, rows_per_chunk), rows_per_chunk)

    def stats(r, carry):
        x = o_ref[chunk(r), :]
        inv = lax.rsqrt(jnp.mean(x * x, axis=-1, keepdims=True) + EPS)
        inv_ref[chunk(r), :] = jnp.broadcast_to(inv, (rows_per_chunk, LANES))
        return carry

    def scale(r, carry):
        xn = o_ref[chunk(r), :] * _tile_lanes(inv_ref[chunk(r), :], o_ref.shape[1] // LANES)
        if final_norm:
            o_ref[chunk(r), :] = xn * g_ref[0:1, :]
        for k, h_ref in enumerate(h_refs):
            h_ref[chunk(r), :] = (xn * g_ref[k:k + 1, :]).astype(h_ref.dtype)
        return carry

    lax.fori_loop(0, n_chunks, stats, 0, unroll=True)
    lax.fori_loop(0, n_chunks, scale, 0, unroll=True)


def _layer_memory(layer, n_seq):
    return pl.BlockSpec((None, n_seq, N_MEM * N_HEADS_M, HEAD_DIM), lambda b, t: (layer, b, 0, 0))


def _tail_a_kernel(bg_ref, cg_ref, u_ref, zc_ref, cgh_ref, uh_ref, st_ref, qm_ref, zm_ref,
                   mk_ref, mv_ref, cw_ref, w_ref, x_ref, g_ref,
                   o_ref, h1_ref, h2_ref, nst_ref, mixed_ref, inv_ref):
    t = pl.program_id(1)
    n_seq, seq_rows = st_ref.shape[0], bg_ref.shape[0] // st_ref.shape[0]
    first = t == 0
    row = lax.broadcasted_iota(jnp.int32, (seq_rows, LANES), 0)
    hr = cgh_ref.shape[0]
    for c in range(CONV_DIM // LANES):
        cols = slice(c * LANES, (c + 1) * LANES)
        w = cw_ref[:, cols]
        halo = cgh_ref[:, cols].astype(F32) * uh_ref[:, cols].astype(F32)
        for s in range(n_seq):
            rows = slice(s * seq_rows, (s + 1) * seq_rows)
            ci = cg_ref[rows, cols].astype(F32) * u_ref[rows, cols].astype(F32)
            st = st_ref[s, :, cols]
            prev1 = jnp.where(first, st[1:2, :], halo[hr - 1:hr, :])
            prev2 = jnp.where(first, st[0:1, :], halo[hr - 2:hr - 1, :])
            s1 = jnp.where(row == 0, prev1, pltpu.roll(ci, 1, axis=0))
            s2 = jnp.where(row == 0, prev2, jnp.where(row == 1, prev1, pltpu.roll(ci, 2, axis=0)))
            conv = w[0:1, :] * s2 + w[1:2, :] * s1 + w[2:3, :] * ci
            branch = bg_ref[rows, cols].astype(F32) * conv * _silu(zc_ref[rows, cols].astype(F32))
            mixed_ref[rows, cols] = branch.astype(mixed_ref.dtype)
            nst_ref[s, :, cols] = ci[seq_rows - 2:seq_rows, :]

        if (c + 1) * LANES % K_CHUNK == 0:
            _project_chunk(mixed_ref, w_ref, x_ref, o_ref, (c + 1) * LANES // K_CHUNK - 1)

    _memory_branch(qm_ref, zm_ref, mk_ref, mv_ref, mixed_ref, w_ref, x_ref, o_ref, n_seq, seq_rows)
    _finish_rows(o_ref, g_ref, (h1_ref, h2_ref), inv_ref, False)


def _tail_a(proj, state, mem_k, mem_v, conv_w, w_out, x, gains, *, layer, bsz, t_len, tt):
    m = bsz * t_len
    seq_rows = min(tt, t_len)
    n_seq = tt // seq_rows
    nt = t_len // seq_rows
    hr = 16
    assert t_len % seq_rows == 0 and bsz % n_seq == 0 and seq_rows % hr == 0
    tile = lambda b, t: b * nt + t
    wide = lambda k: pl.BlockSpec((tt, CONV_DIM), lambda b, t: (tile(b, t), k))
    halo = lambda k: pl.BlockSpec(
        (hr, CONV_DIM), lambda b, t: (jnp.maximum(tile(b, t) * (tt // hr) - 1, 0), k))
    narrow = lambda k: pl.BlockSpec((tt, M_DIM), lambda b, t: (tile(b, t), k))
    per_seq = lambda *shape: pl.BlockSpec((n_seq,) + shape, lambda b, t: (b,) + (0,) * len(shape))
    row = pl.BlockSpec((tt, D_MODEL), lambda b, t: (tile(b, t), 0))
    q_col = 4 * CONV_DIM // M_DIM
    return pl.pallas_call(
        _tail_a_kernel,
        grid=(bsz // n_seq, nt),
        in_specs=[wide(0), wide(1), wide(2), wide(3), halo(1), halo(2),
                  per_seq(CONV_W - 1, CONV_DIM), narrow(q_col), narrow(q_col + 1),
                  _layer_memory(layer, n_seq), _layer_memory(layer, n_seq),
                  _resident(CONV_W, CONV_DIM),
                  _resident(MIX_DIM, D_MODEL), row, _resident(2, D_MODEL)],
        out_specs=[row, row, row, per_seq(CONV_W - 1, CONV_DIM)],
        out_shape=[jax.ShapeDtypeStruct((m, D_MODEL), F32),
                   jax.ShapeDtypeStruct((m, D_MODEL), BF16),
                   jax.ShapeDtypeStruct((m, D_MODEL), BF16),
                   jax.ShapeDtypeStruct((bsz, CONV_W - 1, CONV_DIM), F32)],
        scratch_shapes=[pltpu.VMEM((tt, MIX_DIM), BF16), pltpu.VMEM((tt, LANES), F32)],
        compiler_params=_params("parallel", "arbitrary"),
        name="tail_a",
    )(proj, proj, proj, proj, proj, proj, state, proj, proj, mem_k, mem_v, conv_w, w_out, x, gains)


def _tail_b_kernel(a_ref, zf_ref, qm_ref, zm_ref, mk_ref, mv_ref, w_ref, x_ref, g_ref,
                   o_ref, mixed_ref, inv_ref):
    n_seq = mk_ref.shape[0]
    seq_rows = a_ref.shape[0] // n_seq
    for c in range(F_DIM // LANES):
        cols = slice(c * LANES, (c + 1) * LANES)
        mixed_ref[:, cols] = (a_ref[:, cols].astype(F32)
                              * _silu(zf_ref[:, cols].astype(F32))).astype(mixed_ref.dtype)
        if (c + 1) * LANES % K_CHUNK == 0:
            _project_chunk(mixed_ref, w_ref, x_ref, o_ref, (c + 1) * LANES // K_CHUNK - 1)
    _memory_branch(qm_ref, zm_ref, mk_ref, mv_ref, mixed_ref, w_ref, x_ref, o_ref, n_seq, seq_rows)
    _finish_rows(o_ref, g_ref, (), inv_ref, True)


def _tail_b(a, rest, mem_k, mem_v, w_out, x, gain, *, layer, bsz, t_len, tt):
    m = bsz * t_len
    seq_rows = min(tt, t_len)
    n_seq = tt // seq_rows
    nt = t_len // seq_rows
    assert t_len % seq_rows == 0 and bsz % n_seq == 0
    tile = lambda b, t: b * nt + t
    narrow = lambda k: pl.BlockSpec((tt, M_DIM), lambda b, t: (tile(b, t), k))
    mem = _layer_memory(layer, n_seq)
    wide = pl.BlockSpec((tt, F_DIM), lambda b, t: (tile(b, t), 0))
    row = pl.BlockSpec((tt, D_MODEL), lambda b, t: (tile(b, t), 0))
    q_col = F_DIM // M_DIM
    return pl.pallas_call(
        _tail_b_kernel,
        grid=(bsz // n_seq, nt),
        in_specs=[wide, wide, narrow(q_col), narrow(q_col + 1), mem, mem,
                  _resident(MIX_DIM, D_MODEL), row, _resident(1, D_MODEL)],
        out_specs=row,
        out_shape=jax.ShapeDtypeStruct((m, D_MODEL), F32),
        scratch_shapes=[pltpu.VMEM((tt, MIX_DIM), BF16), pltpu.VMEM((tt, LANES), F32)],
        compiler_params=_params("parallel", "parallel"),
        name="tail_b",
    )(a, rest, rest, rest, mem_k, mem_v, w_out, x, gain)


def _proj_kernel(h_ref, w_ref, o_ref):
    o_ref[...] = jnp.dot(h_ref[...], w_ref[...].astype(BF16),
                         preferred_element_type=F32).astype(o_ref.dtype)


def _proj(h, w, *, first_col, tm, tn, out_dtype):
    m, d = h.shape
    n = w.shape[1] - first_col
    tm = min(tm, m)
    assert m % tm == 0 and n % tn == 0 and first_col % tn == 0
    return pl.pallas_call(
        _proj_kernel,
        grid=(m // tm, n // tn),
        in_specs=[pl.BlockSpec((tm, d), lambda i, j: (i, 0)),
                  pl.BlockSpec((d, tn), lambda i, j: (0, j + first_col // tn))],
        out_specs=pl.BlockSpec((tm, tn), lambda i, j: (i, j)),
        out_shape=jax.ShapeDtypeStruct((m, n), out_dtype),
        compiler_params=_params("parallel", "arbitrary"),
        name="proj",
    )(h, w)


def _kvf_kernel(h_ref, wkv_ref, wf_ref, bf_ref, c0_ref, *refs,
                seg, tiles_per_seq, tn, emit_vt):
    if emit_vt:
        k_ref, v_ref, logf_ref, c_ref, kx_ref, vt_ref, carry_ref = refs
    else:
        k_ref, v_ref, logf_ref, c_ref, kx_ref, carry_ref = refs
    n_seq, seq_rows = k_ref.shape[0], k_ref.shape[2]
    i = pl.program_id(0)
    tm = h_ref.shape[0]
    h = h_ref[...]

    heads_per_chunk = tn // HEAD_DIM
    n_chunks = F_DIM // tn
    nt_dims = (((1,), (1,)), ((), ()))

    def project(n):
        return lax.dot_general(h, wkv_ref[n * tn:(n + 1) * tn, :], nt_dims,
                               preferred_element_type=F32)

    logit = lax.dot_general(h, wf_ref[...], nt_dims, preferred_element_type=F32) + bf_ref[...]

    for n in range(n_chunks, 2 * n_chunks):
        y = project(n)
        for hh in range(heads_per_chunk):
            head = (n - n_chunks) * heads_per_chunk + hh
            yh = y[:, hh * HEAD_DIM:(hh + 1) * HEAD_DIM]
            if emit_vt:
                vt_ref[0, head, 0, 0:HEAD_DIM, :] = yh.T.astype(BF16)
                one_row = lax.broadcasted_iota(jnp.int32, (VT_ROWS - HEAD_DIM, tm), 0) == 0
                vt_ref[0, head, 0, HEAD_DIM:VT_ROWS, :] = jnp.where(one_row, 1.0, 0.0).astype(BF16)
            for s in range(n_seq):
                v_ref[s, head] = yh[s * seq_rows:(s + 1) * seq_rows]

    logf = jnp.minimum(logit, 0.0) - jnp.log1p(jnp.exp(-jnp.abs(logit)))
    logf_ref[...] = logf[:, :N_HEADS_F]
    pos = lax.broadcasted_iota(jnp.int32, (tm, LANES), 0) & (seg - 1)
    c = logf
    step = 1
    while step < seg:
        c = c + jnp.where(pos >= step, pltpu.roll(c, step, axis=0), 0.0)
        step *= 2
    if tiles_per_seq > 1:
        @pl.when(i % tiles_per_seq == 0)
        def _():
            carry_ref[...] = c0_ref[0:1, :]

        c = c + carry_ref[...]
        carry_ref[...] = c[tm - 1:tm, :]
    else:
        c = c + c0_ref[...]
    c_ref[...] = c

    c2 = c * LOG2E
    for n in range(n_chunks):
        y = project(n)
        for hh in range(heads_per_chunk):
            head = n * heads_per_chunk + hh
            yh = y[:, hh * HEAD_DIM:(hh + 1) * HEAD_DIM]
            yb = yh.astype(BF16)
            aug = _aug_block(c2[:, head:head + 1], True)
            for s in range(n_seq):
                rows = slice(s * seq_rows, (s + 1) * seq_rows)
                k_ref[s, head] = yh[rows]
                kx_ref[s, head, :, 0:HEAD_DIM] = yb[rows]
                kx_ref[s, head, :, HEAD_DIM:AUG_DIM] = aug[rows]


def _kvf(h, w_kv, w_f, b_f, c0, *, bsz, t_len, tm, emit_vt):
    m, d = h.shape
    tm = min(tm, m)
    seg = min(t_len, tm)
    assert m % tm == 0 and seg & (seg - 1) == 0 and (t_len % tm == 0 or tm % t_len == 0)
    tiles_per_seq = max(t_len // tm, 1)
    n_seq = tm // seg
    row = lambda w: pl.BlockSpec((tm, w), lambda i: (i, 0))
    heads = lambda w: pl.BlockSpec((n_seq, N_HEADS_F, seg, w),
                                   lambda i: (i // tiles_per_seq, 0, i % tiles_per_seq, 0))
    out_specs = [heads(HEAD_DIM), heads(HEAD_DIM), row(N_HEADS_F), row(LANES), heads(AUG_DIM)]
    out_shape = [jax.ShapeDtypeStruct((bsz, N_HEADS_F, t_len, HEAD_DIM), F32),
                 jax.ShapeDtypeStruct((bsz, N_HEADS_F, t_len, HEAD_DIM), F32),
                 jax.ShapeDtypeStruct((m, N_HEADS_F), F32),
                 jax.ShapeDtypeStruct((m, LANES), F32),
                 jax.ShapeDtypeStruct((bsz, N_HEADS_F, t_len, AUG_DIM), BF16)]
    if emit_vt:
        assert tiles_per_seq * tm == t_len
        out_specs.append(pl.BlockSpec(
            (1, N_HEADS_F, 1, VT_ROWS, tm),
            lambda i: (i // tiles_per_seq, 0, i % tiles_per_seq, 0, 0)))
        out_shape.append(
            jax.ShapeDtypeStruct((bsz, N_HEADS_F, tiles_per_seq, VT_ROWS, tm), BF16))
    return pl.pallas_call(
        functools.partial(_kvf_kernel, seg=seg, tiles_per_seq=tiles_per_seq, tn=512,
                          emit_vt=emit_vt),
        grid=(m // tm,),
        in_specs=[row(d), _resident(2 * F_DIM, d), _resident(LANES, d), _resident(1, LANES),
                  row(LANES)],
        out_specs=out_specs,
        out_shape=out_shape,
        scratch_shapes=[pltpu.VMEM((1, LANES), F32)],
        compiler_params=_params("arbitrary"),
        name="kvf",
    )(h, w_kv, w_f, b_f, c0)


def _qproj_kernel(h_ref, wq_ref, c_ref, qx_ref, *, tn):
    h = h_ref[...]
    c = c_ref[...] * LOG2E
    heads_per_chunk = tn // HEAD_DIM
    for n in range(F_DIM // tn):
        y = jnp.dot(h, wq_ref[:, n * tn:(n + 1) * tn],
                    preferred_element_type=F32) * (SCALE * LOG2E)
        for hh in range(heads_per_chunk):
            head = n * heads_per_chunk + hh
            base = head * AUG_DIM
            qx_ref[:, base:base + HEAD_DIM] = y[:, hh * HEAD_DIM:(hh + 1) * HEAD_DIM].astype(BF16)
            qx_ref[:, base + HEAD_DIM:base + AUG_DIM] = _aug_block(c[:, head:head + 1], False)


def _qproj(h, w_q, c, *, tm):
    m, d = h.shape
    tm = min(tm, m)
    assert m % tm == 0
    return pl.pallas_call(
        functools.partial(_qproj_kernel, tn=512),
        grid=(m // tm,),
        in_specs=[pl.BlockSpec((tm, d), lambda i: (i, 0)),
                  _resident(d, F_DIM),
                  pl.BlockSpec((tm, LANES), lambda i: (i, 0))],
        out_specs=pl.BlockSpec((tm, N_HEADS_F * AUG_DIM), lambda i: (i, 0)),
        out_shape=jax.ShapeDtypeStruct((m, N_HEADS_F * AUG_DIM), BF16),
        compiler_params=_params("parallel"),
        name="qproj",
    )(h, w_q, c)


def _fox_prompt_kernel(qx_ref, kx_ref, vt_ref, o_ref, m_ref, acc_ref, sa_ref, sb_ref, ma_ref, mb_ref,
                       *, cw):
    qi = pl.program_id(2)
    tq = qx_ref.shape[0]
    tk = sa_ref.shape[1]
    tkv = vt_ref.shape[4]
    n_chunks = tq // cw
    buf_a, buf_b = (sa_ref, ma_ref), (sb_ref, mb_ref)
    m_ref[...] = jnp.full_like(m_ref, -jnp.inf)
    acc_ref[...] = jnp.zeros_like(acc_ref)

    def scores(kb, visible, buf):
        s_ref, smax_ref = buf
        for c, (keys, key_minus_query) in visible.items():
            k = kx_ref[0, 0, pl.ds(pl.multiple_of(kb * tk, tk), keys), :]
            q = qx_ref[c * cw:(c + 1) * cw, :]
            s = lax.dot_general(k, q, (((1,), (1,)), ((), ())), preferred_element_type=F32)
            if key_minus_query is not None:
                key = lax.broadcasted_iota(jnp.int32, (keys, cw), 0) + key_minus_query
                query = lax.broadcasted_iota(jnp.int32, (keys, cw), 1)
                s = jnp.where(key <= query, s, -jnp.inf)
            s_ref[c, 0:keys, :] = s
            smax_ref[c] = jnp.max(s, axis=0, keepdims=True)

    def update(kb, visible, buf):
        s_ref, smax_ref = buf
        for c, (keys, _) in visible.items():
            m_old = m_ref[c]
            m_new = jnp.maximum(m_old, smax_ref[c])
            alpha = jnp.exp2(m_old - m_new)
            acc = alpha * acc_ref[c]
            for lo in range(0, keys, tkv):
                n = min(tkv, keys - lo)
                p = jnp.exp2(s_ref[c, lo:lo + n, :] - m_new).astype(BF16)
                acc = acc + jnp.dot(vt_ref[0, 0, kb * (tk // tkv) + lo // tkv, :, 0:n], p,
                                    preferred_element_type=F32)
            acc_ref[c] = acc
            m_ref[c] = m_new

    every = {c: (tk, None) for c in range(n_chunks)}
    own = []
    for jj in range(2):
        visible = {}
        for c in range(n_chunks):
            k_lo, q_lo = jj * tk, c * cw
            keys = min(tk, q_lo + cw - k_lo)
            if keys > 0:
                visible[c] = (keys, k_lo - q_lo if k_lo + keys - 1 > q_lo else None)
        own.append(visible)

    @pl.when(qi == 0)
    def _():
        scores(0, own[0], buf_a)
        scores(1, own[1], buf_b)
        update(0, own[0], buf_a)
        update(1, own[1], buf_b)

    @pl.when(qi > 0)
    def _():
        n = 2 * qi
        scores(0, every, buf_a)

        def pair(j):
            scores(j, every, buf_b)
            update(j - 1, every, buf_a)
            scores(j + 1, every, buf_a)
            update(j, every, buf_b)

        def two_pairs(t, carry):
            pair(4 * t + 1)
            pair(4 * t + 3)
            return carry

        n_pairs = qi - 1
        lax.fori_loop(0, n_pairs // 2, two_pairs, 0)

        @pl.when(n_pairs % 2 == 1)
        def _():
            pair(n - 3)

        scores(n - 1, every, buf_b)
        update(n - 2, every, buf_a)
        scores(n, own[0], buf_a)
        update(n - 1, every, buf_b)
        scores(n + 1, own[1], buf_b)
        update(n, own[0], buf_a)
        update(n + 1, own[1], buf_b)

    for c in range(n_chunks):
        acc = acc_ref[c]
        o = acc[0:HEAD_DIM, :] / acc[HEAD_DIM:HEAD_DIM + 1, :]
        o_ref[c * cw:(c + 1) * cw, :] = o.T.astype(o_ref.dtype)


def _fox_prompt(qx, kx, vt, *, bsz, t_len, tq, cw):
    m = bsz * t_len
    nq = t_len // tq
    tk = tq // 2
    tkv = vt.shape[4]
    n_chunks = tq // cw
    assert t_len % tq == 0 and tk % tkv == 0 and tq % cw == 0
    return pl.pallas_call(
        functools.partial(_fox_prompt_kernel, cw=cw),
        grid=(bsz, N_HEADS_F, nq),
        in_specs=[pl.BlockSpec((tq, AUG_DIM), lambda b, h, qi: (b * nq + qi, h)),
                  pl.BlockSpec((1, 1, t_len, AUG_DIM), lambda b, h, qi: (b, h, 0, 0)),
                  pl.BlockSpec((1, 1, t_len // tkv, VT_ROWS, tkv),
                               lambda b, h, qi: (b, h, 0, 0, 0))],
        out_specs=pl.BlockSpec((tq, HEAD_DIM), lambda b, h, qi: (b * nq + qi, h)),
        out_shape=jax.ShapeDtypeStruct((m, F_DIM), BF16),
        scratch_shapes=[pltpu.VMEM((n_chunks, 1, cw), F32),
                        pltpu.VMEM((n_chunks, VT_ROWS, cw), F32),
                        pltpu.VMEM((n_chunks, tk, cw), F32), pltpu.VMEM((n_chunks, tk, cw), F32),
                        pltpu.VMEM((n_chunks, 1, cw), F32), pltpu.VMEM((n_chunks, 1, cw), F32)],
        compiler_params=_params("parallel", "parallel", "arbitrary"),
        name="fox_prompt",
    )(qx, kx, vt)


def _fox_cached_kernel(qx_ref, kxn_ref, vn_ref, cn_ref, ck_ref, cv_ref, cp_ref, o_ref,
                       m_ref, l_ref, acc_ref, cq_ref, smax_ref, s_ref):
    b = pl.program_id(0)
    ki = pl.program_id(1)
    t_new = qx_ref.shape[0]

    @pl.when(ki == 0)
    def _():
        m_ref[...] = jnp.full_like(m_ref, -jnp.inf)
        l_ref[...] = jnp.zeros_like(l_ref)
        acc_ref[...] = jnp.zeros_like(acc_ref)
        for h in range(N_HEADS_F):
            cq_ref[h] = jnp.broadcast_to(cn_ref[:, h:h + 1] * LOG2E, (t_new, LANES))

    def update(h, s, s_max, v):
        n = s.shape[1]
        across = (lambda x: _tile_lanes(x, n // LANES)) if n >= LANES else (lambda x: x[:, :n])
        m_old = m_ref[h]
        m_new = jnp.maximum(m_old, s_max)
        alpha = jnp.exp2(m_old - m_new)
        p = jnp.exp2(s - across(m_new))
        l_ref[h] = alpha * l_ref[h] + jnp.sum(p, axis=-1, keepdims=True)
        acc_ref[h] = alpha * acc_ref[h] + jnp.dot(p.astype(BF16), v, preferred_element_type=F32)
        m_ref[h] = m_new

    tk = ck_ref.shape[2]
    for h in range(N_HEADS_F):
        q = qx_ref[:, h * AUG_DIM:h * AUG_DIM + HEAD_DIM]
        k = ck_ref[0, h].astype(BF16)
        s = lax.dot_general(q, k, (((1,), (1,)), ((), ())), preferred_element_type=F32)
        decay = _tile_lanes(cq_ref[h], tk // LANES) - cp_ref[h, pl.ds(b, 1), :] * LOG2E
        s = s + decay
        s_ref[h] = s
        smax_ref[h] = jnp.broadcast_to(jnp.max(s, axis=-1, keepdims=True), (t_new, LANES))
    for h in range(N_HEADS_F):
        update(h, s_ref[h], smax_ref[h], cv_ref[0, h].astype(BF16))

    @pl.when(ki == pl.num_programs(1) - 1)
    def _():
        row = lax.broadcasted_iota(jnp.int32, (t_new, t_new), 0)
        col = lax.broadcasted_iota(jnp.int32, (t_new, t_new), 1)
        for h in range(N_HEADS_F):
            xcols = slice(h * AUG_DIM, (h + 1) * AUG_DIM)
            s = lax.dot_general(qx_ref[:, xcols], kxn_ref[0, h], (((1,), (1,)), ((), ())),
                                preferred_element_type=F32)
            s = jnp.where(col <= row, s, -jnp.inf)
            s_ref[h, :, 0:t_new] = s
            smax_ref[h] = jnp.broadcast_to(jnp.max(s, axis=-1, keepdims=True), (t_new, LANES))
        for h in range(N_HEADS_F):
            cols = slice(h * HEAD_DIM, (h + 1) * HEAD_DIM)
            update(h, s_ref[h, :, 0:t_new], smax_ref[h], vn_ref[0, h].astype(BF16))
            o_ref[:, cols] = (acc_ref[h] / l_ref[h]).astype(o_ref.dtype)


def _fox_cached(qx, kx_new, v_new, c_new, cache_k, cache_v, c_past, *, bsz, t_len, tk):
    m = bsz * t_len
    past = cache_k.shape[2]
    assert past % tk == 0 and past > 0
    seq = lambda w: pl.BlockSpec((t_len, w), lambda b, ki: (b, 0))
    new = lambda w: pl.BlockSpec((1, N_HEADS_F, t_len, w), lambda b, ki: (b, 0, 0, 0))
    cache = pl.BlockSpec((1, N_HEADS_F, tk, HEAD_DIM), lambda b, ki: (b, 0, ki, 0))
    return pl.pallas_call(
        _fox_cached_kernel,
        grid=(bsz, past // tk),
        in_specs=[seq(N_HEADS_F * AUG_DIM), new(AUG_DIM), new(HEAD_DIM), seq(LANES),
                  cache, cache,
                  pl.BlockSpec((N_HEADS_F, bsz, tk), lambda b, ki: (0, 0, ki))],
        out_specs=seq(F_DIM),
        out_shape=jax.ShapeDtypeStruct((m, F_DIM), BF16),
        scratch_shapes=[pltpu.VMEM((N_HEADS_F, t_len, LANES), F32) for _ in range(5)]
        + [pltpu.VMEM((N_HEADS_F, t_len, tk), F32)],
        compiler_params=_params("parallel", "arbitrary"),
        name="fox_cached",
    )(qx, kx_new, v_new, c_new, cache_k, cache_v, c_past)


def _cumsum_lanes_kernel(x_ref, o_ref):
    rows, n = x_ref.shape
    r = lax.broadcasted_iota(jnp.int32, (LANES, LANES), 0)
    c = lax.broadcasted_iota(jnp.int32, (LANES, LANES), 1)
    upper = jnp.where(r <= c, 1.0, 0.0).astype(BF16)
    carry = jnp.zeros((rows, 1), F32)
    for j in range(n // LANES):
        cols = slice(j * LANES, (j + 1) * LANES)
        hi, mid, lo = _split3(x_ref[:, cols])
        local = (jnp.dot(hi.astype(BF16), upper, preferred_element_type=F32)
                 + jnp.dot(mid.astype(BF16), upper, preferred_element_type=F32)
                 + jnp.dot(lo.astype(BF16), upper, preferred_element_type=F32))
        o_ref[:, cols] = local + carry
        carry = carry + local[:, LANES - 1:LANES]


def _cumsum_lanes(x):
    rows, n = x.shape
    return pl.pallas_call(
        _cumsum_lanes_kernel,
        grid=(1,),
        in_specs=[pl.BlockSpec((rows, n), lambda i: (0, 0))],
        out_specs=pl.BlockSpec((rows, n), lambda i: (0, 0)),
        out_shape=jax.ShapeDtypeStruct((rows, n), F32),
        compiler_params=_params("arbitrary"),
        name="cumsum_lanes",
    )(x)


class _Tiles(NamedTuple):
    proj_rows: int
    proj_cols: int
    rest_rows: int
    rest_cols: int
    head_rows: int
    tail_rows: int
    attn_queries: int
    attn_chunk: int
    cache_keys: int


def _tiles(t_len):
    tail_rows = 512 if t_len >= 512 else 256
    return _Tiles(proj_rows=1024, proj_cols=1024, rest_rows=2048, rest_cols=512, head_rows=512,
                  tail_rows=tail_rows, attn_queries=1024, attn_chunk=256, cache_keys=1024)


def _trunk(x, conv_state, mem_k, mem_v, past, w):
    bsz, t_len, d = x.shape
    m = bsz * t_len
    x0 = x.reshape(m, d)
    tiles = _tiles(t_len)

    proj = _norm_proj(x0, w["g_norm"][0], w["w_in_a"], tm=tiles.proj_rows, tn=tiles.proj_cols,
                      out_dtype=BF16)
    x1, h1, h_kv, new_state = _tail_a(
        proj, conv_state, mem_k, mem_v, w["conv_w"], w["w_out"][0], x0,
        jnp.stack([w["g_norm"][1], w["g_kv"]]), layer=0, bsz=bsz, t_len=t_len,
        tt=tiles.tail_rows)

    if past is None:
        c0 = jnp.zeros((m, LANES), F32)
    else:
        cache_k, cache_v, cache_logf = past
        past_len = cache_k.shape[1]
        cache_k = jnp.transpose(cache_k, (0, 2, 1, 3))
        cache_v = jnp.transpose(cache_v, (0, 2, 1, 3))
        logf_t = jnp.transpose(cache_logf, (2, 0, 1)).reshape(N_HEADS_F * bsz, past_len)
        c_past = _cumsum_lanes(logf_t).reshape(N_HEADS_F, bsz, past_len)
        c_end = jnp.pad(c_past[:, :, past_len - 1].T, ((0, 0), (0, LANES - N_HEADS_F)))
        c0 = jnp.repeat(c_end, t_len, axis=0)
    kvf_out = _kvf(h_kv, w["w_kv"], w["w_f"], w["b_f"], c0,
                   bsz=bsz, t_len=t_len, tm=tiles.head_rows, emit_vt=past is None)
    k_new, v_new, logf, c, kx = kvf_out[:5]
    qx = _qproj(h1, w["w_q"], c, tm=tiles.head_rows)
    rest = _proj(h1, w["w_in_b"], first_col=F_DIM, tm=tiles.rest_rows, tn=tiles.rest_cols,
                 out_dtype=BF16)

    if past is None:
        o = _fox_prompt(qx, kx, kvf_out[5], bsz=bsz, t_len=t_len, tq=tiles.attn_queries,
                        cw=tiles.attn_chunk)
    else:
        o = _fox_cached(qx, kx, v_new, c, cache_k, cache_v, c_past,
                        bsz=bsz, t_len=t_len, tk=tiles.cache_keys)
    y = _tail_b(o, rest, mem_k, mem_v, w["w_out"][1], x1, w["g_final"][None],
                layer=1, bsz=bsz, t_len=t_len, tt=tiles.tail_rows)
    return (y.reshape(bsz, t_len, d), new_state[None],
            jnp.transpose(k_new, (0, 2, 1, 3)), jnp.transpose(v_new, (0, 2, 1, 3)),
            logf.reshape(bsz, t_len, N_HEADS_F))


def kernel(x_prompt, x_sample, state_conv, cache_k, cache_v, cache_logf, cache_mem_k, cache_mem_v,
           mem_prompt, g_norm, w_in_a, conv_w, w_in_b, w_out, g_mem, w_mem_kv, g_kv, w_kvf, b_f,
           g_final):
    depth = g_norm.shape[0]
    bp = x_prompt.shape[0]
    bs = x_sample.shape[0]
    w = {
        "g_norm": g_norm, "g_kv": g_kv, "g_final": g_final, "conv_w": conv_w[0],
        "w_in_a": w_in_a[0],
        "w_q": w_in_b[0][:, :F_DIM].astype(BF16),
        "w_in_b": w_in_b[0],
        "w_out": w_out.astype(BF16),
        "w_kv": w_kvf.T[:2 * F_DIM].astype(BF16),
        "w_f": jnp.pad(w_kvf.T[2 * F_DIM:], ((0, LANES - N_HEADS_F), (0, 0))).astype(BF16),
        "b_f": jnp.pad(b_f, (0, LANES - N_HEADS_F)).reshape(1, LANES),
    }

    mem_rows = mem_prompt.reshape(bp * N_MEM, D_MODEL)
    mem_kv = [_norm_proj(mem_rows, g_mem[i], w_mem_kv[i], tm=512, tn=512)
              for i in range(depth)]
    p_mem_k = jnp.stack([a[:, :M_DIM] for a in mem_kv]).reshape(depth, bp, N_MEM, N_HEADS_M, HEAD_DIM)
    p_mem_v = jnp.stack([a[:, M_DIM:] for a in mem_kv]).reshape(depth, bp, N_MEM, N_HEADS_M, HEAD_DIM)

    zero_conv = jnp.zeros((bp, CONV_W - 1, CONV_DIM), F32)
    y_p, p_state, p_k, p_v, p_logf = _trunk(
        x_prompt, zero_conv, p_mem_k.reshape(depth, bp, N_MEM * N_HEADS_M, HEAD_DIM),
        p_mem_v.reshape(depth, bp, N_MEM * N_HEADS_M, HEAD_DIM), None, w)
    y_s, s_state, s_k, s_v, s_logf = _trunk(
        x_sample, state_conv[0], cache_mem_k.reshape(depth, bs, N_MEM * N_HEADS_M, HEAD_DIM),
        cache_mem_v.reshape(depth, bs, N_MEM * N_HEADS_M, HEAD_DIM),
        (cache_k, cache_v, cache_logf), w)
    return (y_p, y_s, p_state, p_k, p_v, p_logf, p_mem_k, p_mem_v, s_state, s_k, s_v, s_logf)
```

```python
import functools
import math
from typing import NamedTuple

import jax
import jax.numpy as jnp
from jax import lax
from jax.experimental import pallas as pl
from jax.experimental.pallas import tpu as pltpu

F32 = jnp.float32
BF16 = jnp.bfloat16

D_MODEL = 2048
CONV_W = 3
CONV_DIM = 1536
HEAD_DIM = 128
N_HEADS_F = 12
F_DIM = N_HEADS_F * HEAD_DIM
N_MEM = 256
N_HEADS_M = 4
M_DIM = N_HEADS_M * HEAD_DIM
MIX_DIM = CONV_DIM + M_DIM
EPS = 1e-6
SCALE = 1.0 / math.sqrt(HEAD_DIM)
LOG2E = math.log2(math.e)

LANES = 128
AUG_DIM = 2 * HEAD_DIM
VT_ROWS = HEAD_DIM + 16
VMEM_LIMIT = 58 * 1024 * 1024
NORM_ROWS = 64


def _params(*sem):
    return pltpu.CompilerParams(dimension_semantics=sem, vmem_limit_bytes=VMEM_LIMIT)


def _resident(*shape):
    return pl.BlockSpec(shape, lambda *_: (0,) * len(shape), pipeline_mode=pl.Buffered(1))


def _tile_lanes(x, n):
    return jnp.concatenate([x] * n, axis=1)


def _silu(z):
    return z * (1.0 / (1.0 + jnp.exp(-z)))


def _split3(c):
    hi = c.astype(BF16).astype(F32)
    r = c - hi
    mid = r.astype(BF16).astype(F32)
    lo = (r - mid).astype(BF16).astype(F32)
    return hi, mid, lo


def _aug_block(c_col, key_side):
    rows = c_col.shape[0]
    hi, mid, lo = _split3(-c_col if key_side else c_col)
    lane = lax.broadcasted_iota(jnp.int32, (rows, LANES), 1)
    term0 = 3 if key_side else 0
    one0 = 0 if key_side else 3
    terms = jnp.where(lane == term0, hi,
                      jnp.where(lane == term0 + 1, mid, jnp.where(lane == term0 + 2, lo, 0.0)))
    return jnp.where((lane >= one0) & (lane < one0 + 3), 1.0, terms).astype(BF16)


def _rmsnorm_rows(x_ref, g_ref, h_ref, rows_per_chunk=NORM_ROWS):
    tm = x_ref.shape[0]
    g = g_ref[...]

    def body(r, carry):
        rows = pl.ds(pl.multiple_of(r * rows_per_chunk, rows_per_chunk), rows_per_chunk)
        x = x_ref[rows, :]
        ms = jnp.mean(x * x, axis=-1, keepdims=True)
        h_ref[rows, :] = (x * lax.rsqrt(ms + EPS) * g).astype(h_ref.dtype)
        return carry

    lax.fori_loop(0, tm // rows_per_chunk, body, 0)


def _norm_proj_kernel(x_ref, g_ref, w_ref, o_ref, h_ref):
    @pl.when(pl.program_id(1) == 0)
    def _():
        _rmsnorm_rows(x_ref, g_ref, h_ref)

    o_ref[...] = jnp.dot(h_ref[...], w_ref[...].astype(BF16),
                         preferred_element_type=F32).astype(o_ref.dtype)


def _norm_proj(x, g, w, *, tm, tn, out_dtype=F32):
    m, d = x.shape
    n = w.shape[1]
    tm = min(tm, m)
    assert m % tm == 0 and n % tn == 0
    return pl.pallas_call(
        _norm_proj_kernel,
        grid=(m // tm, n // tn),
        in_specs=[
            pl.BlockSpec((tm, d), lambda i, j: (i, 0)),
            pl.BlockSpec((1, d), lambda i, j: (0, 0)),
            pl.BlockSpec((d, tn), lambda i, j: (0, j)),
        ],
        out_specs=pl.BlockSpec((tm, tn), lambda i, j: (i, j)),
        out_shape=jax.ShapeDtypeStruct((m, n), out_dtype),
        scratch_shapes=[pltpu.VMEM((tm, d), BF16)],
        compiler_params=_params("parallel", "arbitrary"),
        name="norm_proj",
    )(x, g.reshape(1, d), w)


K_CHUNK = 512


def _memory_attention_into(qm_ref, zm_ref, mk_ref, mv_ref, mixed_ref, n_seq, seq_rows, heads):
    for s in range(n_seq):
        rows = slice(s * seq_rows, (s + 1) * seq_rows)
        for h in heads:
            cols = slice(h * HEAD_DIM, (h + 1) * HEAD_DIM)
            q = (qm_ref[rows, cols].astype(F32) * SCALE).astype(BF16)
            k = mk_ref.at[s][pl.ds(h, N_MEM, stride=N_HEADS_M), :].astype(BF16)
            v = mv_ref.at[s][pl.ds(h, N_MEM, stride=N_HEADS_M), :].astype(BF16)
            sc = lax.dot_general(q, k, (((1,), (1,)), ((), ())), preferred_element_type=F32)
            p = jnp.exp(sc - jnp.max(sc, axis=-1, keepdims=True))
            l = jnp.sum(p, axis=-1, keepdims=True)
            o = jnp.dot(p.astype(BF16), v, preferred_element_type=F32) / l
            out_cols = slice(CONV_DIM + h * HEAD_DIM, CONV_DIM + (h + 1) * HEAD_DIM)
            mixed_ref[rows, out_cols] = (o * _silu(zm_ref[rows, cols].astype(F32))).astype(
                mixed_ref.dtype)


def _memory_branch(qm_ref, zm_ref, mk_ref, mv_ref, mixed_ref, w_ref, x_ref, o_ref, n_seq, seq_rows):
    heads_per_chunk = K_CHUNK // HEAD_DIM
    for kc in range(CONV_DIM // K_CHUNK, MIX_DIM // K_CHUNK):
        first = (kc * K_CHUNK - CONV_DIM) // HEAD_DIM
        _memory_attention_into(qm_ref, zm_ref, mk_ref, mv_ref, mixed_ref, n_seq, seq_rows,
                               range(first, first + heads_per_chunk))
        _project_chunk(mixed_ref, w_ref, x_ref, o_ref, kc)


def _project_chunk(mixed_ref, w_ref, x_ref, o_ref, kc, tn=512):
    krows = slice(kc * K_CHUNK, (kc + 1) * K_CHUNK)
    a = mixed_ref[:, krows]
    for c in range(o_ref.shape[1] // tn):
        cols = slice(c * tn, (c + 1) * tn)
        base = x_ref[:, cols] if kc == 0 else o_ref[:, cols]
        o_ref[:, cols] = base + jnp.dot(a, w_ref[krows, cols], preferred_element_type=F32)


def _finish_rows(o_ref, g_ref, h_refs, inv_ref, final_norm, rows_per_chunk=NORM_ROWS):
    n_chunks = o_ref.shape[0] // rows_per_chunk
    chunk = lambda r: pl.ds(pl.multiple_of(r * rows_per_chunk, rows_per_chunk), rows_per_chunk)

    def stats(r, carry):
        x = o_ref[chunk(r), :]
        inv = lax.rsqrt(jnp.mean(x * x, axis=-1, keepdims=True) + EPS)
        inv_ref[chunk(r), :] = jnp.broadcast_to(inv, (rows_per_chunk, LANES))
        return carry

    def scale(r, carry):
        xn = o_ref[chunk(r), :] * _tile_lanes(inv_ref[chunk(r), :], o_ref.shape[1] // LANES)
        if final_norm:
            o_ref[chunk(r), :] = xn * g_ref[0:1, :]
        for k, h_ref in enumerate(h_refs):
            h_ref[chunk(r), :] = (xn * g_ref[k:k + 1, :]).astype(h_ref.dtype)
        return carry

    lax.fori_loop(0, n_chunks, stats, 0, unroll=True)
    lax.fori_loop(0, n_chunks, scale, 0, unroll=True)


def _layer_memory(layer, n_seq):
    return pl.BlockSpec((None, n_seq, N_MEM * N_HEADS_M, HEAD_DIM), lambda b, t: (layer, b, 0, 0))


def _tail_a_kernel(bg_ref, cg_ref, u_ref, zc_ref, cgh_ref, uh_ref, st_ref, qm_ref, zm_ref,
                   mk_ref, mv_ref, cw_ref, w_ref, x_ref, g_ref,
                   o_ref, h1_ref, h2_ref, nst_ref, mixed_ref, inv_ref):
    t = pl.program_id(1)
    n_seq, seq_rows = st_ref.shape[0], bg_ref.shape[0] // st_ref.shape[0]
    first = t == 0
    row = lax.broadcasted_iota(jnp.int32, (seq_rows, LANES), 0)
    hr = cgh_ref.shape[0]
    for c in range(CONV_DIM // LANES):
        cols = slice(c * LANES, (c + 1) * LANES)
        w = cw_ref[:, cols]
        halo = cgh_ref[:, cols].astype(F32) * uh_ref[:, cols].astype(F32)
        for s in range(n_seq):
            rows = slice(s * seq_rows, (s + 1) * seq_rows)
            ci = cg_ref[rows, cols].astype(F32) * u_ref[rows, cols].astype(F32)
            st = st_ref[s, :, cols]
            prev1 = jnp.where(first, st[1:2, :], halo[hr - 1:hr, :])
            prev2 = jnp.where(first, st[0:1, :], halo[hr - 2:hr - 1, :])
            s1 = jnp.where(row == 0, prev1, pltpu.roll(ci, 1, axis=0))
            s2 = jnp.where(row == 0, prev2, jnp.where(row == 1, prev1, pltpu.roll(ci, 2, axis=0)))
            conv = w[0:1, :] * s2 + w[1:2, :] * s1 + w[2:3, :] * ci
            branch = bg_ref[rows, cols].astype(F32) * conv * _silu(zc_ref[rows, cols].astype(F32))
            mixed_ref[rows, cols] = branch.astype(mixed_ref.dtype)
            nst_ref[s, :, cols] = ci[seq_rows - 2:seq_rows, :]

        if (c + 1) * LANES % K_CHUNK == 0:
            _project_chunk(mixed_ref, w_ref, x_ref, o_ref, (c + 1) * LANES // K_CHUNK - 1)

    _memory_branch(qm_ref, zm_ref, mk_ref, mv_ref, mixed_ref, w_ref, x_ref, o_ref, n_seq, seq_rows)
    _finish_rows(o_ref, g_ref, (h1_ref, h2_ref), inv_ref, False)


def _tail_a(proj, state, mem_k, mem_v, conv_w, w_out, x, gains, *, layer, bsz, t_len, tt):
    m = bsz * t_len
    seq_rows = min(tt, t_len)
    n_seq = tt // seq_rows
    nt = t_len // seq_rows
    hr = 16
    assert t_len % seq_rows == 0 and bsz % n_seq == 0 and seq_rows % hr == 0
    tile = lambda b, t: b * nt + t
    wide = lambda k: pl.BlockSpec((tt, CONV_DIM), lambda b, t: (tile(b, t), k))
    halo = lambda k: pl.BlockSpec(
        (hr, CONV_DIM), lambda b, t: (jnp.maximum(tile(b, t) * (tt // hr) - 1, 0), k))
    narrow = lambda k: pl.BlockSpec((tt, M_DIM), lambda b, t: (tile(b, t), k))
    per_seq = lambda *shape: pl.BlockSpec((n_seq,) + shape, lambda b, t: (b,) + (0,) * len(shape))
    row = pl.BlockSpec((tt, D_MODEL), lambda b, t: (tile(b, t), 0))
    q_col = 4 * CONV_DIM // M_DIM
    return pl.pallas_call(
        _tail_a_kernel,
        grid=(bsz // n_seq, nt),
        in_specs=[wide(0), wide(1), wide(2), wide(3), halo(1), halo(2),
                  per_seq(CONV_W - 1, CONV_DIM), narrow(q_col), narrow(q_col + 1),
                  _layer_memory(layer, n_seq), _layer_memory(layer, n_seq),
                  _resident(CONV_W, CONV_DIM),
                  _resident(MIX_DIM, D_MODEL), row, _resident(2, D_MODEL)],
        out_specs=[row, row, row, per_seq(CONV_W - 1, CONV_DIM)],
        out_shape=[jax.ShapeDtypeStruct((m, D_MODEL), F32),
                   jax.ShapeDtypeStruct((m, D_MODEL), BF16),
                   jax.ShapeDtypeStruct((m, D_MODEL), BF16),
                   jax.ShapeDtypeStruct((bsz, CONV_W - 1, CONV_DIM), F32)],
        scratch_shapes=[pltpu.VMEM((tt, MIX_DIM), BF16), pltpu.VMEM((tt, LANES), F32)],
        compiler_params=_params("parallel", "arbitrary"),
        name="tail_a",
    )(proj, proj, proj, proj, proj, proj, state, proj, proj, mem_k, mem_v, conv_w, w_out, x, gains)


def _tail_b_kernel(a_ref, zf_ref, qm_ref, zm_ref, mk_ref, mv_ref, w_ref, x_ref, g_ref,
                   o_ref, mixed_ref, inv_ref):
    n_seq = mk_ref.shape[0]
    seq_rows = a_ref.shape[0] // n_seq
    for c in range(F_DIM // LANES):
        cols = slice(c * LANES, (c + 1) * LANES)
        mixed_ref[:, cols] = (a_ref[:, cols].astype(F32)
                              * _silu(zf_ref[:, cols].astype(F32))).astype(mixed_ref.dtype)
        if (c + 1) * LANES % K_CHUNK == 0:
            _project_chunk(mixed_ref, w_ref, x_ref, o_ref, (c + 1) * LANES // K_CHUNK - 1)
    _memory_branch(qm_ref, zm_ref, mk_ref, mv_ref, mixed_ref, w_ref, x_ref, o_ref, n_seq, seq_rows)
    _finish_rows(o_ref, g_ref, (), inv_ref, True)


def _tail_b(a, rest, mem_k, mem_v, w_out, x, gain, *, layer, bsz, t_len, tt):
    m = bsz * t_len
    seq_rows = min(tt, t_len)
    n_seq = tt // seq_rows
    nt = t_len // seq_rows
    assert t_len % seq_rows == 0 and bsz % n_seq == 0
    tile = lambda b, t: b * nt + t
    narrow = lambda k: pl.BlockSpec((tt, M_DIM), lambda b, t: (tile(b, t), k))
    mem = _layer_memory(layer, n_seq)
    wide = pl.BlockSpec((tt, F_DIM), lambda b, t: (tile(b, t), 0))
    row = pl.BlockSpec((tt, D_MODEL), lambda b, t: (tile(b, t), 0))
    q_col = F_DIM // M_DIM
    return pl.pallas_call(
        _tail_b_kernel,
        grid=(bsz // n_seq, nt),
        in_specs=[wide, wide, narrow(q_col), narrow(q_col + 1), mem, mem,
                  _resident(MIX_DIM, D_MODEL), row, _resident(1, D_MODEL)],
        out_specs=row,
        out_shape=jax.ShapeDtypeStruct((m, D_MODEL), F32),
        scratch_shapes=[pltpu.VMEM((tt, MIX_DIM), BF16), pltpu.VMEM((tt, LANES), F32)],
        compiler_params=_params("parallel", "parallel"),
        name="tail_b",
    )(a, rest, rest, rest, mem_k, mem_v, w_out, x, gain)


def _proj_kernel(h_ref, w_ref, o_ref):
    o_ref[...] = jnp.dot(h_ref[...], w_ref[...].astype(BF16),
                         preferred_element_type=F32).astype(o_ref.dtype)


def _proj(h, w, *, first_col, tm, tn, out_dtype):
    m, d = h.shape
    n = w.shape[1] - first_col
    tm = min(tm, m)
    assert m % tm == 0 and n % tn == 0 and first_col % tn == 0
    return pl.pallas_call(
        _proj_kernel,
        grid=(m // tm, n // tn),
        in_specs=[pl.BlockSpec((tm, d), lambda i, j: (i, 0)),
                  pl.BlockSpec((d, tn), lambda i, j: (0, j + first_col // tn))],
        out_specs=pl.BlockSpec((tm, tn), lambda i, j: (i, j)),
        out_shape=jax.ShapeDtypeStruct((m, n), out_dtype),
        compiler_params=_params("parallel", "arbitrary"),
        name="proj",
    )(h, w)


def _kvf_kernel(h_ref, wkv_ref, wf_ref, bf_ref, c0_ref, *refs,
                seg, tiles_per_seq, tn, emit_vt):
    if emit_vt:
        k_ref, v_ref, logf_ref, c_ref, kx_ref, vt_ref, carry_ref = refs
    else:
        k_ref, v_ref, logf_ref, c_ref, kx_ref, carry_ref = refs
    n_seq, seq_rows = k_ref.shape[0], k_ref.shape[2]
    i = pl.program_id(0)
    tm = h_ref.shape[0]
    h = h_ref[...]

    heads_per_chunk = tn // HEAD_DIM
    n_chunks = F_DIM // tn
    nt_dims = (((1,), (1,)), ((), ()))

    def project(n):
        return lax.dot_general(h, wkv_ref[n * tn:(n + 1) * tn, :], nt_dims,
                               preferred_element_type=F32)

    logit = lax.dot_general(h, wf_ref[...], nt_dims, preferred_element_type=F32) + bf_ref[...]

    for n in range(n_chunks, 2 * n_chunks):
        y = project(n)
        for hh in range(heads_per_chunk):
            head = (n - n_chunks) * heads_per_chunk + hh
            yh = y[:, hh * HEAD_DIM:(hh + 1) * HEAD_DIM]
            if emit_vt:
                vt_ref[0, head, 0, 0:HEAD_DIM, :] = yh.T.astype(BF16)
                one_row = lax.broadcasted_iota(jnp.int32, (VT_ROWS - HEAD_DIM, tm), 0) == 0
                vt_ref[0, head, 0, HEAD_DIM:VT_ROWS, :] = jnp.where(one_row, 1.0, 0.0).astype(BF16)
            for s in range(n_seq):
                v_ref[s, head] = yh[s * seq_rows:(s + 1) * seq_rows]

    logf = jnp.minimum(logit, 0.0) - jnp.log1p(jnp.exp(-jnp.abs(logit)))
    logf_ref[...] = logf[:, :N_HEADS_F]
    pos = lax.broadcasted_iota(jnp.int32, (tm, LANES), 0) & (seg - 1)
    c = logf
    step = 1
    while step < seg:
        c = c + jnp.where(pos >= step, pltpu.roll(c, step, axis=0), 0.0)
        step *= 2
    if tiles_per_seq > 1:
        @pl.when(i % tiles_per_seq == 0)
        def _():
            carry_ref[...] = c0_ref[0:1, :]

        c = c + carry_ref[...]
        carry_ref[...] = c[tm - 1:tm, :]
    else:
        c = c + c0_ref[...]
    c_ref[...] = c

    c2 = c * LOG2E
    for n in range(n_chunks):
        y = project(n)
        for hh in range(heads_per_chunk):
            head = n * heads_per_chunk + hh
            yh = y[:, hh * HEAD_DIM:(hh + 1) * HEAD_DIM]
            yb = yh.astype(BF16)
            aug = _aug_block(c2[:, head:head + 1], True)
            for s in range(n_seq):
                rows = slice(s * seq_rows, (s + 1) * seq_rows)
                k_ref[s, head] = yh[rows]
                kx_ref[s, head, :, 0:HEAD_DIM] = yb[rows]
                kx_ref[s, head, :, HEAD_DIM:AUG_DIM] = aug[rows]


def _kvf(h, w_kv, w_f, b_f, c0, *, bsz, t_len, tm, emit_vt):
    m, d = h.shape
    tm = min(tm, m)
    seg = min(t_len, tm)
    assert m % tm == 0 and seg & (seg - 1) == 0 and (t_len % tm == 0 or tm % t_len == 0)
    tiles_per_seq = max(t_len // tm, 1)
    n_seq = tm // seg
    row = lambda w: pl.BlockSpec((tm, w), lambda i: (i, 0))
    heads = lambda w: pl.BlockSpec((n_seq, N_HEADS_F, seg, w),
                                   lambda i: (i // tiles_per_seq, 0, i % tiles_per_seq, 0))
    out_specs = [heads(HEAD_DIM), heads(HEAD_DIM), row(N_HEADS_F), row(LANES), heads(AUG_DIM)]
    out_shape = [jax.ShapeDtypeStruct((bsz, N_HEADS_F, t_len, HEAD_DIM), F32),
                 jax.ShapeDtypeStruct((bsz, N_HEADS_F, t_len, HEAD_DIM), F32),
                 jax.ShapeDtypeStruct((m, N_HEADS_F), F32),
                 jax.ShapeDtypeStruct((m, LANES), F32),
                 jax.ShapeDtypeStruct((bsz, N_HEADS_F, t_len, AUG_DIM), BF16)]
    if emit_vt:
        assert tiles_per_seq * tm == t_len
        out_specs.append(pl.BlockSpec(
            (1, N_HEADS_F, 1, VT_ROWS, tm),
            lambda i: (i // tiles_per_seq, 0, i % tiles_per_seq, 0, 0)))
        out_shape.append(
            jax.ShapeDtypeStruct((bsz, N_HEADS_F, tiles_per_seq, VT_ROWS, tm), BF16))
    return pl.pallas_call(
        functools.partial(_kvf_kernel, seg=seg, tiles_per_seq=tiles_per_seq, tn=512,
                          emit_vt=emit_vt),
        grid=(m // tm,),
        in_specs=[row(d), _resident(2 * F_DIM, d), _resident(LANES, d), _resident(1, LANES),
                  row(LANES)],
        out_specs=out_specs,
        out_shape=out_shape,
        scratch_shapes=[pltpu.VMEM((1, LANES), F32)],
        compiler_params=_params("arbitrary"),
        name="kvf",
    )(h, w_kv, w_f, b_f, c0)


def _qproj_kernel(h_ref, wq_ref, c_ref, qx_ref, *, tn):
    h = h_ref[...]
    c = c_ref[...] * LOG2E
    heads_per_chunk = tn // HEAD_DIM
    for n in range(F_DIM // tn):
        y = jnp.dot(h, wq_ref[:, n * tn:(n + 1) * tn],
                    preferred_element_type=F32) * (SCALE * LOG2E)
        for hh in range(heads_per_chunk):
            head = n * heads_per_chunk + hh
            base = head * AUG_DIM
            qx_ref[:, base:base + HEAD_DIM] = y[:, hh * HEAD_DIM:(hh + 1) * HEAD_DIM].astype(BF16)
            qx_ref[:, base + HEAD_DIM:base + AUG_DIM] = _aug_block(c[:, head:head + 1], False)


def _qproj(h, w_q, c, *, tm):
    m, d = h.shape
    tm = min(tm, m)
    assert m % tm == 0
    return pl.pallas_call(
        functools.partial(_qproj_kernel, tn=512),
        grid=(m // tm,),
        in_specs=[pl.BlockSpec((tm, d), lambda i: (i, 0)),
                  _resident(d, F_DIM),
                  pl.BlockSpec((tm, LANES), lambda i: (i, 0))],
        out_specs=pl.BlockSpec((tm, N_HEADS_F * AUG_DIM), lambda i: (i, 0)),
        out_shape=jax.ShapeDtypeStruct((m, N_HEADS_F * AUG_DIM), BF16),
        compiler_params=_params("parallel"),
        name="qproj",
    )(h, w_q, c)


def _fox_prompt_kernel(qx_ref, kx_ref, vt_ref, o_ref, m_ref, acc_ref, sa_ref, sb_ref, ma_ref, mb_ref,
                       *, cw):
    qi = pl.program_id(2)
    tq = qx_ref.shape[0]
    tk = sa_ref.shape[1]
    tkv = vt_ref.shape[4]
    n_chunks = tq // cw
    buf_a, buf_b = (sa_ref, ma_ref), (sb_ref, mb_ref)
    m_ref[...] = jnp.full_like(m_ref, -jnp.inf)
    acc_ref[...] = jnp.zeros_like(acc_ref)

    def scores(kb, visible, buf):
        s_ref, smax_ref = buf
        for c, (keys, key_minus_query) in visible.items():
            k = kx_ref[0, 0, pl.ds(pl.multiple_of(kb * tk, tk), keys), :]
            q = qx_ref[c * cw:(c + 1) * cw, :]
            s = lax.dot_general(k, q, (((1,), (1,)), ((), ())), preferred_element_type=F32)
            if key_minus_query is not None:
                key = lax.broadcasted_iota(jnp.int32, (keys, cw), 0) + key_minus_query
                query = lax.broadcasted_iota(jnp.int32, (keys, cw), 1)
                s = jnp.where(key <= query, s, -jnp.inf)
            s_ref[c, 0:keys, :] = s
            smax_ref[c] = jnp.max(s, axis=0, keepdims=True)

    def update(kb, visible, buf):
        s_ref, smax_ref = buf
        for c, (keys, _) in visible.items():
            m_old = m_ref[c]
            m_new = jnp.maximum(m_old, smax_ref[c])
            alpha = jnp.exp2(m_old - m_new)
            acc = alpha * acc_ref[c]
            for lo in range(0, keys, tkv):
                n = min(tkv, keys - lo)
                p = jnp.exp2(s_ref[c, lo:lo + n, :] - m_new).astype(BF16)
                acc = acc + jnp.dot(vt_ref[0, 0, kb * (tk // tkv) + lo // tkv, :, 0:n], p,
                                    preferred_element_type=F32)
            acc_ref[c] = acc
            m_ref[c] = m_new

    every = {c: (tk, None) for c in range(n_chunks)}
    own = []
    for jj in range(2):
        visible = {}
        for c in range(n_chunks):
            k_lo, q_lo = jj * tk, c * cw
            keys = min(tk, q_lo + cw - k_lo)
            if keys > 0:
                visible[c] = (keys, k_lo - q_lo if k_lo + keys - 1 > q_lo else None)
        own.append(visible)

    @pl.when(qi == 0)
    def _():
        scores(0, own[0], buf_a)
        scores(1, own[1], buf_b)
        update(0, own[0], buf_a)
        update(1, own[1], buf_b)

    @pl.when(qi > 0)
    def _():
        n = 2 * qi
        scores(0, every, buf_a)

        def pair(j):
            scores(j, every, buf_b)
            update(j - 1, every, buf_a)
            scores(j + 1, every, buf_a)
            update(j, every, buf_b)

        def two_pairs(t, carry):
            pair(4 * t + 1)
            pair(4 * t + 3)
            return carry

        n_pairs = qi - 1
        lax.fori_loop(0, n_pairs // 2, two_pairs, 0)

        @pl.when(n_pairs % 2 == 1)
        def _():
            pair(n - 3)

        scores(n - 1, every, buf_b)
        update(n - 2, every, buf_a)
        scores(n, own[0], buf_a)
        update(n - 1, every, buf_b)
        scores(n + 1, own[1], buf_b)
        update(n, own[0], buf_a)
        update(n + 1, own[1], buf_b)

    for c in range(n_chunks):
        acc = acc_ref[c]
        o = acc[0:HEAD_DIM, :] / acc[HEAD_DIM:HEAD_DIM + 1, :]
        o_ref[c * cw:(c + 1) * cw, :] = o.T.astype(o_ref.dtype)


def _fox_prompt(qx, kx, vt, *, bsz, t_len, tq, cw):
    m = bsz * t_len
    nq = t_len // tq
    tk = tq // 2
    tkv = vt.shape[4]
    n_chunks = tq // cw
    assert t_len % tq == 0 and tk % tkv == 0 and tq % cw == 0
    return pl.pallas_call(
        functools.partial(_fox_prompt_kernel, cw=cw),
        grid=(bsz, N_HEADS_F, nq),
        in_specs=[pl.BlockSpec((tq, AUG_DIM), lambda b, h, qi: (b * nq + qi, h)),
                  pl.BlockSpec((1, 1, t_len, AUG_DIM), lambda b, h, qi: (b, h, 0, 0)),
                  pl.BlockSpec((1, 1, t_len // tkv, VT_ROWS, tkv),
                               lambda b, h, qi: (b, h, 0, 0, 0))],
        out_specs=pl.BlockSpec((tq, HEAD_DIM), lambda b, h, qi: (b * nq + qi, h)),
        out_shape=jax.ShapeDtypeStruct((m, F_DIM), BF16),
        scratch_shapes=[pltpu.VMEM((n_chunks, 1, cw), F32),
                        pltpu.VMEM((n_chunks, VT_ROWS, cw), F32),
                        pltpu.VMEM((n_chunks, tk, cw), F32), pltpu.VMEM((n_chunks, tk, cw), F32),
                        pltpu.VMEM((n_chunks, 1, cw), F32), pltpu.VMEM((n_chunks, 1, cw), F32)],
        compiler_params=_params("parallel", "parallel", "arbitrary"),
        name="fox_prompt",
    )(qx, kx, vt)


def _fox_cached_kernel(qx_ref, kxn_ref, vn_ref, cn_ref, ck_ref, cv_ref, cp_ref, o_ref,
                       m_ref, l_ref, acc_ref, cq_ref, smax_ref, s_ref):
    b = pl.program_id(0)
    ki = pl.program_id(1)
    t_new = qx_ref.shape[0]

    @pl.when(ki == 0)
    def _():
        m_ref[...] = jnp.full_like(m_ref, -jnp.inf)
        l_ref[...] = jnp.zeros_like(l_ref)
        acc_ref[...] = jnp.zeros_like(acc_ref)
        for h in range(N_HEADS_F):
            cq_ref[h] = jnp.broadcast_to(cn_ref[:, h:h + 1] * LOG2E, (t_new, LANES))

    def update(h, s, s_max, v):
        n = s.shape[1]
        across = (lambda x: _tile_lanes(x, n // LANES)) if n >= LANES else (lambda x: x[:, :n])
        m_old = m_ref[h]
        m_new = jnp.maximum(m_old, s_max)
        alpha = jnp.exp2(m_old - m_new)
        p = jnp.exp2(s - across(m_new))
        l_ref[h] = alpha * l_ref[h] + jnp.sum(p, axis=-1, keepdims=True)
        acc_ref[h] = alpha * acc_ref[h] + jnp.dot(p.astype(BF16), v, preferred_element_type=F32)
        m_ref[h] = m_new

    tk = ck_ref.shape[2]
    for h in range(N_HEADS_F):
        q = qx_ref[:, h * AUG_DIM:h * AUG_DIM + HEAD_DIM]
        k = ck_ref[0, h].astype(BF16)
        s = lax.dot_general(q, k, (((1,), (1,)), ((), ())), preferred_element_type=F32)
        decay = _tile_lanes(cq_ref[h], tk // LANES) - cp_ref[h, pl.ds(b, 1), :] * LOG2E
        s = s + decay
        s_ref[h] = s
        smax_ref[h] = jnp.broadcast_to(jnp.max(s, axis=-1, keepdims=True), (t_new, LANES))
    for h in range(N_HEADS_F):
        update(h, s_ref[h], smax_ref[h], cv_ref[0, h].astype(BF16))

    @pl.when(ki == pl.num_programs(1) - 1)
    def _():
        row = lax.broadcasted_iota(jnp.int32, (t_new, t_new), 0)
        col = lax.broadcasted_iota(jnp.int32, (t_new, t_new), 1)
        for h in range(N_HEADS_F):
            xcols = slice(h * AUG_DIM, (h + 1) * AUG_DIM)
            s = lax.dot_general(qx_ref[:, xcols], kxn_ref[0, h], (((1,), (1,)), ((), ())),
                                preferred_element_type=F32)
            s = jnp.where(col <= row, s, -jnp.inf)
            s_ref[h, :, 0:t_new] = s
            smax_ref[h] = jnp.broadcast_to(jnp.max(s, axis=-1, keepdims=True), (t_new, LANES))
        for h in range(N_HEADS_F):
            cols = slice(h * HEAD_DIM, (h + 1) * HEAD_DIM)
            update(h, s_ref[h, :, 0:t_new], smax_ref[h], vn_ref[0, h].astype(BF16))
            o_ref[:, cols] = (acc_ref[h] / l_ref[h]).astype(o_ref.dtype)


def _fox_cached(qx, kx_new, v_new, c_new, cache_k, cache_v, c_past, *, bsz, t_len, tk):
    m = bsz * t_len
    past = cache_k.shape[2]
    assert past % tk == 0 and past > 0
    seq = lambda w: pl.BlockSpec((t_len, w), lambda b, ki: (b, 0))
    new = lambda w: pl.BlockSpec((1, N_HEADS_F, t_len, w), lambda b, ki: (b, 0, 0, 0))
    cache = pl.BlockSpec((1, N_HEADS_F, tk, HEAD_DIM), lambda b, ki: (b, 0, ki, 0))
    return pl.pallas_call(
        _fox_cached_kernel,
        grid=(bsz, past // tk),
        in_specs=[seq(N_HEADS_F * AUG_DIM), new(AUG_DIM), new(HEAD_DIM), seq(LANES),
                  cache, cache,
                  pl.BlockSpec((N_HEADS_F, bsz, tk), lambda b, ki: (0, 0, ki))],
        out_specs=seq(F_DIM),
        out_shape=jax.ShapeDtypeStruct((m, F_DIM), BF16),
        scratch_shapes=[pltpu.VMEM((N_HEADS_F, t_len, LANES), F32) for _ in range(5)]
        + [pltpu.VMEM((N_HEADS_F, t_len, tk), F32)],
        compiler_params=_params("parallel", "arbitrary"),
        name="fox_cached",
    )(qx, kx_new, v_new, c_new, cache_k, cache_v, c_past)


def _cumsum_lanes_kernel(x_ref, o_ref):
    rows, n = x_ref.shape
    r = lax.broadcasted_iota(jnp.int32, (LANES, LANES), 0)
    c = lax.broadcasted_iota(jnp.int32, (LANES, LANES), 1)
    upper = jnp.where(r <= c, 1.0, 0.0).astype(BF16)
    carry = jnp.zeros((rows, 1), F32)
    for j in range(n // LANES):
        cols = slice(j * LANES, (j + 1) * LANES)
        hi, mid, lo = _split3(x_ref[:, cols])
        local = (jnp.dot(hi.astype(BF16), upper, preferred_element_type=F32)
                 + jnp.dot(mid.astype(BF16), upper, preferred_element_type=F32)
                 + jnp.dot(lo.astype(BF16), upper, preferred_element_type=F32))
        o_ref[:, cols] = local + carry
        carry = carry + local[:, LANES - 1:LANES]


def _cumsum_lanes(x):
    rows, n = x.shape
    return pl.pallas_call(
        _cumsum_lanes_kernel,
        grid=(1,),
        in_specs=[pl.BlockSpec((rows, n), lambda i: (0, 0))],
        out_specs=pl.BlockSpec((rows, n), lambda i: (0, 0)),
        out_shape=jax.ShapeDtypeStruct((rows, n), F32),
        compiler_params=_params("arbitrary"),
        name="cumsum_lanes",
    )(x)


class _Tiles(NamedTuple):
    proj_rows: int
    proj_cols: int
    rest_rows: int
    rest_cols: int
    head_rows: int
    tail_rows: int
    attn_queries: int
    attn_chunk: int
    cache_keys: int


def _tiles(t_len):
    tail_rows = 512 if t_len >= 512 else 256
    return _Tiles(proj_rows=2048, proj_cols=512, rest_rows=2048, rest_cols=512, head_rows=512,
                  tail_rows=tail_rows, attn_queries=1024, attn_chunk=256, cache_keys=1024)


def _trunk(x, conv_state, mem_k, mem_v, past, w):
    bsz, t_len, d = x.shape
    m = bsz * t_len
    x0 = x.reshape(m, d)
    tiles = _tiles(t_len)

    proj = _norm_proj(x0, w["g_norm"][0], w["w_in_a"], tm=tiles.proj_rows, tn=tiles.proj_cols,
                      out_dtype=BF16)
    x1, h1, h_kv, new_state = _tail_a(
        proj, conv_state, mem_k, mem_v, w["conv_w"], w["w_out"][0], x0,
        jnp.stack([w["g_norm"][1], w["g_kv"]]), layer=0, bsz=bsz, t_len=t_len,
        tt=tiles.tail_rows)

    if past is None:
        c0 = jnp.zeros((m, LANES), F32)
    else:
        cache_k, cache_v, cache_logf = past
        past_len = cache_k.shape[1]
        cache_k = jnp.transpose(cache_k, (0, 2, 1, 3))
        cache_v = jnp.transpose(cache_v, (0, 2, 1, 3))
        logf_t = jnp.transpose(cache_logf, (2, 0, 1)).reshape(N_HEADS_F * bsz, past_len)
        c_past = _cumsum_lanes(logf_t).reshape(N_HEADS_F, bsz, past_len)
        c_end = jnp.pad(c_past[:, :, past_len - 1].T, ((0, 0), (0, LANES - N_HEADS_F)))
        c0 = jnp.repeat(c_end, t_len, axis=0)
    kvf_out = _kvf(h_kv, w["w_kv"], w["w_f"], w["b_f"], c0,
                   bsz=bsz, t_len=t_len, tm=tiles.head_rows, emit_vt=past is None)
    k_new, v_new, logf, c, kx = kvf_out[:5]
    qx = _qproj(h1, w["w_q"], c, tm=tiles.head_rows)
    rest = _proj(h1, w["w_in_b"], first_col=F_DIM, tm=tiles.rest_rows, tn=tiles.rest_cols,
                 out_dtype=BF16)

    if past is None:
        o = _fox_prompt(qx, kx, kvf_out[5], bsz=bsz, t_len=t_len, tq=tiles.attn_queries,
                        cw=tiles.attn_chunk)
    else:
        o = _fox_cached(qx, kx, v_new, c, cache_k, cache_v, c_past,
                        bsz=bsz, t_len=t_len, tk=tiles.cache_keys)
    y = _tail_b(o, rest, mem_k, mem_v, w["w_out"][1], x1, w["g_final"][None],
                layer=1, bsz=bsz, t_len=t_len, tt=tiles.tail_rows)
    return (y.reshape(bsz, t_len, d), new_state[None],
            jnp.transpose(k_new, (0, 2, 1, 3)), jnp.transpose(v_new, (0, 2, 1, 3)),
            logf.reshape(bsz, t_len, N_HEADS_F))


def kernel(x_prompt, x_sample, state_conv, cache_k, cache_v, cache_logf, cache_mem_k, cache_mem_v,
           mem_prompt, g_norm, w_in_a, conv_w, w_in_b, w_out, g_mem, w_mem_kv, g_kv, w_kvf, b_f,
           g_final):
    depth = g_norm.shape[0]
    bp = x_prompt.shape[0]
    bs = x_sample.shape[0]
    w = {
        "g_norm": g_norm, "g_kv": g_kv, "g_final": g_final, "conv_w": conv_w[0],
        "w_in_a": w_in_a[0],
        "w_q": w_in_b[0][:, :F_DIM].astype(BF16),
        "w_in_b": w_in_b[0],
        "w_out": w_out.astype(BF16),
        "w_kv": w_kvf.T[:2 * F_DIM].astype(BF16),
        "w_f": jnp.pad(w_kvf.T[2 * F_DIM:], ((0, LANES - N_HEADS_F), (0, 0))).astype(BF16),
        "b_f": jnp.pad(b_f, (0, LANES - N_HEADS_F)).reshape(1, LANES),
    }

    mem_rows = mem_prompt.reshape(bp * N_MEM, D_MODEL)
    mem_kv = [_norm_proj(mem_rows, g_mem[i], w_mem_kv[i], tm=512, tn=512)
              for i in range(depth)]
    p_mem_k = jnp.stack([a[:, :M_DIM] for a in mem_kv]).reshape(depth, bp, N_MEM, N_HEADS_M, HEAD_DIM)
    p_mem_v = jnp.stack([a[:, M_DIM:] for a in mem_kv]).reshape(depth, bp, N_MEM, N_HEADS_M, HEAD_DIM)

    zero_conv = jnp.zeros((bp, CONV_W - 1, CONV_DIM), F32)
    y_p, p_state, p_k, p_v, p_logf = _trunk(
        x_prompt, zero_conv, p_mem_k.reshape(depth, bp, N_MEM * N_HEADS_M, HEAD_DIM),
        p_mem_v.reshape(depth, bp, N_MEM * N_HEADS_M, HEAD_DIM), None, w)
    y_s, s_state, s_k, s_v, s_logf = _trunk(
        x_sample, state_conv[0], cache_mem_k.reshape(depth, bs, N_MEM * N_HEADS_M, HEAD_DIM),
        cache_mem_v.reshape(depth, bs, N_MEM * N_HEADS_M, HEAD_DIM),
        (cache_k, cache_v, cache_logf), w)
    return (y_p, y_s, p_state, p_k, p_v, p_logf, p_mem_k, p_mem_v, s_state, s_k, s_v, s_logf)
```
